```python
import math
import jax, jax.numpy as jnp
from jax import lax
import numpy as np

D_MODEL = 2048
BATCH = 1
SEQ = 16384
DEPTH = 1

GRID_W = 64
NA_HEADS = 8
NA_HEAD_DIM = 128
NA_WIDTH = NA_HEADS * NA_HEAD_DIM
NA_MAX_ROWS = 8
NA_COLS = 16
S5_GROUP = 16
S5_GROUPS = 64
S5_WIDTH = S5_GROUP * S5_GROUPS
S5_STATE = 64
D_FF = 256 * (-(-8 * D_MODEL // (3 * 256)))
IN_COLS = 3 * NA_WIDTH + S5_WIDTH + 2 * D_MODEL
LN_EPS = 1e-5
DEEPNORM_ALPHA = (2.0 * DEPTH) ** 0.25
DEEPNORM_BETA = (8.0 * DEPTH) ** -0.25

kernel_name = "hybrid_natten_s5_gated_deepnorm"


def _layer_norm(x, g, b):
    xf = x.astype(jnp.float32)
    mu = jnp.mean(xf, axis=-1, keepdims=True)
    var = jnp.mean(jnp.square(xf - mu), axis=-1, keepdims=True)
    y = (xf - mu) * lax.rsqrt(var + LN_EPS) * g.astype(jnp.float32) + b.astype(jnp.float32)
    return y.astype(x.dtype)


def _neighbourhood_attention(q, k, v, rpb):
    bsz, seq = q.shape[0], q.shape[1]
    rows = seq // GRID_W
    kh = min(NA_MAX_ROWS, rows)
    kw = NA_COLS
    grid = lambda t: t.reshape(bsz, rows, GRID_W, NA_HEADS, NA_HEAD_DIM)
    qg, kg, vg = grid(q), grid(k), grid(v)
    cols = jnp.arange(GRID_W)
    col_start = jnp.clip(cols - kw // 2, 0, GRID_W - kw)
    col_mask = (cols[None, :] >= col_start[:, None]) & (cols[None, :] < col_start[:, None] + kw)
    col_idx = jnp.clip(cols[None, :] - cols[:, None], -(kw - 1), kw - 1) + (NA_COLS - 1)
    rpb32 = rpb.astype(jnp.float32)
    scale = NA_HEAD_DIM ** -0.5

    def row_block(r):
        r0 = jnp.clip(r - kh // 2, 0, rows - kh)
        q_r = lax.dynamic_index_in_dim(qg, r, axis=1, keepdims=False)
        k_b = lax.dynamic_slice_in_dim(kg, r0, kh, axis=1)
        v_b = lax.dynamic_slice_in_dim(vg, r0, kh, axis=1)
        s = jnp.einsum("bqhd,bikhd->bhqik", q_r, k_b).astype(jnp.float32) * scale
        row_idx = r0 + jnp.arange(kh) - r + (NA_MAX_ROWS - 1)
        bias = rpb32[:, row_idx][:, :, col_idx].transpose(0, 2, 1, 3)
        s = jnp.where(col_mask[None, None, :, None, :], s + bias[None], -1e30)
        p = jax.nn.softmax(s.reshape(bsz, NA_HEADS, GRID_W, kh * GRID_W), axis=-1)
        p = p.reshape(s.shape).astype(v.dtype)
        return jnp.einsum("bhqik,bikhd->bqhd", p, v_b)

    out = lax.map(row_block, jnp.arange(rows))
    return out.transpose(1, 0, 2, 3, 4).reshape(bsz, seq, NA_WIDTH)


def _s5_scan(u, a_re, a_im, log_dt, b_re, b_im, c_re, c_im, reverse):
    f32 = jnp.float32
    lam = lax.complex(a_re.astype(f32), a_im.astype(f32))
    dt = jnp.exp(log_dt.astype(f32))[:, None]
    lam_bar = jnp.exp(lam * dt)
    b = lax.complex(b_re.astype(f32), b_im.astype(f32))
    b_bar = ((lam_bar - 1.0) / lam)[..., None] * b
    bu = jnp.einsum("blgc,gpc->blgp", u.astype(jnp.complex64), b_bar)
    a = jnp.broadcast_to(lam_bar, bu.shape)

    def combine(e1, e2):
        a1, s1 = e1
        a2, s2 = e2
        return a1 * a2, a2 * s1 + s2

    _, states = lax.associative_scan(combine, (a, bu), reverse=reverse, axis=1)
    return (jnp.einsum("blgp,gcp->blgc", states.real, c_re.astype(f32))
            - jnp.einsum("blgp,gcp->blgc", states.imag, c_im.astype(f32)))


def setup_inputs(seed: int = 0) -> dict:
    key = jax.random.key(seed)
    ks = jax.random.split(key, 32)
    f32 = jnp.float32
    L = DEPTH
    nrm = lambda k, shape, sc: jax.random.normal(k, shape, f32) * sc
    n = jnp.arange(S5_STATE, dtype=f32)
    return {
        "x": nrm(ks[0], (BATCH, SEQ, D_MODEL), 1.0),
        "w_in": nrm(ks[1], (L, D_MODEL, IN_COLS), D_MODEL ** -0.5),
        "b_gate": nrm(ks[2], (L, 2 * D_MODEL), 0.02),
        "na_rpb": nrm(ks[3], (L, NA_HEADS, 2 * NA_MAX_ROWS - 1, 2 * NA_COLS - 1), 0.1),
        "w_na_out": nrm(ks[4], (L, NA_WIDTH, D_MODEL), NA_WIDTH ** -0.5),
        "s5_a_re": -0.5 + nrm(ks[5], (L, 2, S5_GROUPS, S5_STATE), 0.01),
        "s5_a_im": math.pi * n + nrm(ks[6], (L, 2, S5_GROUPS, S5_STATE), 0.01),
        "s5_log_dt": jax.random.uniform(ks[7], (L, 2, S5_GROUPS), f32, math.log(1e-3), math.log(1e-1)),
        "s5_b_re": nrm(ks[8], (L, 2, S5_GROUPS, S5_STATE, S5_GROUP), (2 * S5_GROUP) ** -0.5),
        "s5_b_im": nrm(ks[9], (L, 2, S5_GROUPS, S5_STATE, S5_GROUP), (2 * S5_GROUP) ** -0.5),
        "s5_c_re": nrm(ks[10], (L, 2, S5_GROUPS, S5_GROUP, S5_STATE), (2 * S5_STATE) ** -0.5),
        "s5_c_im": nrm(ks[11], (L, 2, S5_GROUPS, S5_GROUP, S5_STATE), (2 * S5_STATE) ** -0.5),
        "s5_d": nrm(ks[12], (L, S5_GROUPS, S5_GROUP), 1.0),
        "w_glu": nrm(ks[13], (L, S5_WIDTH, S5_WIDTH), S5_WIDTH ** -0.5),
        "b_glu": nrm(ks[14], (L, S5_WIDTH), 0.02),
        "w_s5_out": nrm(ks[15], (L, S5_WIDTH, D_MODEL), S5_WIDTH ** -0.5),
        "w_out": nrm(ks[16], (L, D_MODEL, D_MODEL), DEEPNORM_BETA * D_MODEL ** -0.5),
        "ln1_g": 1.0 + nrm(ks[17], (L, D_MODEL), 0.02),
        "ln1_b": nrm(ks[18], (L, D_MODEL), 0.02),
        "w_ffn_gate": nrm(ks[19], (L, D_MODEL, D_FF), D_MODEL ** -0.5),
        "w_ffn_up": nrm(ks[20], (L, D_MODEL, D_FF), D_MODEL ** -0.5),
        "w_ffn_down": nrm(ks[21], (L, D_FF, D_MODEL), DEEPNORM_BETA * D_FF ** -0.5),
        "ln2_g": 1.0 + nrm(ks[22], (L, D_MODEL), 0.02),
        "ln2_b": nrm(ks[23], (L, D_MODEL), 0.02),
    }


def reference(x, w_in, b_gate, na_rpb, w_na_out, s5_a_re, s5_a_im, s5_log_dt, s5_b_re, s5_b_im,
              s5_c_re, s5_c_im, s5_d, w_glu, b_glu, w_s5_out, w_out, ln1_g, ln1_b,
              w_ffn_gate, w_ffn_up, w_ffn_down, ln2_g, ln2_b):
    bsz, seq, _ = x.shape
    splits = [NA_WIDTH, 2 * NA_WIDTH, 3 * NA_WIDTH, 3 * NA_WIDTH + S5_WIDTH,
              3 * NA_WIDTH + S5_WIDTH + D_MODEL]
    for l in range(DEPTH):
        proj = x @ w_in[l]
        q, k, v, u, g_att, g_ssm = jnp.split(proj, splits, axis=-1)
        g_att = jax.nn.sigmoid(g_att + b_gate[l, :D_MODEL])
        g_ssm = jax.nn.sigmoid(g_ssm + b_gate[l, D_MODEL:])

        heads = lambda t: t.reshape(bsz, seq, NA_HEADS, NA_HEAD_DIM)
        y_att = _neighbourhood_attention(heads(q), heads(k), heads(v), na_rpb[l]) @ w_na_out[l]

        ug = u.astype(jnp.float32).reshape(bsz, seq, S5_GROUPS, S5_GROUP)
        y_f = _s5_scan(ug, s5_a_re[l, 0], s5_a_im[l, 0], s5_log_dt[l, 0], s5_b_re[l, 0], s5_b_im[l, 0],
                       s5_c_re[l, 0], s5_c_im[l, 0], reverse=False)
        y_b = _s5_scan(ug, s5_a_re[l, 1], s5_a_im[l, 1], s5_log_dt[l, 1], s5_b_re[l, 1], s5_b_im[l, 1],
                       s5_c_re[l, 1], s5_c_im[l, 1], reverse=True)
        y_s = (y_f + y_b + s5_d[l].astype(jnp.float32) * ug).reshape(bsz, seq, S5_WIDTH).astype(x.dtype)
        y_s = jax.nn.gelu(y_s)
        y_s = y_s * jax.nn.sigmoid(y_s @ w_glu[l] + b_glu[l])
        y_ssm = y_s @ w_s5_out[l]

        mix = (g_att * y_att + g_ssm * y_ssm) @ w_out[l]
        h = _layer_norm(DEEPNORM_ALPHA * x + mix, ln1_g[l], ln1_b[l])

        ff = (jax.nn.silu(h @ w_ffn_gate[l]) * (h @ w_ffn_up[l])) @ w_ffn_down[l]
        x = _layer_norm(DEEPNORM_ALPHA * h + ff, ln2_g[l], ln2_b[l])
    return x
```

```python
import functools
import math

import jax
import jax.numpy as jnp
from jax import lax
from jax.experimental import pallas as pl
from jax.experimental.pallas import tpu as pltpu

F32 = jnp.float32
BF16 = jnp.bfloat16

GRID_W = 64
NA_HEADS = 8
NA_HEAD_DIM = 128
NA_WIDTH = NA_HEADS * NA_HEAD_DIM
NA_ROWS = 8
NA_COLS = 16
S5_GROUP = 16
S5_GROUPS = 64
S5_WIDTH = S5_GROUP * S5_GROUPS
S5_STATE = 64
LN_EPS = 1e-5
MASK_VALUE = -1e30

S5_CHUNK = 16
S5_CHUNK_WIDTH = S5_CHUNK * S5_GROUP
S5_STATE_WIDTH = 4 * S5_STATE
SCAN_GROUPS = 8

ATT_Q_ROWS = 8
ATT_K_ROWS = 16
ATT_Q = ATT_Q_ROWS * GRID_W
ATT_K = ATT_K_ROWS * GRID_W
ATT_KBLK = 256

VMEM_LIMIT = 56 * 1024 * 1024


def _params(sem):
    return pltpu.CompilerParams(dimension_semantics=sem, vmem_limit_bytes=VMEM_LIMIT)


def _layer_norm_rows(y, g, b):
    mu = jnp.mean(y, axis=-1, keepdims=True)
    d = y - mu
    var = jnp.mean(d * d, axis=-1, keepdims=True)
    return d * lax.rsqrt(var + LN_EPS) * g + b


def _inproj_kernel(x_ref, w_ref, s_ref, b_ref, o_ref, xb_ref, *, gate_tile0):
    j = pl.program_id(1)

    @pl.when(j == 0)
    def _():
        xb_ref[...] = x_ref[...].astype(BF16)

    @pl.when(j < gate_tile0)
    def _():
        acc = jnp.dot(xb_ref[...], w_ref[...], preferred_element_type=F32)
        o_ref[...] = (acc * s_ref[...]).astype(o_ref.dtype)

    @pl.when(j >= gate_tile0)
    def _():
        acc = jnp.dot(xb_ref[...], w_ref[...], preferred_element_type=F32)
        o_ref[...] = jax.nn.sigmoid(acc + b_ref[...]).astype(o_ref.dtype)


def _inproj(x, w, col_scale, col_bias, gate_col0, tm, tn):
    m, k = x.shape
    n = w.shape[1]
    return pl.pallas_call(
        functools.partial(_inproj_kernel, gate_tile0=gate_col0 // tn),
        grid=(m // tm, n // tn),
        in_specs=[
            pl.BlockSpec((tm, k), lambda i, j: (i, 0)),
            pl.BlockSpec((k, tn), lambda i, j: (0, j)),
            pl.BlockSpec((1, tn), lambda i, j: (0, j)),
            pl.BlockSpec((1, tn), lambda i, j: (0, j)),
        ],
        out_specs=pl.BlockSpec((tm, tn), lambda i, j: (i, j)),
        out_shape=jax.ShapeDtypeStruct((m, n), BF16),
        scratch_shapes=[pltpu.VMEM((tm, k), BF16)],
        compiler_params=_params(("parallel", "arbitrary")),
        name="inproj",
    )(x, w, col_scale, col_bias)


def _attn_kernel(q_ref, k0, k1, k2, k3, v0, v1, v2, v3, bias_ref, o_ref):
    k = jnp.concatenate([k0[...], k1[...], k2[...], k3[...]], axis=0)
    v = jnp.concatenate([v0[...], v1[...], v2[...], v3[...]], axis=0)
    s = lax.dot_general(q_ref[...], k, (((1,), (1,)), ((), ())), preferred_element_type=F32)
    s = s + bias_ref[...]
    m = jnp.max(s, axis=-1, keepdims=True)
    p = jnp.exp(s - m)
    l = jnp.sum(p, axis=-1, keepdims=True)
    o = jnp.dot(p.astype(BF16), v, preferred_element_type=F32)
    o_ref[...] = (o / l).astype(o_ref.dtype)


def _attention(proj, bias, seq):
    nb = seq // ATT_Q
    n_kblk = seq // ATT_KBLK
    per_blk = ATT_K // ATT_KBLK

    def kstart(b):
        return jnp.clip(2 * b - 1, 0, n_kblk - per_blk)

    def variant(b):
        return jnp.where(b == 0, 0, jnp.where(b == nb - 1, 2, 1))

    def kv_spec(col0, t):
        return pl.BlockSpec((ATT_KBLK, NA_HEAD_DIM), lambda h, b: (kstart(b) + t, col0 + h))

    in_specs = [pl.BlockSpec((ATT_Q, NA_HEAD_DIM), lambda h, b: (b, h))]
    in_specs += [kv_spec(NA_HEADS, t) for t in range(per_blk)]
    in_specs += [kv_spec(2 * NA_HEADS, t) for t in range(per_blk)]
    in_specs += [pl.BlockSpec((None, None, ATT_Q, ATT_K), lambda h, b: (variant(b), h, 0, 0))]
    return pl.pallas_call(
        _attn_kernel,
        grid=(NA_HEADS, nb),
        in_specs=in_specs,
        out_specs=pl.BlockSpec((ATT_Q, NA_HEAD_DIM), lambda h, b: (b, h)),
        out_shape=jax.ShapeDtypeStruct((seq, NA_WIDTH), BF16),
        compiler_params=_params(("parallel", "arbitrary")),
        name="na_attention",
    )(*([proj] * (1 + 2 * per_blk)), bias)


def _attention_bias(rpb, rows):
    nb = rows // ATT_Q_ROWS
    cols = jnp.arange(GRID_W)
    col_start = jnp.clip(cols - NA_COLS // 2, 0, GRID_W - NA_COLS)
    col_ok = (cols[None, :] >= col_start[:, None]) & (cols[None, :] < col_start[:, None] + NA_COLS)
    col_idx = jnp.clip(cols[None, :] - cols[:, None], -(NA_COLS - 1), NA_COLS - 1) + (NA_COLS - 1)
    by_col = jnp.take(rpb.astype(F32), col_idx.reshape(-1), axis=2)
    by_col = by_col.reshape(NA_HEADS, 2 * NA_ROWS - 1, GRID_W, GRID_W)
    out = []
    for b in (0, 1, nb - 1):
        qr = b * ATT_Q_ROWS + jnp.arange(ATT_Q_ROWS)
        base = min(max(b * ATT_Q_ROWS - NA_ROWS // 2, 0), rows - ATT_K_ROWS)
        kr = base + jnp.arange(ATT_K_ROWS)
        r0 = jnp.clip(qr - NA_ROWS // 2, 0, rows - NA_ROWS)
        row_ok = (kr[None, :] >= r0[:, None]) & (kr[None, :] < r0[:, None] + NA_ROWS)
        row_idx = jnp.clip(kr[None, :] - qr[:, None] + (NA_ROWS - 1), 0, 2 * NA_ROWS - 2)
        t = jnp.take(by_col, row_idx.reshape(-1), axis=1)
        t = t.reshape(NA_HEADS, ATT_Q_ROWS, ATT_K_ROWS, GRID_W, GRID_W)
        ok = row_ok[:, :, None, None] & col_ok[None, None, :, :]
        t = jnp.where(ok[None], t, MASK_VALUE)
        out.append(t.transpose(0, 1, 3, 2, 4).reshape(NA_HEADS, ATT_Q, ATT_K))
    return jnp.stack(out)


def _s5_tables(a_re, a_im, log_dt, b_re, b_im, c_re, c_im, d):
    t_len = S5_CHUNK
    n = jnp.arange(t_len + 1, dtype=F32)

    def direction(k):
        ar, ai = a_re[k].astype(F32), a_im[k].astype(F32)
        dt = jnp.exp(log_dt[k].astype(F32))[:, None]
        zr, zi = ar * dt, ai * dt
        mag = jnp.exp(zr[..., None] * n)
        pr, pi = mag * jnp.cos(zi[..., None] * n), mag * jnp.sin(zi[..., None] * n)
        lr, li = pr[..., 1] - 1.0, pi[..., 1]
        den = ar * ar + ai * ai
        fr, fi = (lr * ar + li * ai) / den, (li * ar - lr * ai) / den
        br, bi = b_re[k].astype(F32), b_im[k].astype(F32)
        bbr = fr[..., None] * br - fi[..., None] * bi
        bbi = fr[..., None] * bi + fi[..., None] * br
        return pr, pi, bbr, bbi, c_re[k].astype(F32), c_im[k].astype(F32)

    def cmul(xr, xi, yr, yi):
        return xr * yr - xi * yi, xr * yi + xi * yr

    w_cols, m_rows, kern = [], [], []
    for k in range(2):
        pr, pi, bbr, bbi, cr, ci = direction(k)
        wp_r = pr[..., :t_len][..., ::-1] if k == 0 else pr[..., :t_len]
        wp_i = pi[..., :t_len][..., ::-1] if k == 0 else pi[..., :t_len]
        wr, wi = cmul(wp_r[:, :, :, None], wp_i[:, :, :, None], bbr[:, :, None, :], bbi[:, :, None, :])
        w_cols.append((wr.transpose(0, 2, 3, 1).reshape(S5_GROUPS, S5_CHUNK_WIDTH, S5_STATE),
                       wi.transpose(0, 2, 3, 1).reshape(S5_GROUPS, S5_CHUNK_WIDTH, S5_STATE)))
        ep_r = pr[..., 1:] if k == 0 else pr[..., 1:][..., ::-1]
        ep_i = pi[..., 1:] if k == 0 else pi[..., 1:][..., ::-1]
        er, ei = cmul(cr[:, :, :, None], ci[:, :, :, None], ep_r[:, None, :, :], ep_i[:, None, :, :])
        m_rows.append((er.transpose(0, 2, 3, 1).reshape(S5_GROUPS, S5_STATE, S5_CHUNK_WIDTH),
                       (-ei).transpose(0, 2, 3, 1).reshape(S5_GROUPS, S5_STATE, S5_CHUNK_WIDTH)))
        xr, xi = cmul(cr[:, :, :, None], ci[:, :, :, None], pr[:, None, :, :t_len], pi[:, None, :, :t_len])
        kern.append(jnp.einsum("gopt,gpc->gtoc", xr, bbr, precision=lax.Precision.HIGHEST)
                    - jnp.einsum("gopt,gpc->gtoc", xi, bbi, precision=lax.Precision.HIGHEST))
        if k == 0:
            step_r, step_i = [pr[..., t_len]], [pi[..., t_len]]
        else:
            step_r.append(pr[..., t_len]); step_i.append(pi[..., t_len])

    w_state = jnp.concatenate([w_cols[0][0], w_cols[1][0], w_cols[0][1], w_cols[1][1]], axis=-1)
    m_state = jnp.concatenate([m_rows[0][0], m_rows[1][0], m_rows[0][1], m_rows[1][1]], axis=1)
    eye = jnp.eye(S5_GROUP, dtype=F32)
    k0 = kern[0][:, 0] + kern[1][:, 0] + d.astype(F32)[:, :, None] * eye[None]
    lags = jnp.concatenate([kern[1][:, 1:][:, ::-1], k0[:, None], kern[0][:, 1:]], axis=1)
    steps = jnp.arange(t_len)
    lag_idx = steps[None, :] - steps[:, None] + (t_len - 1)
    toe = jnp.take(lags, lag_idx.reshape(-1), axis=1).reshape(S5_GROUPS, t_len, t_len, S5_GROUP, S5_GROUP)
    toe = toe.transpose(0, 1, 4, 2, 3).reshape(S5_GROUPS, S5_CHUNK_WIDTH, S5_CHUNK_WIDTH)
    m_out = jnp.concatenate([toe, m_state], axis=1)
    a_step_re = jnp.concatenate(step_r, axis=-1)
    a_step_im = jnp.concatenate(step_i, axis=-1)
    return w_state, m_out, a_step_re, a_step_im


def _s5_state_kernel(u_ref, w_ref, o_ref):
    o_ref[...] = jnp.dot(u_ref[...], w_ref[...], preferred_element_type=F32)


def _s5_chunk_states(ug, w_state):
    g, nc, cw = ug.shape
    return pl.pallas_call(
        _s5_state_kernel,
        grid=(g,),
        in_specs=[pl.BlockSpec((None, nc, cw), lambda i: (i, 0, 0)),
                  pl.BlockSpec((None, cw, S5_STATE_WIDTH), lambda i: (i, 0, 0))],
        out_specs=pl.BlockSpec((nc, S5_STATE_WIDTH), lambda i: (0, i)),
        out_shape=jax.ShapeDtypeStruct((nc, g * S5_STATE_WIDTH), F32),
        compiler_params=_params(("parallel",)),
        name="s5_chunk_states",
    )(ug, w_state)


def _s5_scan_kernel(s_ref, ar_ref, ai_ref, h_ref, *, nc):
    half = S5_STATE_WIDTH // 2
    lane = lax.broadcasted_iota(jnp.int32, (SCAN_GROUPS, half), 1)
    fwd = lane < S5_STATE
    ar = ar_ref[...]
    ai = ai_ref[...]

    def step(i, carry):
        hr, hi = carry
        sf = s_ref[i]
        sb = s_ref[nc - 1 - i]
        h_ref[i, :, :half] = hr
        h_ref[i, :, half:] = hi
        sr = jnp.where(fwd, sf[:, :half], sb[:, :half])
        si = jnp.where(fwd, sf[:, half:], sb[:, half:])
        return ar * hr - ai * hi + sr, ar * hi + ai * hr + si

    zero = jnp.zeros((SCAN_GROUPS, half), F32)
    lax.fori_loop(0, nc, step, (zero, zero))

    fwd2 = jnp.concatenate([fwd, fwd], axis=1)

    def swap(i, carry):
        a = h_ref[i]
        b = h_ref[nc - 1 - i]
        h_ref[i] = jnp.where(fwd2, a, b)
        h_ref[nc - 1 - i] = jnp.where(fwd2, b, a)
        return carry

    lax.fori_loop(0, nc // 2, swap, 0)


def _s5_scan(s3, a_re, a_im):
    nc, g, sw = s3.shape
    return pl.pallas_call(
        functools.partial(_s5_scan_kernel, nc=nc),
        grid=(g // SCAN_GROUPS,),
        in_specs=[pl.BlockSpec((nc, SCAN_GROUPS, sw), lambda i: (0, i, 0)),
                  pl.BlockSpec((SCAN_GROUPS, sw // 2), lambda i: (i, 0)),
                  pl.BlockSpec((SCAN_GROUPS, sw // 2), lambda i: (i, 0))],
        out_specs=pl.BlockSpec((nc, SCAN_GROUPS, sw), lambda i: (0, i, 0)),
        out_shape=jax.ShapeDtypeStruct((nc, g, sw), F32),
        compiler_params=_params(("parallel",)),
        name="s5_scan",
    )(s3, a_re, a_im)


def _s5_out_kernel(u_ref, h_ref, m_ref, o_ref):
    cw = S5_CHUNK_WIDTH
    y = jnp.dot(u_ref[...], m_ref[:cw, :], preferred_element_type=F32)
    y = y + jnp.dot(h_ref[...].astype(BF16), m_ref[cw:, :], preferred_element_type=F32)
    o_ref[...] = jax.nn.gelu(y, approximate=True).astype(o_ref.dtype)


def _s5_outputs(ug, h2, m_out):
    g, nc, cw = ug.shape
    return pl.pallas_call(
        _s5_out_kernel,
        grid=(g,),
        in_specs=[pl.BlockSpec((None, nc, cw), lambda i: (i, 0, 0)),
                  pl.BlockSpec((nc, S5_STATE_WIDTH), lambda i: (0, i)),
                  pl.BlockSpec((None, cw + S5_STATE_WIDTH, cw), lambda i: (i, 0, 0))],
        out_specs=pl.BlockSpec((None, nc, cw), lambda i: (i, 0, 0)),
        out_shape=jax.ShapeDtypeStruct((g, nc, cw), BF16),
        compiler_params=_params(("parallel",)),
        name="s5_outputs",
    )(ug, h2, m_out)


def _glu_kernel(a_ref, w_ref, b_ref, y_ref, o_ref):
    t = jnp.dot(a_ref[...], w_ref[...], preferred_element_type=F32) + b_ref[...]
    o_ref[...] = (y_ref[...].astype(F32) * jax.nn.sigmoid(t)).astype(o_ref.dtype)


def _glu(ys, w, b, tm, tn):
    m, k = ys.shape
    n = w.shape[1]
    return pl.pallas_call(
        _glu_kernel,
        grid=(m // tm, n // tn),
        in_specs=[pl.BlockSpec((tm, k), lambda i, j: (i, 0)),
                  pl.BlockSpec((k, tn), lambda i, j: (0, j)),
                  pl.BlockSpec((1, tn), lambda i, j: (0, j)),
                  pl.BlockSpec((tm, tn), lambda i, j: (i, j))],
        out_specs=pl.BlockSpec((tm, tn), lambda i, j: (i, j)),
        out_shape=jax.ShapeDtypeStruct((m, n), BF16),
        compiler_params=_params(("parallel", "arbitrary")),
        name="s5_glu",
    )(ys, w, b, ys)


def _merge_kernel(a1_ref, w1_ref, a2_ref, w2_ref, g1_ref, g2_ref, o_ref):
    y1 = jnp.dot(a1_ref[...], w1_ref[...], preferred_element_type=F32)
    y2 = jnp.dot(a2_ref[...], w2_ref[...], preferred_element_type=F32)
    o_ref[...] = (g1_ref[...].astype(F32) * y1 + g2_ref[...].astype(F32) * y2).astype(o_ref.dtype)


def _merge(att, w_na, z, w_s5, proj, gate_col0, tm, tn):
    m, k1 = att.shape
    k2 = z.shape[1]
    n = w_na.shape[1]
    g1 = gate_col0 // tn
    g2 = (gate_col0 + n) // tn
    return pl.pallas_call(
        _merge_kernel,
        grid=(m // tm, n // tn),
        in_specs=[pl.BlockSpec((tm, k1), lambda i, j: (i, 0)),
                  pl.BlockSpec((k1, tn), lambda i, j: (0, j)),
                  pl.BlockSpec((tm, k2), lambda i, j: (i, 0)),
                  pl.BlockSpec((k2, tn), lambda i, j: (0, j)),
                  pl.BlockSpec((tm, tn), lambda i, j: (i, g1 + j)),
                  pl.BlockSpec((tm, tn), lambda i, j: (i, g2 + j))],
        out_specs=pl.BlockSpec((tm, tn), lambda i, j: (i, j)),
        out_shape=jax.ShapeDtypeStruct((m, n), BF16),
        compiler_params=_params(("parallel", "arbitrary")),
        name="gated_merge",
    )(att, w_na, z, w_s5, proj, proj)


def _outproj_ln_kernel(a_ref, w_ref, x_ref, g_ref, b_ref, h_ref, hb_ref, *, alpha):
    mix = jnp.dot(a_ref[...], w_ref[...], preferred_element_type=F32)
    h = _layer_norm_rows(alpha * x_ref[...] + mix, g_ref[...], b_ref[...])
    h_ref[...] = h
    hb_ref[...] = h.astype(BF16)


def _outproj_ln(a, w, x, g, b, alpha, tm):
    m, k = a.shape
    n = w.shape[1]
    return pl.pallas_call(
        functools.partial(_outproj_ln_kernel, alpha=alpha),
        grid=(m // tm,),
        in_specs=[pl.BlockSpec((tm, k), lambda i: (i, 0)),
                  pl.BlockSpec((k, n), lambda i: (0, 0)),
                  pl.BlockSpec((tm, n), lambda i: (i, 0)),
                  pl.BlockSpec((1, n), lambda i: (0, 0)),
                  pl.BlockSpec((1, n), lambda i: (0, 0))],
        out_specs=[pl.BlockSpec((tm, n), lambda i: (i, 0)),
                   pl.BlockSpec((tm, n), lambda i: (i, 0))],
        out_shape=[jax.ShapeDtypeStruct((m, n), F32), jax.ShapeDtypeStruct((m, n), BF16)],
        compiler_params=_params(("parallel",)),
        name="outproj_ln1",
    )(a, w, x, g, b)


def _ffn_kernel(hb_ref, h_ref, wg_ref, wu_ref, wd_ref, g_ref, b_ref, o_ref, acc_ref, *, alpha):
    j = pl.program_id(1)
    hb = hb_ref[...]
    gate = jnp.dot(hb, wg_ref[...], preferred_element_type=F32)
    up = jnp.dot(hb, wu_ref[...], preferred_element_type=F32)
    act = (jax.nn.silu(gate) * up).astype(BF16)
    part = jnp.dot(act, wd_ref[...], preferred_element_type=F32)

    @pl.when(j == 0)
    def _():
        acc_ref[...] = part

    @pl.when(j > 0)
    def _():
        acc_ref[...] += part

    @pl.when(j == pl.num_programs(1) - 1)
    def _():
        o_ref[...] = _layer_norm_rows(alpha * h_ref[...] + acc_ref[...], g_ref[...], b_ref[...])


def _ffn(hb, h, wg, wu, wd, g, b, alpha, tm, tf):
    m, d = hb.shape
    f = wg.shape[1]
    return pl.pallas_call(
        functools.partial(_ffn_kernel, alpha=alpha),
        grid=(m // tm, f // tf),
        in_specs=[pl.BlockSpec((tm, d), lambda i, j: (i, 0)),
                  pl.BlockSpec((tm, d), lambda i, j: (i, 0)),
                  pl.BlockSpec((d, tf), lambda i, j: (0, j)),
                  pl.BlockSpec((d, tf), lambda i, j: (0, j)),
                  pl.BlockSpec((tf, d), lambda i, j: (j, 0)),
                  pl.BlockSpec((1, d), lambda i, j: (0, 0)),
                  pl.BlockSpec((1, d), lambda i, j: (0, 0))],
        out_specs=pl.BlockSpec((tm, d), lambda i, j: (i, 0)),
        out_shape=jax.ShapeDtypeStruct((m, d), F32),
        scratch_shapes=[pltpu.VMEM((tm, d), F32)],
        compiler_params=_params(("parallel", "arbitrary")),
        name="swiglu_ln2",
    )(hb, h, wg, wu, wd, g, b)


def _layer(x2, w_in, b_gate, na_rpb, w_na_out, s5_a_re, s5_a_im, s5_log_dt, s5_b_re, s5_b_im,
           s5_c_re, s5_c_im, s5_d, w_glu, b_glu, w_s5_out, w_out, ln1_g, ln1_b,
           w_ffn_gate, w_ffn_up, w_ffn_down, ln2_g, ln2_b, alpha):
    seq, d_model = x2.shape
    rows = seq // GRID_W
    nc = seq // S5_CHUNK
    gate_col0 = 3 * NA_WIDTH + S5_WIDTH
    in_cols = gate_col0 + 2 * d_model
    row = lambda v: v.astype(F32).reshape(1, -1)

    col_scale = jnp.where(jnp.arange(in_cols) < NA_WIDTH, NA_HEAD_DIM ** -0.5, 1.0).astype(F32).reshape(1, -1)
    col_bias = jnp.concatenate([jnp.zeros((gate_col0,), F32), b_gate.astype(F32)]).reshape(1, -1)
    proj = _inproj(x2, w_in.astype(BF16), col_scale, col_bias, gate_col0, tm=1024, tn=512)

    att = _attention(proj, _attention_bias(na_rpb, rows), seq)

    w_state, m_out, a_re, a_im = _s5_tables(s5_a_re, s5_a_im, s5_log_dt, s5_b_re, s5_b_im,
                                            s5_c_re, s5_c_im, s5_d)
    u = proj[:, 3 * NA_WIDTH:gate_col0]
    ug = u.reshape(nc, S5_CHUNK, S5_GROUPS, S5_GROUP).transpose(2, 0, 1, 3).reshape(S5_GROUPS, nc, S5_CHUNK_WIDTH)
    s = _s5_chunk_states(ug, w_state.astype(BF16))
    h_states = _s5_scan(s.reshape(nc, S5_GROUPS, S5_STATE_WIDTH), a_re, a_im)
    yg = _s5_outputs(ug, h_states.reshape(nc, S5_GROUPS * S5_STATE_WIDTH), m_out.astype(BF16))
    ys = yg.reshape(S5_GROUPS, nc, S5_CHUNK, S5_GROUP).transpose(1, 2, 0, 3).reshape(seq, S5_WIDTH)

    z = _glu(ys, w_glu.astype(BF16), row(b_glu), tm=1024, tn=512)
    merged = _merge(att, w_na_out.astype(BF16), z, w_s5_out.astype(BF16), proj, gate_col0, tm=1024, tn=512)
    h, hb = _outproj_ln(merged, w_out.astype(BF16), x2, row(ln1_g), row(ln1_b), alpha, tm=512)
    return _ffn(hb, h, w_ffn_gate.astype(BF16), w_ffn_up.astype(BF16), w_ffn_down.astype(BF16),
                row(ln2_g), row(ln2_b), alpha, tm=512, tf=512)


def kernel(x, w_in, b_gate, na_rpb, w_na_out, s5_a_re, s5_a_im, s5_log_dt, s5_b_re, s5_b_im, s5_c_re, s5_c_im, s5_d, w_glu, b_glu, w_s5_out, w_out, ln1_g, ln1_b, w_ffn_gate, w_ffn_up, w_ffn_down, ln2_g, ln2_b):
    bsz, seq, d_model = x.shape
    depth = w_in.shape[0]
    alpha = (2.0 * depth) ** 0.25
    outs = []
    for bi in range(bsz):
        xb = x[bi]
        for l in range(depth):
            xb = _layer(xb, w_in[l], b_gate[l], na_rpb[l], w_na_out[l], s5_a_re[l], s5_a_im[l], s5_log_dt[l],
                        s5_b_re[l], s5_b_im[l], s5_c_re[l], s5_c_im[l], s5_d[l], w_glu[l], b_glu[l],
                        w_s5_out[l], w_out[l], ln1_g[l], ln1_b[l], w_ffn_gate[l], w_ffn_up[l],
                        w_ffn_down[l], ln2_g[l], ln2_b[l], alpha)
        outs.append(xb)
    return jnp.stack(outs)
```

```python
import functools
import math

import jax
import jax.numpy as jnp
from jax import lax
from jax.experimental import pallas as pl
from jax.experimental.pallas import tpu as pltpu

F32 = jnp.float32
BF16 = jnp.bfloat16

GRID_W = 64
NA_HEADS = 8
NA_HEAD_DIM = 128
NA_WIDTH = NA_HEADS * NA_HEAD_DIM
NA_ROWS = 8
NA_COLS = 16
S5_GROUP = 16
S5_GROUPS = 64
S5_WIDTH = S5_GROUP * S5_GROUPS
S5_STATE = 64
LN_EPS = 1e-5
MASK_VALUE = -1e30

S5_CHUNK = 16
S5_CHUNK_WIDTH = S5_CHUNK * S5_GROUP
S5_STATE_WIDTH = 4 * S5_STATE
SCAN_GROUPS = 8

ATT_Q_ROWS = 8
ATT_K_ROWS = 16
ATT_Q = ATT_Q_ROWS * GRID_W
ATT_K = ATT_K_ROWS * GRID_W
ATT_KBLK = 256
LANES = 128
ATT_WIN = NA_ROWS * GRID_W + LANES
S5_LANE_GROUPS = LANES // S5_GROUP
RELAYOUT_ROWS = 16

VMEM_LIMIT = 56 * 1024 * 1024


def _params(sem):
    return pltpu.CompilerParams(dimension_semantics=sem, vmem_limit_bytes=VMEM_LIMIT)


def _layer_norm_rows(y, g, b):
    mu = jnp.mean(y, axis=-1, keepdims=True)
    d = y - mu
    var = jnp.mean(d * d, axis=-1, keepdims=True)
    return d * lax.rsqrt(var + LN_EPS) * g + b


def _inproj_kernel(x_ref, w_ref, s_ref, b_ref, o_ref, u_ref, xb_ref, *, u_tile0, gate_tile0):
    j = pl.program_id(1)

    @pl.when(j == 0)
    def _():
        xb_ref[...] = x_ref[...].astype(BF16)

    @pl.when(j < u_tile0)
    def _():
        acc = jnp.dot(xb_ref[...], w_ref[...], preferred_element_type=F32)
        o_ref[...] = (acc * s_ref[...]).astype(o_ref.dtype)

    @pl.when((j >= u_tile0) & (j < gate_tile0))
    def _():
        acc = jnp.dot(xb_ref[...], w_ref[...], preferred_element_type=F32)
        o_ref[...] = acc.astype(o_ref.dtype)
        u_ref[...] = acc

    @pl.when(j >= gate_tile0)
    def _():
        acc = jnp.dot(xb_ref[...], w_ref[...], preferred_element_type=F32)
        o_ref[...] = jax.nn.sigmoid(acc + b_ref[...]).astype(o_ref.dtype)


def _inproj(x, w, col_scale, col_bias, u_col0, gate_col0, tm, tn):
    m, k = x.shape
    n = w.shape[1]
    u_tile0, gate_tile0 = u_col0 // tn, gate_col0 // tn
    return pl.pallas_call(
        functools.partial(_inproj_kernel, u_tile0=u_tile0, gate_tile0=gate_tile0),
        grid=(m // tm, n // tn),
        in_specs=[
            pl.BlockSpec((tm, k), lambda i, j: (i, 0)),
            pl.BlockSpec((k, tn), lambda i, j: (0, j)),
            pl.BlockSpec((1, tn), lambda i, j: (0, j)),
            pl.BlockSpec((1, tn), lambda i, j: (0, j)),
        ],
        out_specs=[pl.BlockSpec((tm, tn), lambda i, j: (i, j)),
                   pl.BlockSpec((tm, tn), lambda i, j: (i, jnp.clip(j - u_tile0, 0, gate_tile0 - u_tile0 - 1)))],
        out_shape=[jax.ShapeDtypeStruct((m, n), BF16),
                   jax.ShapeDtypeStruct((m, gate_col0 - u_col0), F32)],
        scratch_shapes=[pltpu.VMEM((tm, k), BF16)],
        compiler_params=_params(("parallel", "arbitrary")),
        name="inproj",
    )(x, w, col_scale, col_bias)


def _attn_windows(rows):
    nb = rows // ATT_Q_ROWS
    kinds = []
    for b in (0, 1, nb - 1):
        base = min(max(b * ATT_Q_ROWS - NA_ROWS // 2, 0), rows - ATT_K_ROWS)
        geo = []
        for ql in range(ATT_Q_ROWS):
            qr = b * ATT_Q_ROWS + ql
            r0 = min(max(qr - NA_ROWS // 2, 0), rows - NA_ROWS)
            koff = r0 - base
            lane0 = min(LANES * (koff * GRID_W // LANES), ATT_K - ATT_WIN)
            geo.append((lane0, koff * GRID_W - lane0, r0 - qr + NA_ROWS - 1))
        kinds.append(tuple(geo))
    return tuple(kinds)


def _attn_kernel(q_ref, k0, k1, k2, k3, v0, v1, v2, v3, bias_ref, o_ref, p_ref, *, windows, nb):
    b = pl.program_id(1)
    k = jnp.concatenate([k0[...], k1[...], k2[...], k3[...]], axis=0)
    v = jnp.concatenate([v0[...], v1[...], v2[...], v3[...]], axis=0)
    s = lax.dot_general(q_ref[...], k, (((1,), (1,)), ((), ())), preferred_element_type=F32)

    def softmax_rows(geo):
        inv = []
        for ql, (lane0, _, _) in enumerate(geo):
            r = slice(ql * GRID_W, (ql + 1) * GRID_W)
            sw = s[r, lane0:lane0 + ATT_WIN] + bias_ref[ql]
            m = jnp.max(sw, axis=-1, keepdims=True)
            p = jnp.exp(sw - m)
            inv.append(1.0 / jnp.sum(p, axis=-1, keepdims=True))
            if lane0 > 0:
                p_ref[r, :lane0] = jnp.zeros((GRID_W, lane0), BF16)
            p_ref[r, lane0:lane0 + ATT_WIN] = p.astype(BF16)
            if lane0 + ATT_WIN < ATT_K:
                p_ref[r, lane0 + ATT_WIN:] = jnp.zeros((GRID_W, ATT_K - lane0 - ATT_WIN), BF16)
        o = jnp.dot(p_ref[...], v, preferred_element_type=F32)
        o_ref[...] = (o * jnp.concatenate(inv, axis=0)).astype(o_ref.dtype)

    @pl.when(b == 0)
    def _():
        softmax_rows(windows[0])

    @pl.when((b > 0) & (b < nb - 1))
    def _():
        softmax_rows(windows[1])

    @pl.when(b == nb - 1)
    def _():
        softmax_rows(windows[2])


def _attention(proj, bias, seq):
    nb = seq // ATT_Q
    n_kblk = seq // ATT_KBLK
    per_blk = ATT_K // ATT_KBLK

    def kstart(b):
        return jnp.clip(2 * b - 1, 0, n_kblk - per_blk)

    def kind(b):
        return jnp.where(b == 0, 0, jnp.where(b == nb - 1, 2, 1))

    def kv_spec(col0, t):
        return pl.BlockSpec((ATT_KBLK, NA_HEAD_DIM), lambda h, b: (kstart(b) + t, col0 + h))

    in_specs = [pl.BlockSpec((ATT_Q, NA_HEAD_DIM), lambda h, b: (b, h))]
    in_specs += [kv_spec(NA_HEADS, t) for t in range(per_blk)]
    in_specs += [kv_spec(2 * NA_HEADS, t) for t in range(per_blk)]
    in_specs += [pl.BlockSpec((None, None, ATT_Q_ROWS, GRID_W, ATT_WIN), lambda h, b: (kind(b), h, 0, 0, 0))]
    return pl.pallas_call(
        functools.partial(_attn_kernel, windows=_attn_windows(seq // GRID_W), nb=nb),
        grid=(NA_HEADS, nb),
        in_specs=in_specs,
        out_specs=pl.BlockSpec((ATT_Q, NA_HEAD_DIM), lambda h, b: (b, h)),
        out_shape=jax.ShapeDtypeStruct((seq, NA_WIDTH), BF16),
        scratch_shapes=[pltpu.VMEM((ATT_Q, ATT_K), BF16)],
        compiler_params=_params(("parallel", "arbitrary")),
        name="na_attention",
    )(*([proj] * (1 + 2 * per_blk)), bias)


def _attention_bias(rpb, rows):
    cols = jnp.arange(GRID_W)
    col_start = jnp.clip(cols - NA_COLS // 2, 0, GRID_W - NA_COLS)
    col_ok = (cols[None, :] >= col_start[:, None]) & (cols[None, :] < col_start[:, None] + NA_COLS)
    col_idx = jnp.clip(cols[None, :] - cols[:, None], -(NA_COLS - 1), NA_COLS - 1) + (NA_COLS - 1)
    by_col = jnp.take(rpb.astype(F32), col_idx.reshape(-1), axis=2)
    by_col = by_col.reshape(NA_HEADS, 2 * NA_ROWS - 1, GRID_W, GRID_W)
    by_col = jnp.where(col_ok[None, None], by_col, MASK_VALUE).transpose(0, 2, 1, 3)
    kinds = []
    for geo in _attn_windows(rows):
        strips = []
        for _, phase, rfirst in geo:
            strip = by_col[:, :, rfirst:rfirst + NA_ROWS, :].reshape(NA_HEADS, GRID_W, NA_ROWS * GRID_W)
            strips.append(jnp.pad(strip, ((0, 0), (0, 0), (phase, ATT_WIN - NA_ROWS * GRID_W - phase)),
                                  constant_values=MASK_VALUE))
        kinds.append(jnp.stack(strips, axis=1))
    return jnp.stack(kinds)


def _s5_tables(a_re, a_im, log_dt, b_re, b_im, c_re, c_im, d):
    t_len = S5_CHUNK
    n = jnp.arange(t_len + 1, dtype=F32)

    def direction(k):
        ar, ai = a_re[k].astype(F32), a_im[k].astype(F32)
        dt = jnp.exp(log_dt[k].astype(F32))[:, None]
        zr, zi = ar * dt, ai * dt
        mag = jnp.exp(zr[..., None] * n)
        pr, pi = mag * jnp.cos(zi[..., None] * n), mag * jnp.sin(zi[..., None] * n)
        lr, li = pr[..., 1] - 1.0, pi[..., 1]
        den = ar * ar + ai * ai
        fr, fi = (lr * ar + li * ai) / den, (li * ar - lr * ai) / den
        br, bi = b_re[k].astype(F32), b_im[k].astype(F32)
        bbr = fr[..., None] * br - fi[..., None] * bi
        bbi = fr[..., None] * bi + fi[..., None] * br
        return pr, pi, bbr, bbi, c_re[k].astype(F32), c_im[k].astype(F32)

    def cmul(xr, xi, yr, yi):
        return xr * yr - xi * yi, xr * yi + xi * yr

    w_cols, m_rows, kern = [], [], []
    for k in range(2):
        pr, pi, bbr, bbi, cr, ci = direction(k)
        wp_r = pr[..., :t_len][..., ::-1] if k == 0 else pr[..., :t_len]
        wp_i = pi[..., :t_len][..., ::-1] if k == 0 else pi[..., :t_len]
        wr, wi = cmul(wp_r[:, :, :, None], wp_i[:, :, :, None], bbr[:, :, None, :], bbi[:, :, None, :])
        w_cols.append((wr.transpose(0, 2, 3, 1).reshape(S5_GROUPS, S5_CHUNK_WIDTH, S5_STATE),
                       wi.transpose(0, 2, 3, 1).reshape(S5_GROUPS, S5_CHUNK_WIDTH, S5_STATE)))
        ep_r = pr[..., 1:] if k == 0 else pr[..., 1:][..., ::-1]
        ep_i = pi[..., 1:] if k == 0 else pi[..., 1:][..., ::-1]
        er, ei = cmul(cr[:, :, :, None], ci[:, :, :, None], ep_r[:, None, :, :], ep_i[:, None, :, :])
        m_rows.append((er.transpose(0, 2, 3, 1).reshape(S5_GROUPS, S5_STATE, S5_CHUNK_WIDTH),
                       (-ei).transpose(0, 2, 3, 1).reshape(S5_GROUPS, S5_STATE, S5_CHUNK_WIDTH)))
        xr, xi = cmul(cr[:, :, :, None], ci[:, :, :, None], pr[:, None, :, :t_len], pi[:, None, :, :t_len])
        kern.append(jnp.einsum("gopt,gpc->gtoc", xr, bbr, precision=lax.Precision.HIGHEST)
                    - jnp.einsum("gopt,gpc->gtoc", xi, bbi, precision=lax.Precision.HIGHEST))
        if k == 0:
            step_r, step_i = [pr[..., t_len]], [pi[..., t_len]]
        else:
            step_r.append(pr[..., t_len]); step_i.append(pi[..., t_len])

    w_state = jnp.concatenate([w_cols[0][0], w_cols[1][0], w_cols[0][1], w_cols[1][1]], axis=-1)
    m_state = jnp.concatenate([m_rows[0][0], m_rows[1][0], m_rows[0][1], m_rows[1][1]], axis=1)
    eye = jnp.eye(S5_GROUP, dtype=F32)
    k0 = kern[0][:, 0] + kern[1][:, 0] + d.astype(F32)[:, :, None] * eye[None]
    lags = jnp.concatenate([kern[1][:, 1:][:, ::-1], k0[:, None], kern[0][:, 1:]], axis=1)
    steps = jnp.arange(t_len)
    lag_idx = steps[None, :] - steps[:, None] + (t_len - 1)
    toe = jnp.take(lags, lag_idx.reshape(-1), axis=1).reshape(S5_GROUPS, t_len, t_len, S5_GROUP, S5_GROUP)
    toe = toe.transpose(0, 1, 4, 2, 3).reshape(S5_GROUPS, S5_CHUNK_WIDTH, S5_CHUNK_WIDTH)
    m_out = jnp.concatenate([toe, m_state], axis=1)
    a_step_re = jnp.concatenate(step_r, axis=-1)
    a_step_im = jnp.concatenate(step_i, axis=-1)
    return w_state, m_out, a_step_re, a_step_im


def _lane_group_ids():
    return lax.broadcasted_iota(jnp.int32, (RELAYOUT_ROWS, LANES), 1) // S5_GROUP


def _s5_state_kernel(u_ref, w_ref, s_ref, ug_ref):
    nch = ug_ref.shape[1]
    grp = _lane_group_ids()

    def regroup(rb, carry):
        r0 = pl.multiple_of(rb * RELAYOUT_ROWS, RELAYOUT_ROWS)
        xs = [u_ref[pl.ds(r0 * S5_CHUNK + t, RELAYOUT_ROWS, stride=S5_CHUNK), :] for t in range(S5_CHUNK)]
        for g in range(S5_LANE_GROUPS):
            for half in range(S5_CHUNK_WIDTH // LANES):
                acc = None
                for tp in range(S5_LANE_GROUPS):
                    shift = (S5_GROUP * (tp - g)) % LANES
                    x = xs[half * S5_LANE_GROUPS + tp]
                    x = pltpu.roll(x, shift, 1) if shift else x
                    acc = x if acc is None else jnp.where(grp == tp, x, acc)
                ug_ref[g, pl.ds(r0, RELAYOUT_ROWS), half * LANES:(half + 1) * LANES] = acc.astype(BF16)
        return carry

    lax.fori_loop(0, nch // RELAYOUT_ROWS, regroup, 0)
    for g in range(S5_LANE_GROUPS):
        s_ref[:, g * S5_STATE_WIDTH:(g + 1) * S5_STATE_WIDTH] = jnp.dot(
            ug_ref[g], w_ref[g], preferred_element_type=F32)


def _s5_chunk_states(u, w_state, nsplit):
    seq, width = u.shape
    g = width // S5_GROUP
    nc = seq // S5_CHUNK
    nch = nc // nsplit
    lg = S5_LANE_GROUPS
    return pl.pallas_call(
        _s5_state_kernel,
        grid=(width // LANES, nsplit),
        in_specs=[pl.BlockSpec((nch * S5_CHUNK, LANES), lambda j, h: (h, j)),
                  pl.BlockSpec((lg, S5_CHUNK_WIDTH, S5_STATE_WIDTH), lambda j, h: (j, 0, 0))],
        out_specs=[pl.BlockSpec((nch, lg * S5_STATE_WIDTH), lambda j, h: (h, j)),
                   pl.BlockSpec((lg, nch, S5_CHUNK_WIDTH), lambda j, h: (j, h, 0))],
        out_shape=[jax.ShapeDtypeStruct((nc, g * S5_STATE_WIDTH), F32),
                   jax.ShapeDtypeStruct((g, nc, S5_CHUNK_WIDTH), BF16)],
        compiler_params=_params(("parallel", "parallel")),
        name="s5_chunk_states",
    )(u, w_state)


def _s5_scan_kernel(s_ref, ar_ref, ai_ref, h_ref, *, nc):
    half = S5_STATE_WIDTH // 2
    lane = lax.broadcasted_iota(jnp.int32, (SCAN_GROUPS, half), 1)
    fwd = lane < S5_STATE
    ar = ar_ref[...]
    ai = ai_ref[...]

    def step(i, carry):
        hr, hi = carry
        sf = s_ref[i]
        sb = s_ref[nc - 1 - i]
        h_ref[i, :, :half] = hr
        h_ref[i, :, half:] = hi
        sr = jnp.where(fwd, sf[:, :half], sb[:, :half])
        si = jnp.where(fwd, sf[:, half:], sb[:, half:])
        return ar * hr - ai * hi + sr, ar * hi + ai * hr + si

    zero = jnp.zeros((SCAN_GROUPS, half), F32)
    lax.fori_loop(0, nc, step, (zero, zero))

    fwd2 = jnp.concatenate([fwd, fwd], axis=1)

    def swap(i, carry):
        a = h_ref[i]
        b = h_ref[nc - 1 - i]
        h_ref[i] = jnp.where(fwd2, a, b)
        h_ref[nc - 1 - i] = jnp.where(fwd2, b, a)
        return carry

    lax.fori_loop(0, nc // 2, swap, 0)


def _s5_scan(s3, a_re, a_im):
    nc, g, sw = s3.shape
    return pl.pallas_call(
        functools.partial(_s5_scan_kernel, nc=nc),
        grid=(g // SCAN_GROUPS,),
        in_specs=[pl.BlockSpec((nc, SCAN_GROUPS, sw), lambda i: (0, i, 0)),
                  pl.BlockSpec((SCAN_GROUPS, sw // 2), lambda i: (i, 0)),
                  pl.BlockSpec((SCAN_GROUPS, sw // 2), lambda i: (i, 0))],
        out_specs=pl.BlockSpec((nc, SCAN_GROUPS, sw), lambda i: (0, i, 0)),
        out_shape=jax.ShapeDtypeStruct((nc, g, sw), F32),
        compiler_params=_params(("parallel",)),
        name="s5_scan",
    )(s3, a_re, a_im)


def _s5_out_kernel(ug_ref, h_ref, m_ref, o_ref, y_ref):
    cw = S5_CHUNK_WIDTH
    nch = ug_ref.shape[1]
    for g in range(S5_LANE_GROUPS):
        hg = h_ref[:, g * S5_STATE_WIDTH:(g + 1) * S5_STATE_WIDTH].astype(BF16)
        y = jnp.dot(ug_ref[g], m_ref[g, :cw, :], preferred_element_type=F32)
        y = y + jnp.dot(hg, m_ref[g, cw:, :], preferred_element_type=F32)
        y_ref[g] = jax.nn.gelu(y, approximate=True)
    grp = _lane_group_ids()

    def regroup(rb, carry):
        r0 = pl.multiple_of(rb * RELAYOUT_ROWS, RELAYOUT_ROWS)
        for half in range(cw // LANES):
            ys = [y_ref[g, pl.ds(r0, RELAYOUT_ROWS), half * LANES:(half + 1) * LANES]
                  for g in range(S5_LANE_GROUPS)]
            for tp in range(S5_LANE_GROUPS):
                acc = None
                for g in range(S5_LANE_GROUPS):
                    shift = (S5_GROUP * (g - tp)) % LANES
                    x = pltpu.roll(ys[g], shift, 1) if shift else ys[g]
                    acc = x if acc is None else jnp.where(grp == g, x, acc)
                t = half * S5_LANE_GROUPS + tp
                o_ref[pl.ds(r0 * S5_CHUNK + t, RELAYOUT_ROWS, stride=S5_CHUNK), :] = acc
        return carry

    lax.fori_loop(0, nch // RELAYOUT_ROWS, regroup, 0)


def _s5_outputs(ug, h2, m_out, nsplit):
    g, nc, cw = ug.shape
    nch = nc // nsplit
    lg = S5_LANE_GROUPS
    return pl.pallas_call(
        _s5_out_kernel,
        grid=(g // lg, nsplit),
        in_specs=[pl.BlockSpec((lg, nch, cw), lambda j, h: (j, h, 0)),
                  pl.BlockSpec((nch, lg * S5_STATE_WIDTH), lambda j, h: (h, j)),
                  pl.BlockSpec((lg, cw + S5_STATE_WIDTH, cw), lambda j, h: (j, 0, 0))],
        out_specs=pl.BlockSpec((nch * S5_CHUNK, LANES), lambda j, h: (h, j)),
        out_shape=jax.ShapeDtypeStruct((nc * S5_CHUNK, g * S5_GROUP), F32),
        scratch_shapes=[pltpu.VMEM((lg, nch, cw), F32)],
        compiler_params=_params(("parallel", "parallel")),
        name="s5_outputs",
    )(ug, h2, m_out)


def _glu_kernel(a_ref, w_ref, b_ref, y_ref, o_ref, ab_ref):
    @pl.when(pl.program_id(1) == 0)
    def _():
        ab_ref[...] = a_ref[...].astype(BF16)

    t = jnp.dot(ab_ref[...], w_ref[...], preferred_element_type=F32) + b_ref[...]
    o_ref[...] = (y_ref[...] * jax.nn.sigmoid(t)).astype(o_ref.dtype)


def _glu(ys, w, b, tm, tn):
    m, k = ys.shape
    n = w.shape[1]
    return pl.pallas_call(
        _glu_kernel,
        grid=(m // tm, n // tn),
        in_specs=[pl.BlockSpec((tm, k), lambda i, j: (i, 0)),
                  pl.BlockSpec((k, tn), lambda i, j: (0, j)),
                  pl.BlockSpec((1, tn), lambda i, j: (0, j)),
                  pl.BlockSpec((tm, tn), lambda i, j: (i, j))],
        out_specs=pl.BlockSpec((tm, tn), lambda i, j: (i, j)),
        out_shape=jax.ShapeDtypeStruct((m, n), BF16),
        scratch_shapes=[pltpu.VMEM((tm, k), BF16)],
        compiler_params=_params(("parallel", "arbitrary")),
        name="s5_glu",
    )(ys, w, b, ys)


def _merge_kernel(a1_ref, w1_ref, a2_ref, w2_ref, g1_ref, g2_ref, o_ref):
    y1 = jnp.dot(a1_ref[...], w1_ref[...], preferred_element_type=F32)
    y2 = jnp.dot(a2_ref[...], w2_ref[...], preferred_element_type=F32)
    o_ref[...] = (g1_ref[...].astype(F32) * y1 + g2_ref[...].astype(F32) * y2).astype(o_ref.dtype)


def _merge(att, w_na, z, w_s5, proj, gate_col0, tm, tn):
    m, k1 = att.shape
    k2 = z.shape[1]
    n = w_na.shape[1]
    g1 = gate_col0 // tn
    g2 = (gate_col0 + n) // tn
    return pl.pallas_call(
        _merge_kernel,
        grid=(m // tm, n // tn),
        in_specs=[pl.BlockSpec((tm, k1), lambda i, j: (i, 0)),
                  pl.BlockSpec((k1, tn), lambda i, j: (0, j)),
                  pl.BlockSpec((tm, k2), lambda i, j: (i, 0)),
                  pl.BlockSpec((k2, tn), lambda i, j: (0, j)),
                  pl.BlockSpec((tm, tn), lambda i, j: (i, g1 + j)),
                  pl.BlockSpec((tm, tn), lambda i, j: (i, g2 + j))],
        out_specs=pl.BlockSpec((tm, tn), lambda i, j: (i, j)),
        out_shape=jax.ShapeDtypeStruct((m, n), BF16),
        compiler_params=_params(("parallel", "arbitrary")),
        name="gated_merge",
    )(att, w_na, z, w_s5, proj, proj)


def _outproj_ln_kernel(a_ref, w_ref, x_ref, g_ref, b_ref, h_ref, hb_ref, *, alpha):
    mix = jnp.dot(a_ref[...], w_ref[...], preferred_element_type=F32)
    h = _layer_norm_rows(alpha * x_ref[...] + mix, g_ref[...], b_ref[...])
    h_ref[...] = h
    hb_ref[...] = h.astype(BF16)


def _outproj_ln(a, w, x, g, b, alpha, tm):
    m, k = a.shape
    n = w.shape[1]
    return pl.pallas_call(
        functools.partial(_outproj_ln_kernel, alpha=alpha),
        grid=(m // tm,),
        in_specs=[pl.BlockSpec((tm, k), lambda i: (i, 0)),
                  pl.BlockSpec((k, n), lambda i: (0, 0)),
                  pl.BlockSpec((tm, n), lambda i: (i, 0)),
                  pl.BlockSpec((1, n), lambda i: (0, 0)),
                  pl.BlockSpec((1, n), lambda i: (0, 0))],
        out_specs=[pl.BlockSpec((tm, n), lambda i: (i, 0)),
                   pl.BlockSpec((tm, n), lambda i: (i, 0))],
        out_shape=[jax.ShapeDtypeStruct((m, n), F32), jax.ShapeDtypeStruct((m, n), BF16)],
        compiler_params=_params(("parallel",)),
        name="outproj_ln1",
    )(a, w, x, g, b)


def _ffn_kernel(hb_ref, h_ref, wg_ref, wu_ref, wd_ref, g_ref, b_ref, o_ref, acc_ref, *, alpha):
    j = pl.program_id(1)
    hb = hb_ref[...]
    gate = jnp.dot(hb, wg_ref[...], preferred_element_type=F32)
    up = jnp.dot(hb, wu_ref[...], preferred_element_type=F32)
    act = (jax.nn.silu(gate) * up).astype(BF16)
    part = jnp.dot(act, wd_ref[...], preferred_element_type=F32)

    @pl.when(j == 0)
    def _():
        acc_ref[...] = part

    @pl.when(j > 0)
    def _():
        acc_ref[...] += part

    @pl.when(j == pl.num_programs(1) - 1)
    def _():
        o_ref[...] = _layer_norm_rows(alpha * h_ref[...] + acc_ref[...], g_ref[...], b_ref[...])


def _ffn(hb, h, wg, wu, wd, g, b, alpha, tm, tf):
    m, d = hb.shape
    f = wg.shape[1]
    return pl.pallas_call(
        functools.partial(_ffn_kernel, alpha=alpha),
        grid=(m // tm, f // tf),
        in_specs=[pl.BlockSpec((tm, d), lambda i, j: (i, 0)),
                  pl.BlockSpec((tm, d), lambda i, j: (i, 0)),
                  pl.BlockSpec((d, tf), lambda i, j: (0, j)),
                  pl.BlockSpec((d, tf), lambda i, j: (0, j)),
                  pl.BlockSpec((tf, d), lambda i, j: (j, 0)),
                  pl.BlockSpec((1, d), lambda i, j: (0, 0)),
                  pl.BlockSpec((1, d), lambda i, j: (0, 0))],
        out_specs=pl.BlockSpec((tm, d), lambda i, j: (i, 0)),
        out_shape=jax.ShapeDtypeStruct((m, d), F32),
        scratch_shapes=[pltpu.VMEM((tm, d), F32)],
        compiler_params=_params(("parallel", "arbitrary")),
        name="swiglu_ln2",
    )(hb, h, wg, wu, wd, g, b)


def _layer(x2, w_in, b_gate, na_rpb, w_na_out, s5_a_re, s5_a_im, s5_log_dt, s5_b_re, s5_b_im,
           s5_c_re, s5_c_im, s5_d, w_glu, b_glu, w_s5_out, w_out, ln1_g, ln1_b,
           w_ffn_gate, w_ffn_up, w_ffn_down, ln2_g, ln2_b, alpha):
    seq, d_model = x2.shape
    rows = seq // GRID_W
    nc = seq // S5_CHUNK
    gate_col0 = 3 * NA_WIDTH + S5_WIDTH
    in_cols = gate_col0 + 2 * d_model
    row = lambda v: v.astype(F32).reshape(1, -1)

    col_scale = jnp.where(jnp.arange(in_cols) < NA_WIDTH, NA_HEAD_DIM ** -0.5, 1.0).astype(F32).reshape(1, -1)
    col_bias = jnp.concatenate([jnp.zeros((gate_col0,), F32), b_gate.astype(F32)]).reshape(1, -1)
    proj, u = _inproj(x2, w_in.astype(BF16), col_scale, col_bias, 3 * NA_WIDTH, gate_col0, tm=1024, tn=512)

    att = _attention(proj, _attention_bias(na_rpb, rows), seq)

    w_state, m_out, a_re, a_im = _s5_tables(s5_a_re, s5_a_im, s5_log_dt, s5_b_re, s5_b_im,
                                            s5_c_re, s5_c_im, s5_d)
    s, ug = _s5_chunk_states(u, w_state.astype(BF16), nsplit=2)
    h_states = _s5_scan(s.reshape(nc, S5_GROUPS, S5_STATE_WIDTH), a_re, a_im)
    ys = _s5_outputs(ug, h_states.reshape(nc, S5_GROUPS * S5_STATE_WIDTH), m_out.astype(BF16), nsplit=2)

    z = _glu(ys, w_glu.astype(BF16), row(b_glu), tm=1024, tn=512)
    merged = _merge(att, w_na_out.astype(BF16), z, w_s5_out.astype(BF16), proj, gate_col0, tm=1024, tn=512)
    h, hb = _outproj_ln(merged, w_out.astype(BF16), x2, row(ln1_g), row(ln1_b), alpha, tm=512)
    return _ffn(hb, h, w_ffn_gate.astype(BF16), w_ffn_up.astype(BF16), w_ffn_down.astype(BF16),
                row(ln2_g), row(ln2_b), alpha, tm=512, tf=512)


def kernel(x, w_in, b_gate, na_rpb, w_na_out, s5_a_re, s5_a_im, s5_log_dt, s5_b_re, s5_b_im, s5_c_re, s5_c_im, s5_d, w_glu, b_glu, w_s5_out, w_out, ln1_g, ln1_b, w_ffn_gate, w_ffn_up, w_ffn_down, ln2_g, ln2_b):
    bsz, seq, d_model = x.shape
    depth = w_in.shape[0]
    alpha = (2.0 * depth) ** 0.25
    outs = []
    for bi in range(bsz):
        xb = x[bi]
        for l in range(depth):
            xb = _layer(xb, w_in[l], b_gate[l], na_rpb[l], w_na_out[l], s5_a_re[l], s5_a_im[l], s5_log_dt[l],
                        s5_b_re[l], s5_b_im[l], s5_c_re[l], s5_c_im[l], s5_d[l], w_glu[l], b_glu[l],
                        w_s5_out[l], w_out[l], ln1_g[l], ln1_b[l], w_ffn_gate[l], w_ffn_up[l],
                        w_ffn_down[l], ln2_g[l], ln2_b[l], alpha)
        outs.append(xb)
    return jnp.stack(outs)
```

```python
import functools
import math

import jax
import jax.numpy as jnp
from jax import lax
from jax.experimental import pallas as pl
from jax.experimental.pallas import tpu as pltpu

F32 = jnp.float32
BF16 = jnp.bfloat16

GRID_W = 64
NA_HEADS = 8
NA_HEAD_DIM = 128
NA_WIDTH = NA_HEADS * NA_HEAD_DIM
NA_ROWS = 8
NA_COLS = 16
S5_GROUP = 16
S5_GROUPS = 64
S5_WIDTH = S5_GROUP * S5_GROUPS
S5_STATE = 64
LN_EPS = 1e-5
MASK_VALUE = -1e30

S5_CHUNK = 16
S5_CHUNK_WIDTH = S5_CHUNK * S5_GROUP
S5_STATE_WIDTH = 4 * S5_STATE
SCAN_GROUPS = 8

ATT_Q_ROWS = 8
ATT_K_ROWS = 16
ATT_Q = ATT_Q_ROWS * GRID_W
ATT_K = ATT_K_ROWS * GRID_W
ATT_KBLK = 256
LANES = 128
ATT_WIN = NA_ROWS * GRID_W + LANES
S5_LANE_GROUPS = LANES // S5_GROUP
RELAYOUT_ROWS = 16

VMEM_LIMIT = 56 * 1024 * 1024


def _params(sem):
    return pltpu.CompilerParams(dimension_semantics=sem, vmem_limit_bytes=VMEM_LIMIT)


def _layer_norm_rows(y, g, b):
    mu = jnp.mean(y, axis=-1, keepdims=True)
    d = y - mu
    var = jnp.mean(d * d, axis=-1, keepdims=True)
    return d * lax.rsqrt(var + LN_EPS) * g + b


def _inproj_kernel(x_ref, w_ref, s_ref, b_ref, o_ref, u_ref, xb_ref, *, u_tile0, gate_tile0):
    j = pl.program_id(1)

    @pl.when(j == 0)
    def _():
        xb_ref[...] = x_ref[...].astype(BF16)

    @pl.when(j < u_tile0)
    def _():
        acc = jnp.dot(xb_ref[...], w_ref[...], preferred_element_type=F32)
        o_ref[...] = (acc * s_ref[...]).astype(o_ref.dtype)

    @pl.when((j >= u_tile0) & (j < gate_tile0))
    def _():
        acc = jnp.dot(xb_ref[...], w_ref[...], preferred_element_type=F32)
        o_ref[...] = acc.astype(o_ref.dtype)
        u_ref[...] = acc

    @pl.when(j >= gate_tile0)
    def _():
        acc = jnp.dot(xb_ref[...], w_ref[...], preferred_element_type=F32)
        o_ref[...] = jax.nn.sigmoid(acc + b_ref[...]).astype(o_ref.dtype)


def _inproj(x, w, col_scale, col_bias, u_col0, gate_col0, tm, tn):
    m, k = x.shape
    n = w.shape[1]
    u_tile0, gate_tile0 = u_col0 // tn, gate_col0 // tn
    return pl.pallas_call(
        functools.partial(_inproj_kernel, u_tile0=u_tile0, gate_tile0=gate_tile0),
        grid=(m // tm, n // tn),
        in_specs=[
            pl.BlockSpec((tm, k), lambda i, j: (i, 0)),
            pl.BlockSpec((k, tn), lambda i, j: (0, j)),
            pl.BlockSpec((1, tn), lambda i, j: (0, j)),
            pl.BlockSpec((1, tn), lambda i, j: (0, j)),
        ],
        out_specs=[pl.BlockSpec((tm, tn), lambda i, j: (i, j)),
                   pl.BlockSpec((tm, tn), lambda i, j: (i, jnp.clip(j - u_tile0, 0, gate_tile0 - u_tile0 - 1)))],
        out_shape=[jax.ShapeDtypeStruct((m, n), BF16),
                   jax.ShapeDtypeStruct((m, gate_col0 - u_col0), F32)],
        scratch_shapes=[pltpu.VMEM((tm, k), BF16)],
        compiler_params=_params(("parallel", "arbitrary")),
        name="inproj",
    )(x, w, col_scale, col_bias)


def _attn_windows(rows):
    nb = rows // ATT_Q_ROWS
    kinds = []
    for b in (0, 1, nb - 1):
        base = min(max(b * ATT_Q_ROWS - NA_ROWS // 2, 0), rows - ATT_K_ROWS)
        geo = []
        for ql in range(ATT_Q_ROWS):
            qr = b * ATT_Q_ROWS + ql
            r0 = min(max(qr - NA_ROWS // 2, 0), rows - NA_ROWS)
            koff = r0 - base
            lane0 = min(LANES * (koff * GRID_W // LANES), ATT_K - ATT_WIN)
            geo.append((lane0, koff * GRID_W - lane0, r0 - qr + NA_ROWS - 1))
        kinds.append(tuple(geo))
    return tuple(kinds)


def _attn_kernel(q_ref, k0, k1, k2, k3, v0, v1, v2, v3, bias_ref, o_ref, p_ref, *, windows, nb):
    b = pl.program_id(1)
    k = jnp.concatenate([k0[...], k1[...], k2[...], k3[...]], axis=0)
    v = jnp.concatenate([v0[...], v1[...], v2[...], v3[...]], axis=0)
    s = lax.dot_general(q_ref[...], k, (((1,), (1,)), ((), ())), preferred_element_type=F32)

    def softmax_rows(geo):
        inv = []
        for ql, (lane0, _, _) in enumerate(geo):
            r = slice(ql * GRID_W, (ql + 1) * GRID_W)
            sw = s[r, lane0:lane0 + ATT_WIN] + bias_ref[ql]
            m = jnp.max(sw, axis=-1, keepdims=True)
            p = jnp.exp(sw - m)
            inv.append(1.0 / jnp.sum(p, axis=-1, keepdims=True))
            if lane0 > 0:
                p_ref[r, :lane0] = jnp.zeros((GRID_W, lane0), BF16)
            p_ref[r, lane0:lane0 + ATT_WIN] = p.astype(BF16)
            if lane0 + ATT_WIN < ATT_K:
                p_ref[r, lane0 + ATT_WIN:] = jnp.zeros((GRID_W, ATT_K - lane0 - ATT_WIN), BF16)
        o = jnp.dot(p_ref[...], v, preferred_element_type=F32)
        o_ref[...] = (o * jnp.concatenate(inv, axis=0)).astype(o_ref.dtype)

    @pl.when(b == 0)
    def _():
        softmax_rows(windows[0])

    @pl.when((b > 0) & (b < nb - 1))
    def _():
        softmax_rows(windows[1])

    @pl.when(b == nb - 1)
    def _():
        softmax_rows(windows[2])


def _attention(proj, bias, seq):
    nb = seq // ATT_Q
    n_kblk = seq // ATT_KBLK
    per_blk = ATT_K // ATT_KBLK

    def kstart(b):
        return jnp.clip(2 * b - 1, 0, n_kblk - per_blk)

    def kind(b):
        return jnp.where(b == 0, 0, jnp.where(b == nb - 1, 2, 1))

    def kv_spec(col0, t):
        return pl.BlockSpec((ATT_KBLK, NA_HEAD_DIM), lambda h, b: (kstart(b) + t, col0 + h))

    in_specs = [pl.BlockSpec((ATT_Q, NA_HEAD_DIM), lambda h, b: (b, h))]
    in_specs += [kv_spec(NA_HEADS, t) for t in range(per_blk)]
    in_specs += [kv_spec(2 * NA_HEADS, t) for t in range(per_blk)]
    in_specs += [pl.BlockSpec((None, None, ATT_Q_ROWS, GRID_W, ATT_WIN), lambda h, b: (kind(b), h, 0, 0, 0))]
    return pl.pallas_call(
        functools.partial(_attn_kernel, windows=_attn_windows(seq // GRID_W), nb=nb),
        grid=(NA_HEADS, nb),
        in_specs=in_specs,
        out_specs=pl.BlockSpec((ATT_Q, NA_HEAD_DIM), lambda h, b: (b, h)),
        out_shape=jax.ShapeDtypeStruct((seq, NA_WIDTH), BF16),
        scratch_shapes=[pltpu.VMEM((ATT_Q, ATT_K), BF16)],
        compiler_params=_params(("parallel", "arbitrary")),
        name="na_attention",
    )(*([proj] * (1 + 2 * per_blk)), bias)


def _attention_bias(rpb, rows):
    cols = jnp.arange(GRID_W)
    col_start = jnp.clip(cols - NA_COLS // 2, 0, GRID_W - NA_COLS)
    col_ok = (cols[None, :] >= col_start[:, None]) & (cols[None, :] < col_start[:, None] + NA_COLS)
    col_idx = jnp.clip(cols[None, :] - cols[:, None], -(NA_COLS - 1), NA_COLS - 1) + (NA_COLS - 1)
    by_col = jnp.take(rpb.astype(F32), col_idx.reshape(-1), axis=2)
    by_col = by_col.reshape(NA_HEADS, 2 * NA_ROWS - 1, GRID_W, GRID_W)
    by_col = jnp.where(col_ok[None, None], by_col, MASK_VALUE).transpose(0, 2, 1, 3)
    kinds = []
    for geo in _attn_windows(rows):
        strips = []
        for _, phase, rfirst in geo:
            strip = by_col[:, :, rfirst:rfirst + NA_ROWS, :].reshape(NA_HEADS, GRID_W, NA_ROWS * GRID_W)
            strips.append(jnp.pad(strip, ((0, 0), (0, 0), (phase, ATT_WIN - NA_ROWS * GRID_W - phase)),
                                  constant_values=MASK_VALUE))
        kinds.append(jnp.stack(strips, axis=1))
    return jnp.stack(kinds)


def _s5_tables(a_re, a_im, log_dt, b_re, b_im, c_re, c_im, d):
    t_len = S5_CHUNK
    groups = a_re.shape[1]
    n = jnp.arange(t_len + 1, dtype=F32)[None, :, None]

    def cmul(xr, xi, yr, yi):
        return xr * yr - xi * yi, xr * yi + xi * yr

    swap = lambda v: jnp.swapaxes(v, 1, 2)
    w_cols, m_rows, lag_kernels, step_r, step_i = [], [], [], [], []
    for k in range(2):
        ar, ai = a_re[k].astype(F32), a_im[k].astype(F32)
        dt = jnp.exp(log_dt[k].astype(F32))[:, None]
        zr, zi = (ar * dt)[:, None, :], (ai * dt)[:, None, :]
        mag = jnp.exp(zr * n)
        pr, pi = mag * jnp.cos(zi * n), mag * jnp.sin(zi * n)
        lr, li = pr[:, 1] - 1.0, pi[:, 1]
        den = ar * ar + ai * ai
        fr, fi = (lr * ar + li * ai) / den, (li * ar - lr * ai) / den
        bbr, bbi = cmul(fr[:, None, :], fi[:, None, :],
                        swap(b_re[k].astype(F32)), swap(b_im[k].astype(F32)))
        crt, cit = swap(c_re[k].astype(F32)), swap(c_im[k].astype(F32))
        prt, pit = swap(pr), swap(pi)
        wp_r = pr[:, :t_len][:, ::-1] if k == 0 else pr[:, :t_len]
        wp_i = pi[:, :t_len][:, ::-1] if k == 0 else pi[:, :t_len]
        wr, wi = cmul(wp_r[:, :, None, :], wp_i[:, :, None, :], bbr[:, None, :, :], bbi[:, None, :, :])
        w_cols.append((wr.reshape(groups, S5_CHUNK_WIDTH, S5_STATE), wi.reshape(groups, S5_CHUNK_WIDTH, S5_STATE)))
        ep_r = prt[:, :, 1:] if k == 0 else prt[:, :, 1:][:, :, ::-1]
        ep_i = pit[:, :, 1:] if k == 0 else pit[:, :, 1:][:, :, ::-1]
        er, ei = cmul(crt[:, :, None, :], cit[:, :, None, :], ep_r[:, :, :, None], ep_i[:, :, :, None])
        m_rows.append((er.reshape(groups, S5_STATE, S5_CHUNK_WIDTH), (-ei).reshape(groups, S5_STATE, S5_CHUNK_WIDTH)))
        xr, xi = cmul(crt[:, :, None, :], cit[:, :, None, :], prt[:, :, :t_len, None], pit[:, :, :t_len, None])
        lhs = jnp.concatenate([bbr, -bbi], axis=-1)
        rhs = jnp.concatenate([xr, xi], axis=1).reshape(groups, 2 * S5_STATE, S5_CHUNK_WIDTH)
        lag_kernels.append(jnp.einsum("gck,gkn->gcn", lhs, rhs, precision=lax.Precision.HIGHEST))
        step_r.append(pr[:, t_len])
        step_i.append(pi[:, t_len])

    w_state = jnp.concatenate([w_cols[0][0], w_cols[1][0], w_cols[0][1], w_cols[1][1]], axis=-1)
    m_state = jnp.concatenate([m_rows[0][0], m_rows[1][0], m_rows[0][1], m_rows[1][1]], axis=1)
    kf, kb = lag_kernels
    kb_rev = kb[:, :, S5_GROUP:].reshape(groups, S5_GROUP, t_len - 1, S5_GROUP)[:, :, ::-1]
    lag0 = kf[:, :, :S5_GROUP] + kb[:, :, :S5_GROUP] + d.astype(F32)[:, None, :] * jnp.eye(S5_GROUP, dtype=F32)[None]
    lags = jnp.concatenate([kb_rev.reshape(groups, S5_GROUP, -1), lag0, kf[:, :, S5_GROUP:]], axis=-1)
    toe = jnp.stack([lags[:, :, S5_GROUP * (t_len - 1 - s):S5_GROUP * (2 * t_len - 1 - s)] for s in range(t_len)],
                    axis=1).reshape(groups, S5_CHUNK_WIDTH, S5_CHUNK_WIDTH)
    m_out = jnp.concatenate([toe, m_state], axis=1)
    return (w_state.astype(BF16), m_out.astype(BF16),
            jnp.concatenate(step_r, axis=-1), jnp.concatenate(step_i, axis=-1))


def _lane_group_ids():
    return lax.broadcasted_iota(jnp.int32, (RELAYOUT_ROWS, LANES), 1) // S5_GROUP


def _s5_state_kernel(u_ref, w_ref, s_ref, ug_ref):
    nch = ug_ref.shape[1]
    grp = _lane_group_ids()

    def regroup(rb, carry):
        r0 = pl.multiple_of(rb * RELAYOUT_ROWS, RELAYOUT_ROWS)
        xs = [u_ref[pl.ds(r0 * S5_CHUNK + t, RELAYOUT_ROWS, stride=S5_CHUNK), :] for t in range(S5_CHUNK)]
        for g in range(S5_LANE_GROUPS):
            for half in range(S5_CHUNK_WIDTH // LANES):
                acc = None
                for tp in range(S5_LANE_GROUPS):
                    shift = (S5_GROUP * (tp - g)) % LANES
                    x = xs[half * S5_LANE_GROUPS + tp]
                    x = pltpu.roll(x, shift, 1) if shift else x
                    acc = x if acc is None else jnp.where(grp == tp, x, acc)
                ug_ref[g, pl.ds(r0, RELAYOUT_ROWS), half * LANES:(half + 1) * LANES] = acc.astype(BF16)
        return carry

    lax.fori_loop(0, nch // RELAYOUT_ROWS, regroup, 0)
    for g in range(S5_LANE_GROUPS):
        s_ref[:, g * S5_STATE_WIDTH:(g + 1) * S5_STATE_WIDTH] = jnp.dot(
            ug_ref[g], w_ref[g], preferred_element_type=F32)


def _s5_chunk_states(u, w_state, nsplit):
    seq, width = u.shape
    g = width // S5_GROUP
    nc = seq // S5_CHUNK
    nch = nc // nsplit
    lg = S5_LANE_GROUPS
    return pl.pallas_call(
        _s5_state_kernel,
        grid=(width // LANES, nsplit),
        in_specs=[pl.BlockSpec((nch * S5_CHUNK, LANES), lambda j, h: (h, j)),
                  pl.BlockSpec((lg, S5_CHUNK_WIDTH, S5_STATE_WIDTH), lambda j, h: (j, 0, 0))],
        out_specs=[pl.BlockSpec((nch, lg * S5_STATE_WIDTH), lambda j, h: (h, j)),
                   pl.BlockSpec((lg, nch, S5_CHUNK_WIDTH), lambda j, h: (j, h, 0))],
        out_shape=[jax.ShapeDtypeStruct((nc, g * S5_STATE_WIDTH), F32),
                   jax.ShapeDtypeStruct((g, nc, S5_CHUNK_WIDTH), BF16)],
        compiler_params=_params(("parallel", "parallel")),
        name="s5_chunk_states",
    )(u, w_state)


def _s5_scan_kernel(s_ref, ar_ref, ai_ref, h_ref, *, nc):
    half = S5_STATE_WIDTH // 2
    lane = lax.broadcasted_iota(jnp.int32, (SCAN_GROUPS, half), 1)
    fwd = lane < S5_STATE
    ar = ar_ref[...]
    ai = ai_ref[...]

    def step(i, carry):
        hr, hi = carry
        sf = s_ref[i]
        sb = s_ref[nc - 1 - i]
        h_ref[i, :, :half] = hr
        h_ref[i, :, half:] = hi
        sr = jnp.where(fwd, sf[:, :half], sb[:, :half])
        si = jnp.where(fwd, sf[:, half:], sb[:, half:])
        return ar * hr - ai * hi + sr, ar * hi + ai * hr + si

    zero = jnp.zeros((SCAN_GROUPS, half), F32)
    lax.fori_loop(0, nc, step, (zero, zero))

    fwd2 = jnp.concatenate([fwd, fwd], axis=1)

    def swap(i, carry):
        a = h_ref[i]
        b = h_ref[nc - 1 - i]
        h_ref[i] = jnp.where(fwd2, a, b)
        h_ref[nc - 1 - i] = jnp.where(fwd2, b, a)
        return carry

    lax.fori_loop(0, nc // 2, swap, 0)


def _s5_scan(s3, a_re, a_im):
    nc, g, sw = s3.shape
    return pl.pallas_call(
        functools.partial(_s5_scan_kernel, nc=nc),
        grid=(g // SCAN_GROUPS,),
        in_specs=[pl.BlockSpec((nc, SCAN_GROUPS, sw), lambda i: (0, i, 0)),
                  pl.BlockSpec((SCAN_GROUPS, sw // 2), lambda i: (i, 0)),
                  pl.BlockSpec((SCAN_GROUPS, sw // 2), lambda i: (i, 0))],
        out_specs=pl.BlockSpec((nc, SCAN_GROUPS, sw), lambda i: (0, i, 0)),
        out_shape=jax.ShapeDtypeStruct((nc, g, sw), F32),
        compiler_params=_params(("parallel",)),
        name="s5_scan",
    )(s3, a_re, a_im)


def _s5_out_kernel(ug_ref, h_ref, m_ref, o_ref, y_ref):
    cw = S5_CHUNK_WIDTH
    nch = ug_ref.shape[1]
    for g in range(S5_LANE_GROUPS):
        hg = h_ref[:, g * S5_STATE_WIDTH:(g + 1) * S5_STATE_WIDTH].astype(BF16)
        y = jnp.dot(ug_ref[g], m_ref[g, :cw, :], preferred_element_type=F32)
        y = y + jnp.dot(hg, m_ref[g, cw:, :], preferred_element_type=F32)
        y_ref[g] = jax.nn.gelu(y, approximate=True)
    grp = _lane_group_ids()

    def regroup(rb, carry):
        r0 = pl.multiple_of(rb * RELAYOUT_ROWS, RELAYOUT_ROWS)
        for half in range(cw // LANES):
            ys = [y_ref[g, pl.ds(r0, RELAYOUT_ROWS), half * LANES:(half + 1) * LANES]
                  for g in range(S5_LANE_GROUPS)]
            for tp in range(S5_LANE_GROUPS):
                acc = None
                for g in range(S5_LANE_GROUPS):
                    shift = (S5_GROUP * (g - tp)) % LANES
                    x = pltpu.roll(ys[g], shift, 1) if shift else ys[g]
                    acc = x if acc is None else jnp.where(grp == g, x, acc)
                t = half * S5_LANE_GROUPS + tp
                o_ref[pl.ds(r0 * S5_CHUNK + t, RELAYOUT_ROWS, stride=S5_CHUNK), :] = acc
        return carry

    lax.fori_loop(0, nch // RELAYOUT_ROWS, regroup, 0)


def _s5_outputs(ug, h2, m_out, nsplit):
    g, nc, cw = ug.shape
    nch = nc // nsplit
    lg = S5_LANE_GROUPS
    return pl.pallas_call(
        _s5_out_kernel,
        grid=(g // lg, nsplit),
        in_specs=[pl.BlockSpec((lg, nch, cw), lambda j, h: (j, h, 0)),
                  pl.BlockSpec((nch, lg * S5_STATE_WIDTH), lambda j, h: (h, j)),
                  pl.BlockSpec((lg, cw + S5_STATE_WIDTH, cw), lambda j, h: (j, 0, 0))],
        out_specs=pl.BlockSpec((nch * S5_CHUNK, LANES), lambda j, h: (h, j)),
        out_shape=jax.ShapeDtypeStruct((nc * S5_CHUNK, g * S5_GROUP), F32),
        scratch_shapes=[pltpu.VMEM((lg, nch, cw), F32)],
        compiler_params=_params(("parallel", "parallel")),
        name="s5_outputs",
    )(ug, h2, m_out)


def _glu_kernel(a_ref, w_ref, b_ref, y_ref, o_ref, ab_ref):
    @pl.when(pl.program_id(1) == 0)
    def _():
        ab_ref[...] = a_ref[...].astype(BF16)

    t = jnp.dot(ab_ref[...], w_ref[...], preferred_element_type=F32) + b_ref[...]
    o_ref[...] = (y_ref[...] * jax.nn.sigmoid(t)).astype(o_ref.dtype)


def _glu(ys, w, b, tm, tn):
    m, k = ys.shape
    n = w.shape[1]
    return pl.pallas_call(
        _glu_kernel,
        grid=(m // tm, n // tn),
        in_specs=[pl.BlockSpec((tm, k), lambda i, j: (i, 0)),
                  pl.BlockSpec((k, tn), lambda i, j: (0, j)),
                  pl.BlockSpec((1, tn), lambda i, j: (0, j)),
                  pl.BlockSpec((tm, tn), lambda i, j: (i, j))],
        out_specs=pl.BlockSpec((tm, tn), lambda i, j: (i, j)),
        out_shape=jax.ShapeDtypeStruct((m, n), BF16),
        scratch_shapes=[pltpu.VMEM((tm, k), BF16)],
        compiler_params=_params(("parallel", "arbitrary")),
        name="s5_glu",
    )(ys, w, b, ys)


def _merge_kernel(a1_ref, w1_ref, a2_ref, w2_ref, g1_ref, g2_ref, o_ref):
    y1 = jnp.dot(a1_ref[...], w1_ref[...], preferred_element_type=F32)
    y2 = jnp.dot(a2_ref[...], w2_ref[...], preferred_element_type=F32)
    o_ref[...] = (g1_ref[...].astype(F32) * y1 + g2_ref[...].astype(F32) * y2).astype(o_ref.dtype)


def _merge(att, w_na, z, w_s5, proj, gate_col0, tm, tn):
    m, k1 = att.shape
    k2 = z.shape[1]
    n = w_na.shape[1]
    g1 = gate_col0 // tn
    g2 = (gate_col0 + n) // tn
    return pl.pallas_call(
        _merge_kernel,
        grid=(m // tm, n // tn),
        in_specs=[pl.BlockSpec((tm, k1), lambda i, j: (i, 0)),
                  pl.BlockSpec((k1, tn), lambda i, j: (0, j)),
                  pl.BlockSpec((tm, k2), lambda i, j: (i, 0)),
                  pl.BlockSpec((k2, tn), lambda i, j: (0, j)),
                  pl.BlockSpec((tm, tn), lambda i, j: (i, g1 + j)),
                  pl.BlockSpec((tm, tn), lambda i, j: (i, g2 + j))],
        out_specs=pl.BlockSpec((tm, tn), lambda i, j: (i, j)),
        out_shape=jax.ShapeDtypeStruct((m, n), BF16),
        compiler_params=_params(("parallel", "arbitrary")),
        name="gated_merge",
    )(att, w_na, z, w_s5, proj, proj)


def _outproj_ln_kernel(a_ref, w_ref, x_ref, g_ref, b_ref, h_ref, *, alpha):
    mix = jnp.dot(a_ref[...], w_ref[...], preferred_element_type=F32)
    h_ref[...] = _layer_norm_rows(alpha * x_ref[...] + mix, g_ref[...], b_ref[...])


def _outproj_ln(a, w, x, g, b, alpha, tm):
    m, k = a.shape
    n = w.shape[1]
    return pl.pallas_call(
        functools.partial(_outproj_ln_kernel, alpha=alpha),
        grid=(m // tm,),
        in_specs=[pl.BlockSpec((tm, k), lambda i: (i, 0)),
                  pl.BlockSpec((k, n), lambda i: (0, 0)),
                  pl.BlockSpec((tm, n), lambda i: (i, 0)),
                  pl.BlockSpec((1, n), lambda i: (0, 0)),
                  pl.BlockSpec((1, n), lambda i: (0, 0))],
        out_specs=pl.BlockSpec((tm, n), lambda i: (i, 0)),
        out_shape=jax.ShapeDtypeStruct((m, n), F32),
        compiler_params=_params(("parallel",)),
        name="outproj_ln1",
    )(a, w, x, g, b)


def _ffn_act_kernel(h_ref, wg_ref, wu_ref, o_ref, hb_ref):
    @pl.when(pl.program_id(1) == 0)
    def _():
        hb_ref[...] = h_ref[...].astype(BF16)

    hb = hb_ref[...]
    gate = jnp.dot(hb, wg_ref[...], preferred_element_type=F32)
    up = jnp.dot(hb, wu_ref[...], preferred_element_type=F32)
    o_ref[...] = (jax.nn.silu(gate) * up).astype(o_ref.dtype)


def _ffn_act(h, wg, wu, tm, tf):
    m, d = h.shape
    f = wg.shape[1]
    return pl.pallas_call(
        _ffn_act_kernel,
        grid=(m // tm, f // tf),
        in_specs=[pl.BlockSpec((tm, d), lambda i, j: (i, 0)),
                  pl.BlockSpec((d, tf), lambda i, j: (0, j)),
                  pl.BlockSpec((d, tf), lambda i, j: (0, j))],
        out_specs=pl.BlockSpec((tm, tf), lambda i, j: (i, j)),
        out_shape=jax.ShapeDtypeStruct((m, f), BF16),
        scratch_shapes=[pltpu.VMEM((tm, d), BF16)],
        compiler_params=_params(("parallel", "arbitrary")),
        name="swiglu_act",
    )(h, wg, wu)


def _ffn_down_ln_kernel(a_ref, w_ref, h_ref, g_ref, b_ref, o_ref, *, alpha):
    ff = jnp.dot(a_ref[...], w_ref[...], preferred_element_type=F32)
    o_ref[...] = _layer_norm_rows(alpha * h_ref[...] + ff, g_ref[...], b_ref[...])


def _ffn_down_ln(act, wd, h, g, b, alpha, tm):
    m, f = act.shape
    d = wd.shape[1]
    return pl.pallas_call(
        functools.partial(_ffn_down_ln_kernel, alpha=alpha),
        grid=(m // tm,),
        in_specs=[pl.BlockSpec((tm, f), lambda i: (i, 0)),
                  pl.BlockSpec((f, d), lambda i: (0, 0), pipeline_mode=pl.Buffered(1)),
                  pl.BlockSpec((tm, d), lambda i: (i, 0)),
                  pl.BlockSpec((1, d), lambda i: (0, 0)),
                  pl.BlockSpec((1, d), lambda i: (0, 0))],
        out_specs=pl.BlockSpec((tm, d), lambda i: (i, 0)),
        out_shape=jax.ShapeDtypeStruct((m, d), F32),
        compiler_params=_params(("parallel",)),
        name="swiglu_down_ln2",
    )(act, wd, h, g, b)


def _layer(x2, w_in, b_gate, na_rpb, w_na_out, s5_a_re, s5_a_im, s5_log_dt, s5_b_re, s5_b_im,
           s5_c_re, s5_c_im, s5_d, w_glu, b_glu, w_s5_out, w_out, ln1_g, ln1_b,
           w_ffn_gate, w_ffn_up, w_ffn_down, ln2_g, ln2_b, alpha):
    seq, d_model = x2.shape
    rows = seq // GRID_W
    nc = seq // S5_CHUNK
    gate_col0 = 3 * NA_WIDTH + S5_WIDTH
    in_cols = gate_col0 + 2 * d_model
    row = lambda v: v.astype(F32).reshape(1, -1)

    col_scale = jnp.where(jnp.arange(in_cols) < NA_WIDTH, NA_HEAD_DIM ** -0.5, 1.0).astype(F32).reshape(1, -1)
    col_bias = jnp.concatenate([jnp.zeros((gate_col0,), F32), b_gate.astype(F32)]).reshape(1, -1)
    proj, u = _inproj(x2, w_in.astype(BF16), col_scale, col_bias, 3 * NA_WIDTH, gate_col0, tm=1024, tn=512)

    att = _attention(proj, _attention_bias(na_rpb, rows), seq)

    w_state, m_out, a_re, a_im = _s5_tables(s5_a_re, s5_a_im, s5_log_dt, s5_b_re, s5_b_im,
                                            s5_c_re, s5_c_im, s5_d)
    s, ug = _s5_chunk_states(u, w_state, nsplit=2)
    h_states = _s5_scan(s.reshape(nc, S5_GROUPS, S5_STATE_WIDTH), a_re, a_im)
    ys = _s5_outputs(ug, h_states.reshape(nc, S5_GROUPS * S5_STATE_WIDTH), m_out, nsplit=2)

    z = _glu(ys, w_glu.astype(BF16), row(b_glu), tm=1024, tn=512)
    merged = _merge(att, w_na_out.astype(BF16), z, w_s5_out.astype(BF16), proj, gate_col0, tm=1024, tn=512)
    h = _outproj_ln(merged, w_out.astype(BF16), x2, row(ln1_g), row(ln1_b), alpha, tm=512)
    act = _ffn_act(h, w_ffn_gate.astype(BF16), w_ffn_up.astype(BF16), tm=1024, tf=512)
    return _ffn_down_ln(act, w_ffn_down.astype(BF16), h, row(ln2_g), row(ln2_b), alpha, tm=512)


def kernel(x, w_in, b_gate, na_rpb, w_na_out, s5_a_re, s5_a_im, s5_log_dt, s5_b_re, s5_b_im, s5_c_re, s5_c_im, s5_d, w_glu, b_glu, w_s5_out, w_out, ln1_g, ln1_b, w_ffn_gate, w_ffn_up, w_ffn_down, ln2_g, ln2_b):
    bsz, seq, d_model = x.shape
    depth = w_in.shape[0]
    alpha = (2.0 * depth) ** 0.25
    outs = []
    for bi in range(bsz):
        xb = x[bi]
        for l in range(depth):
            xb = _layer(xb, w_in[l], b_gate[l], na_rpb[l], w_na_out[l], s5_a_re[l], s5_a_im[l], s5_log_dt[l],
                        s5_b_re[l], s5_b_im[l], s5_c_re[l], s5_c_im[l], s5_d[l], w_glu[l], b_glu[l],
                        w_s5_out[l], w_out[l], ln1_g[l], ln1_b[l], w_ffn_gate[l], w_ffn_up[l],
                        w_ffn_down[l], ln2_g[l], ln2_b[l], alpha)
        outs.append(xb)
    return jnp.stack(outs)
```

```python
import functools
import math

import jax
import jax.numpy as jnp
from jax import lax
from jax.experimental import pallas as pl
from jax.experimental.pallas import tpu as pltpu

F32 = jnp.float32
BF16 = jnp.bfloat16

GRID_W = 64
NA_HEADS = 8
NA_HEAD_DIM = 128
NA_WIDTH = NA_HEADS * NA_HEAD_DIM
NA_ROWS = 8
NA_COLS = 16
S5_GROUP = 16
S5_GROUPS = 64
S5_WIDTH = S5_GROUP * S5_GROUPS
S5_STATE = 64
LN_EPS = 1e-5
MASK_VALUE = -1e30

S5_CHUNK = 16
S5_CHUNK_WIDTH = S5_CHUNK * S5_GROUP
S5_STATE_WIDTH = 4 * S5_STATE
SCAN_GROUPS = 8

ATT_Q_ROWS = 8
ATT_K_ROWS = 16
ATT_Q = ATT_Q_ROWS * GRID_W
ATT_K = ATT_K_ROWS * GRID_W
ATT_KBLK = 256
LANES = 128
ATT_WIN = NA_ROWS * GRID_W + LANES
S5_LANE_GROUPS = LANES // S5_GROUP
RELAYOUT_ROWS = 16

VMEM_LIMIT = 56 * 1024 * 1024


def _params(sem):
    return pltpu.CompilerParams(dimension_semantics=sem, vmem_limit_bytes=VMEM_LIMIT)


def _layer_norm_rows(y, g, b):
    mu = jnp.mean(y, axis=-1, keepdims=True)
    d = y - mu
    var = jnp.mean(d * d, axis=-1, keepdims=True)
    return d * lax.rsqrt(var + LN_EPS) * g + b


def _inproj_kernel(x_ref, w_ref, s_ref, b_ref, o_ref, u_ref, xb_ref, *, u_tile0, gate_tile0):
    j = pl.program_id(1)

    @pl.when(j == 0)
    def _():
        xb_ref[...] = x_ref[...].astype(BF16)

    @pl.when(j < u_tile0)
    def _():
        acc = jnp.dot(xb_ref[...], w_ref[...], preferred_element_type=F32)
        o_ref[...] = (acc * s_ref[...]).astype(o_ref.dtype)

    @pl.when((j >= u_tile0) & (j < gate_tile0))
    def _():
        acc = jnp.dot(xb_ref[...], w_ref[...], preferred_element_type=F32)
        o_ref[...] = acc.astype(o_ref.dtype)
        u_ref[...] = acc

    @pl.when(j >= gate_tile0)
    def _():
        acc = jnp.dot(xb_ref[...], w_ref[...], preferred_element_type=F32)
        o_ref[...] = jax.nn.sigmoid(acc + b_ref[...]).astype(o_ref.dtype)


def _inproj(x, w, col_scale, col_bias, u_col0, gate_col0, tm, tn):
    m, k = x.shape
    n = w.shape[1]
    u_tile0, gate_tile0 = u_col0 // tn, gate_col0 // tn
    return pl.pallas_call(
        functools.partial(_inproj_kernel, u_tile0=u_tile0, gate_tile0=gate_tile0),
        grid=(m // tm, n // tn),
        in_specs=[
            pl.BlockSpec((tm, k), lambda i, j: (i, 0)),
            pl.BlockSpec((k, tn), lambda i, j: (0, j)),
            pl.BlockSpec((1, tn), lambda i, j: (0, j)),
            pl.BlockSpec((1, tn), lambda i, j: (0, j)),
        ],
        out_specs=[pl.BlockSpec((tm, tn), lambda i, j: (i, j)),
                   pl.BlockSpec((tm, tn), lambda i, j: (i, jnp.clip(j - u_tile0, 0, gate_tile0 - u_tile0 - 1)))],
        out_shape=[jax.ShapeDtypeStruct((m, n), BF16),
                   jax.ShapeDtypeStruct((m, gate_col0 - u_col0), F32)],
        scratch_shapes=[pltpu.VMEM((tm, k), BF16)],
        compiler_params=_params(("parallel", "arbitrary")),
        name="inproj",
    )(x, w, col_scale, col_bias)


def _attn_windows(rows):
    nb = rows // ATT_Q_ROWS
    kinds = []
    for b in (0, 1, nb - 1):
        base = min(max(b * ATT_Q_ROWS - NA_ROWS // 2, 0), rows - ATT_K_ROWS)
        geo = []
        for ql in range(ATT_Q_ROWS):
            qr = b * ATT_Q_ROWS + ql
            r0 = min(max(qr - NA_ROWS // 2, 0), rows - NA_ROWS)
            koff = r0 - base
            lane0 = min(LANES * (koff * GRID_W // LANES), ATT_K - ATT_WIN)
            geo.append((lane0, koff * GRID_W - lane0, r0 - qr + NA_ROWS - 1))
        kinds.append(tuple(geo))
    return tuple(kinds)


def _attn_kernel(q_ref, k0, k1, k2, k3, v0, v1, v2, v3, bias_ref, o_ref, p_ref, *, windows, nb):
    b = pl.program_id(1)
    k = jnp.concatenate([k0[...], k1[...], k2[...], k3[...]], axis=0)
    v = jnp.concatenate([v0[...], v1[...], v2[...], v3[...]], axis=0)
    s = lax.dot_general(q_ref[...], k, (((1,), (1,)), ((), ())), preferred_element_type=F32)

    def softmax_rows(geo):
        inv = []
        for ql, (lane0, _, _) in enumerate(geo):
            r = slice(ql * GRID_W, (ql + 1) * GRID_W)
            sw = s[r, lane0:lane0 + ATT_WIN] + bias_ref[ql]
            m = jnp.max(sw, axis=-1, keepdims=True)
            p = jnp.exp(sw - m)
            inv.append(1.0 / jnp.sum(p, axis=-1, keepdims=True))
            if lane0 > 0:
                p_ref[r, :lane0] = jnp.zeros((GRID_W, lane0), BF16)
            p_ref[r, lane0:lane0 + ATT_WIN] = p.astype(BF16)
            if lane0 + ATT_WIN < ATT_K:
                p_ref[r, lane0 + ATT_WIN:] = jnp.zeros((GRID_W, ATT_K - lane0 - ATT_WIN), BF16)
        o = jnp.dot(p_ref[...], v, preferred_element_type=F32)
        o_ref[...] = (o * jnp.concatenate(inv, axis=0)).astype(o_ref.dtype)

    @pl.when(b == 0)
    def _():
        softmax_rows(windows[0])

    @pl.when((b > 0) & (b < nb - 1))
    def _():
        softmax_rows(windows[1])

    @pl.when(b == nb - 1)
    def _():
        softmax_rows(windows[2])


def _attention(proj, bias, seq):
    nb = seq // ATT_Q
    n_kblk = seq // ATT_KBLK
    per_blk = ATT_K // ATT_KBLK

    def kstart(b):
        return jnp.clip(2 * b - 1, 0, n_kblk - per_blk)

    def kind(b):
        return jnp.where(b == 0, 0, jnp.where(b == nb - 1, 2, 1))

    def kv_spec(col0, t):
        return pl.BlockSpec((ATT_KBLK, NA_HEAD_DIM), lambda h, b: (kstart(b) + t, col0 + h))

    in_specs = [pl.BlockSpec((ATT_Q, NA_HEAD_DIM), lambda h, b: (b, h))]
    in_specs += [kv_spec(NA_HEADS, t) for t in range(per_blk)]
    in_specs += [kv_spec(2 * NA_HEADS, t) for t in range(per_blk)]
    in_specs += [pl.BlockSpec((None, None, ATT_Q_ROWS, GRID_W, ATT_WIN), lambda h, b: (kind(b), h, 0, 0, 0))]
    return pl.pallas_call(
        functools.partial(_attn_kernel, windows=_attn_windows(seq // GRID_W), nb=nb),
        grid=(NA_HEADS, nb),
        in_specs=in_specs,
        out_specs=pl.BlockSpec((ATT_Q, NA_HEAD_DIM), lambda h, b: (b, h)),
        out_shape=jax.ShapeDtypeStruct((seq, NA_WIDTH), BF16),
        scratch_shapes=[pltpu.VMEM((ATT_Q, ATT_K), BF16)],
        compiler_params=_params(("parallel", "arbitrary")),
        name="na_attention",
    )(*([proj] * (1 + 2 * per_blk)), bias)


def _attention_bias(rpb, rows):
    cols = jnp.arange(GRID_W)
    col_start = jnp.clip(cols - NA_COLS // 2, 0, GRID_W - NA_COLS)
    col_ok = (cols[None, :] >= col_start[:, None]) & (cols[None, :] < col_start[:, None] + NA_COLS)
    col_idx = jnp.clip(cols[None, :] - cols[:, None], -(NA_COLS - 1), NA_COLS - 1) + (NA_COLS - 1)
    by_col = jnp.take(rpb.astype(F32), col_idx.reshape(-1), axis=2)
    by_col = by_col.reshape(NA_HEADS, 2 * NA_ROWS - 1, GRID_W, GRID_W)
    by_col = jnp.where(col_ok[None, None], by_col, MASK_VALUE).transpose(0, 2, 1, 3)
    kinds = []
    for geo in _attn_windows(rows):
        strips = []
        for _, phase, rfirst in geo:
            strip = by_col[:, :, rfirst:rfirst + NA_ROWS, :].reshape(NA_HEADS, GRID_W, NA_ROWS * GRID_W)
            strips.append(jnp.pad(strip, ((0, 0), (0, 0), (phase, ATT_WIN - NA_ROWS * GRID_W - phase)),
                                  constant_values=MASK_VALUE))
        kinds.append(jnp.stack(strips, axis=1))
    return jnp.stack(kinds)


def _s5_tables(a_re, a_im, log_dt, b_re, b_im, c_re, c_im, d):
    t_len = S5_CHUNK
    groups = a_re.shape[1]
    n = jnp.arange(t_len + 1, dtype=F32)[None, :, None]

    def cmul(xr, xi, yr, yi):
        return xr * yr - xi * yi, xr * yi + xi * yr

    swap = lambda v: jnp.swapaxes(v, 1, 2)
    w_cols, m_rows, lag_kernels, step_r, step_i = [], [], [], [], []
    for k in range(2):
        ar, ai = a_re[k].astype(F32), a_im[k].astype(F32)
        dt = jnp.exp(log_dt[k].astype(F32))[:, None]
        zr, zi = (ar * dt)[:, None, :], (ai * dt)[:, None, :]
        mag = jnp.exp(zr * n)
        pr, pi = mag * jnp.cos(zi * n), mag * jnp.sin(zi * n)
        lr, li = pr[:, 1] - 1.0, pi[:, 1]
        den = ar * ar + ai * ai
        fr, fi = (lr * ar + li * ai) / den, (li * ar - lr * ai) / den
        bbr, bbi = cmul(fr[:, None, :], fi[:, None, :],
                        swap(b_re[k].astype(F32)), swap(b_im[k].astype(F32)))
        crt, cit = swap(c_re[k].astype(F32)), swap(c_im[k].astype(F32))
        prt, pit = swap(pr), swap(pi)
        wp_r = pr[:, :t_len][:, ::-1] if k == 0 else pr[:, :t_len]
        wp_i = pi[:, :t_len][:, ::-1] if k == 0 else pi[:, :t_len]
        wr, wi = cmul(wp_r[:, :, None, :], wp_i[:, :, None, :], bbr[:, None, :, :], bbi[:, None, :, :])
        w_cols.append((wr.reshape(groups, S5_CHUNK_WIDTH, S5_STATE), wi.reshape(groups, S5_CHUNK_WIDTH, S5_STATE)))
        ep_r = prt[:, :, 1:] if k == 0 else prt[:, :, 1:][:, :, ::-1]
        ep_i = pit[:, :, 1:] if k == 0 else pit[:, :, 1:][:, :, ::-1]
        er, ei = cmul(crt[:, :, None, :], cit[:, :, None, :], ep_r[:, :, :, None], ep_i[:, :, :, None])
        m_rows.append((er.reshape(groups, S5_STATE, S5_CHUNK_WIDTH), (-ei).reshape(groups, S5_STATE, S5_CHUNK_WIDTH)))
        xr, xi = cmul(crt[:, :, None, :], cit[:, :, None, :], prt[:, :, :t_len, None], pit[:, :, :t_len, None])
        lhs = jnp.concatenate([bbr, -bbi], axis=-1)
        rhs = jnp.concatenate([xr, xi], axis=1).reshape(groups, 2 * S5_STATE, S5_CHUNK_WIDTH)
        lag_kernels.append(jnp.einsum("gck,gkn->gcn", lhs, rhs, precision=lax.Precision.HIGHEST))
        step_r.append(pr[:, t_len])
        step_i.append(pi[:, t_len])

    w_state = jnp.concatenate([w_cols[0][0], w_cols[1][0], w_cols[0][1], w_cols[1][1]], axis=-1)
    m_state = jnp.concatenate([m_rows[0][0], m_rows[1][0], m_rows[0][1], m_rows[1][1]], axis=1)
    kf, kb = lag_kernels
    kb_rev = kb[:, :, S5_GROUP:].reshape(groups, S5_GROUP, t_len - 1, S5_GROUP)[:, :, ::-1]
    lag0 = kf[:, :, :S5_GROUP] + kb[:, :, :S5_GROUP] + d.astype(F32)[:, None, :] * jnp.eye(S5_GROUP, dtype=F32)[None]
    lags = jnp.concatenate([kb_rev.reshape(groups, S5_GROUP, -1), lag0, kf[:, :, S5_GROUP:]], axis=-1)
    toe = jnp.stack([lags[:, :, S5_GROUP * (t_len - 1 - s):S5_GROUP * (2 * t_len - 1 - s)] for s in range(t_len)],
                    axis=1).reshape(groups, S5_CHUNK_WIDTH, S5_CHUNK_WIDTH)
    m_out = jnp.concatenate([toe, m_state], axis=1)
    return (w_state.astype(BF16), m_out.astype(BF16),
            jnp.concatenate(step_r, axis=-1), jnp.concatenate(step_i, axis=-1))


def _lane_group_ids():
    return lax.broadcasted_iota(jnp.int32, (RELAYOUT_ROWS, LANES), 1) // S5_GROUP


def _transpose_lane_blocks(pieces, grp):
    nblk = S5_LANE_GROUPS
    rolled = []
    for r in range(nblk):
        acc = None
        for b in range(nblk):
            acc = pieces[b] if acc is None else jnp.where(grp == (b - r) % nblk, pieces[b], acc)
        rolled.append(pltpu.roll(acc, S5_GROUP * r, 1) if r else acc)
    out = []
    for k in range(nblk):
        acc = None
        for b in range(nblk):
            z = rolled[(b - k) % nblk]
            acc = z if acc is None else jnp.where(grp == b, z, acc)
        out.append(acc)
    return out


def _s5_state_kernel(u_ref, w_ref, s_ref, ug_ref):
    nch = ug_ref.shape[1]
    grp = _lane_group_ids()

    def regroup(rb, carry):
        r0 = pl.multiple_of(rb * RELAYOUT_ROWS, RELAYOUT_ROWS)
        for half in range(S5_CHUNK_WIDTH // LANES):
            steps = [u_ref[pl.ds(r0 * S5_CHUNK + half * S5_LANE_GROUPS + tp, RELAYOUT_ROWS, stride=S5_CHUNK), :]
                     for tp in range(S5_LANE_GROUPS)]
            for g, v in enumerate(_transpose_lane_blocks(steps, grp)):
                ug_ref[g, pl.ds(r0, RELAYOUT_ROWS), half * LANES:(half + 1) * LANES] = v.astype(BF16)
        return carry

    lax.fori_loop(0, nch // RELAYOUT_ROWS, regroup, 0, unroll=2)
    for g in range(S5_LANE_GROUPS):
        s_ref[:, g * S5_STATE_WIDTH:(g + 1) * S5_STATE_WIDTH] = jnp.dot(
            ug_ref[g], w_ref[g], preferred_element_type=F32)


def _s5_chunk_states(u, w_state, nsplit):
    seq, width = u.shape
    g = width // S5_GROUP
    nc = seq // S5_CHUNK
    nch = nc // nsplit
    lg = S5_LANE_GROUPS
    return pl.pallas_call(
        _s5_state_kernel,
        grid=(width // LANES, nsplit),
        in_specs=[pl.BlockSpec((nch * S5_CHUNK, LANES), lambda j, h: (h, j)),
                  pl.BlockSpec((lg, S5_CHUNK_WIDTH, S5_STATE_WIDTH), lambda j, h: (j, 0, 0))],
        out_specs=[pl.BlockSpec((None, nch, lg * S5_STATE_WIDTH), lambda j, h: (j, h, 0)),
                   pl.BlockSpec((lg, nch, S5_CHUNK_WIDTH), lambda j, h: (j, h, 0))],
        out_shape=[jax.ShapeDtypeStruct((g // lg, nc, lg * S5_STATE_WIDTH), F32),
                   jax.ShapeDtypeStruct((g, nc, S5_CHUNK_WIDTH), BF16)],
        compiler_params=_params(("parallel", "parallel")),
        name="s5_chunk_states",
    )(u, w_state)


def _s5_scan_kernel(s_ref, ar_ref, ai_ref, h_ref, *, nc):
    half = S5_STATE_WIDTH // 2
    lane = lax.broadcasted_iota(jnp.int32, (SCAN_GROUPS, half), 1)
    fwd = lane < S5_STATE
    ar = ar_ref[...]
    ai = ai_ref[...]

    def advance(i, hr, hi):
        sf = s_ref[i]
        sb = s_ref[nc - 1 - i]
        sr = jnp.where(fwd, sf[:, :half], sb[:, :half])
        si = jnp.where(fwd, sf[:, half:], sb[:, half:])
        return ar * hr - ai * hi + sr, ar * hi + ai * hr + si

    def first_touch(i, carry):
        hr, hi = carry
        for row in (i, nc - 1 - i):
            h_ref[row, :, :half] = hr
            h_ref[row, :, half:] = hi
        return advance(i, hr, hi)

    def second_touch(i, carry):
        hr, hi = carry
        own = h_ref[i]
        other = h_ref[nc - 1 - i]
        h_ref[i, :, :half] = jnp.where(fwd, hr, own[:, :half])
        h_ref[i, :, half:] = jnp.where(fwd, hi, own[:, half:])
        h_ref[nc - 1 - i, :, :half] = jnp.where(fwd, other[:, :half], hr)
        h_ref[nc - 1 - i, :, half:] = jnp.where(fwd, other[:, half:], hi)
        return advance(i, hr, hi)

    zero = jnp.zeros((SCAN_GROUPS, half), F32)
    carry = lax.fori_loop(0, nc // 2, first_touch, (zero, zero))
    lax.fori_loop(nc // 2, nc, second_touch, carry)


def _s5_scan(s4, a_re, a_im):
    nt, nc, sg, sw = s4.shape
    return pl.pallas_call(
        functools.partial(_s5_scan_kernel, nc=nc),
        grid=(nt,),
        in_specs=[pl.BlockSpec((None, nc, sg, sw), lambda i: (i, 0, 0, 0)),
                  pl.BlockSpec((sg, sw // 2), lambda i: (i, 0)),
                  pl.BlockSpec((sg, sw // 2), lambda i: (i, 0))],
        out_specs=pl.BlockSpec((None, nc, sg, sw), lambda i: (i, 0, 0, 0)),
        out_shape=jax.ShapeDtypeStruct((nt, nc, sg, sw), F32),
        compiler_params=_params(("parallel",)),
        name="s5_scan",
    )(s4, a_re, a_im)


def _s5_out_kernel(ug_ref, h_ref, m_ref, o_ref, y_ref):
    cw = S5_CHUNK_WIDTH
    nch = ug_ref.shape[1]
    for g in range(S5_LANE_GROUPS):
        hg = h_ref[:, g * S5_STATE_WIDTH:(g + 1) * S5_STATE_WIDTH].astype(BF16)
        y = jnp.dot(ug_ref[g], m_ref[g, :cw, :], preferred_element_type=F32)
        y = y + jnp.dot(hg, m_ref[g, cw:, :], preferred_element_type=F32)
        y_ref[g] = jax.nn.gelu(y, approximate=True)
    grp = _lane_group_ids()

    def regroup(rb, carry):
        r0 = pl.multiple_of(rb * RELAYOUT_ROWS, RELAYOUT_ROWS)
        for half in range(cw // LANES):
            groups = [y_ref[g, pl.ds(r0, RELAYOUT_ROWS), half * LANES:(half + 1) * LANES]
                      for g in range(S5_LANE_GROUPS)]
            for tp, v in enumerate(_transpose_lane_blocks(groups, grp)):
                t = half * S5_LANE_GROUPS + tp
                o_ref[pl.ds(r0 * S5_CHUNK + t, RELAYOUT_ROWS, stride=S5_CHUNK), :] = v
        return carry

    lax.fori_loop(0, nch // RELAYOUT_ROWS, regroup, 0, unroll=2)


def _s5_outputs(ug, h3, m_out, nsplit):
    g, nc, cw = ug.shape
    nch = nc // nsplit
    lg = S5_LANE_GROUPS
    return pl.pallas_call(
        _s5_out_kernel,
        grid=(g // lg, nsplit),
        in_specs=[pl.BlockSpec((lg, nch, cw), lambda j, h: (j, h, 0)),
                  pl.BlockSpec((None, nch, lg * S5_STATE_WIDTH), lambda j, h: (j, h, 0)),
                  pl.BlockSpec((lg, cw + S5_STATE_WIDTH, cw), lambda j, h: (j, 0, 0))],
        out_specs=pl.BlockSpec((nch * S5_CHUNK, LANES), lambda j, h: (h, j)),
        out_shape=jax.ShapeDtypeStruct((nc * S5_CHUNK, g * S5_GROUP), F32),
        scratch_shapes=[pltpu.VMEM((lg, nch, cw), F32)],
        compiler_params=_params(("parallel", "parallel")),
        name="s5_outputs",
    )(ug, h3, m_out)


def _glu_kernel(a_ref, w_ref, b_ref, y_ref, o_ref, ab_ref):
    @pl.when(pl.program_id(1) == 0)
    def _():
        ab_ref[...] = a_ref[...].astype(BF16)

    t = jnp.dot(ab_ref[...], w_ref[...], preferred_element_type=F32) + b_ref[...]
    o_ref[...] = (y_ref[...] * jax.nn.sigmoid(t)).astype(o_ref.dtype)


def _glu(ys, w, b, tm, tn):
    m, k = ys.shape
    n = w.shape[1]
    return pl.pallas_call(
        _glu_kernel,
        grid=(m // tm, n // tn),
        in_specs=[pl.BlockSpec((tm, k), lambda i, j: (i, 0)),
                  pl.BlockSpec((k, tn), lambda i, j: (0, j)),
                  pl.BlockSpec((1, tn), lambda i, j: (0, j)),
                  pl.BlockSpec((tm, tn), lambda i, j: (i, j))],
        out_specs=pl.BlockSpec((tm, tn), lambda i, j: (i, j)),
        out_shape=jax.ShapeDtypeStruct((m, n), BF16),
        scratch_shapes=[pltpu.VMEM((tm, k), BF16)],
        compiler_params=_params(("parallel", "arbitrary")),
        name="s5_glu",
    )(ys, w, b, ys)


def _merge_kernel(a1_ref, w1_ref, a2_ref, w2_ref, g1_ref, g2_ref, o_ref):
    y1 = jnp.dot(a1_ref[...], w1_ref[...], preferred_element_type=F32)
    y2 = jnp.dot(a2_ref[...], w2_ref[...], preferred_element_type=F32)
    o_ref[...] = (g1_ref[...].astype(F32) * y1 + g2_ref[...].astype(F32) * y2).astype(o_ref.dtype)


def _merge(att, w_na, z, w_s5, proj, gate_col0, tm, tn):
    m, k1 = att.shape
    k2 = z.shape[1]
    n = w_na.shape[1]
    g1 = gate_col0 // tn
    g2 = (gate_col0 + n) // tn
    return pl.pallas_call(
        _merge_kernel,
        grid=(m // tm, n // tn),
        in_specs=[pl.BlockSpec((tm, k1), lambda i, j: (i, 0)),
                  pl.BlockSpec((k1, tn), lambda i, j: (0, j)),
                  pl.BlockSpec((tm, k2), lambda i, j: (i, 0)),
                  pl.BlockSpec((k2, tn), lambda i, j: (0, j)),
                  pl.BlockSpec((tm, tn), lambda i, j: (i, g1 + j)),
                  pl.BlockSpec((tm, tn), lambda i, j: (i, g2 + j))],
        out_specs=pl.BlockSpec((tm, tn), lambda i, j: (i, j)),
        out_shape=jax.ShapeDtypeStruct((m, n), BF16),
        compiler_params=_params(("parallel", "arbitrary")),
        name="gated_merge",
    )(att, w_na, z, w_s5, proj, proj)


def _outproj_ln_kernel(a_ref, w_ref, x_ref, g_ref, b_ref, h_ref, *, alpha):
    mix = jnp.dot(a_ref[...], w_ref[...], preferred_element_type=F32)
    h_ref[...] = _layer_norm_rows(alpha * x_ref[...] + mix, g_ref[...], b_ref[...])


def _outproj_ln(a, w, x, g, b, alpha, tm):
    m, k = a.shape
    n = w.shape[1]
    return pl.pallas_call(
        functools.partial(_outproj_ln_kernel, alpha=alpha),
        grid=(m // tm,),
        in_specs=[pl.BlockSpec((tm, k), lambda i: (i, 0)),
                  pl.BlockSpec((k, n), lambda i: (0, 0)),
                  pl.BlockSpec((tm, n), lambda i: (i, 0)),
                  pl.BlockSpec((1, n), lambda i: (0, 0)),
                  pl.BlockSpec((1, n), lambda i: (0, 0))],
        out_specs=pl.BlockSpec((tm, n), lambda i: (i, 0)),
        out_shape=jax.ShapeDtypeStruct((m, n), F32),
        compiler_params=_params(("parallel",)),
        name="outproj_ln1",
    )(a, w, x, g, b)


def _ffn_act_kernel(h_ref, wg_ref, wu_ref, o_ref, hb_ref):
    @pl.when(pl.program_id(1) == 0)
    def _():
        hb_ref[...] = h_ref[...].astype(BF16)

    hb = hb_ref[...]
    gate = jnp.dot(hb, wg_ref[...], preferred_element_type=F32)
    up = jnp.dot(hb, wu_ref[...], preferred_element_type=F32)
    o_ref[...] = (jax.nn.silu(gate) * up).astype(o_ref.dtype)


def _ffn_act(h, wg, wu, tm, tf):
    m, d = h.shape
    f = wg.shape[1]
    return pl.pallas_call(
        _ffn_act_kernel,
        grid=(m // tm, f // tf),
        in_specs=[pl.BlockSpec((tm, d), lambda i, j: (i, 0)),
                  pl.BlockSpec((d, tf), lambda i, j: (0, j)),
                  pl.BlockSpec((d, tf), lambda i, j: (0, j))],
        out_specs=pl.BlockSpec((tm, tf), lambda i, j: (i, j)),
        out_shape=jax.ShapeDtypeStruct((m, f), BF16),
        scratch_shapes=[pltpu.VMEM((tm, d), BF16)],
        compiler_params=_params(("parallel", "arbitrary")),
        name="swiglu_act",
    )(h, wg, wu)


def _ffn_down_ln_kernel(a_ref, w_ref, h_ref, g_ref, b_ref, o_ref, *, alpha):
    ff = jnp.dot(a_ref[...], w_ref[...], preferred_element_type=F32)
    o_ref[...] = _layer_norm_rows(alpha * h_ref[...] + ff, g_ref[...], b_ref[...])


def _ffn_down_ln(act, wd, h, g, b, alpha, tm):
    m, f = act.shape
    d = wd.shape[1]
    return pl.pallas_call(
        functools.partial(_ffn_down_ln_kernel, alpha=alpha),
        grid=(m // tm,),
        in_specs=[pl.BlockSpec((tm, f), lambda i: (i, 0)),
                  pl.BlockSpec((f, d), lambda i: (0, 0), pipeline_mode=pl.Buffered(1)),
                  pl.BlockSpec((tm, d), lambda i: (i, 0)),
                  pl.BlockSpec((1, d), lambda i: (0, 0)),
                  pl.BlockSpec((1, d), lambda i: (0, 0))],
        out_specs=pl.BlockSpec((tm, d), lambda i: (i, 0)),
        out_shape=jax.ShapeDtypeStruct((m, d), F32),
        compiler_params=_params(("parallel",)),
        name="swiglu_down_ln2",
    )(act, wd, h, g, b)


def _layer(x2, w_in, b_gate, na_rpb, w_na_out, s5_a_re, s5_a_im, s5_log_dt, s5_b_re, s5_b_im,
           s5_c_re, s5_c_im, s5_d, w_glu, b_glu, w_s5_out, w_out, ln1_g, ln1_b,
           w_ffn_gate, w_ffn_up, w_ffn_down, ln2_g, ln2_b, alpha):
    seq, d_model = x2.shape
    rows = seq // GRID_W
    nc = seq // S5_CHUNK
    gate_col0 = 3 * NA_WIDTH + S5_WIDTH
    in_cols = gate_col0 + 2 * d_model
    row = lambda v: v.astype(F32).reshape(1, -1)

    col_scale = jnp.where(jnp.arange(in_cols) < NA_WIDTH, NA_HEAD_DIM ** -0.5, 1.0).astype(F32).reshape(1, -1)
    col_bias = jnp.concatenate([jnp.zeros((gate_col0,), F32), b_gate.astype(F32)]).reshape(1, -1)
    proj, u = _inproj(x2, w_in.astype(BF16), col_scale, col_bias, 3 * NA_WIDTH, gate_col0, tm=1024, tn=512)

    att = _attention(proj, _attention_bias(na_rpb, rows), seq)

    w_state, m_out, a_re, a_im = _s5_tables(s5_a_re, s5_a_im, s5_log_dt, s5_b_re, s5_b_im,
                                            s5_c_re, s5_c_im, s5_d)
    s, ug = _s5_chunk_states(u, w_state, nsplit=2)
    n_tiles = S5_WIDTH // LANES
    h_states = _s5_scan(s.reshape(n_tiles, nc, SCAN_GROUPS, S5_STATE_WIDTH), a_re, a_im)
    ys = _s5_outputs(ug, h_states.reshape(n_tiles, nc, SCAN_GROUPS * S5_STATE_WIDTH), m_out, nsplit=2)

    z = _glu(ys, w_glu.astype(BF16), row(b_glu), tm=1024, tn=512)
    merged = _merge(att, w_na_out.astype(BF16), z, w_s5_out.astype(BF16), proj, gate_col0, tm=1024, tn=512)
    h = _outproj_ln(merged, w_out.astype(BF16), x2, row(ln1_g), row(ln1_b), alpha, tm=512)
    act = _ffn_act(h, w_ffn_gate.astype(BF16), w_ffn_up.astype(BF16), tm=1024, tf=512)
    return _ffn_down_ln(act, w_ffn_down.astype(BF16), h, row(ln2_g), row(ln2_b), alpha, tm=512)


def kernel(x, w_in, b_gate, na_rpb, w_na_out, s5_a_re, s5_a_im, s5_log_dt, s5_b_re, s5_b_im, s5_c_re, s5_c_im, s5_d, w_glu, b_glu, w_s5_out, w_out, ln1_g, ln1_b, w_ffn_gate, w_ffn_up, w_ffn_down, ln2_g, ln2_b):
    bsz, seq, d_model = x.shape
    depth = w_in.shape[0]
    alpha = (2.0 * depth) ** 0.25
    outs = []
    for bi in range(bsz):
        xb = x[bi]
        for l in range(depth):
            xb = _layer(xb, w_in[l], b_gate[l], na_rpb[l], w_na_out[l], s5_a_re[l], s5_a_im[l], s5_log_dt[l],
                        s5_b_re[l], s5_b_im[l], s5_c_re[l], s5_c_im[l], s5_d[l], w_glu[l], b_glu[l],
                        w_s5_out[l], w_out[l], ln1_g[l], ln1_b[l], w_ffn_gate[l], w_ffn_up[l],
                        w_ffn_down[l], ln2_g[l], ln2_b[l], alpha)
        outs.append(xb)
    return jnp.stack(outs)
```

```python
import functools
import math

import jax
import jax.numpy as jnp
from jax import lax
from jax.experimental import pallas as pl
from jax.experimental.pallas import tpu as pltpu

F32 = jnp.float32
BF16 = jnp.bfloat16

GRID_W = 64
NA_HEADS = 8
NA_HEAD_DIM = 128
NA_WIDTH = NA_HEADS * NA_HEAD_DIM
NA_ROWS = 8
NA_COLS = 16
S5_GROUP = 16
S5_GROUPS = 64
S5_WIDTH = S5_GROUP * S5_GROUPS
S5_STATE = 64
LN_EPS = 1e-5
MASK_VALUE = -1e30

S5_CHUNK = 16
S5_CHUNK_WIDTH = S5_CHUNK * S5_GROUP
S5_STATE_WIDTH = 4 * S5_STATE
SCAN_GROUPS = 8

ATT_Q_ROWS = 8
ATT_K_ROWS = 16
ATT_Q = ATT_Q_ROWS * GRID_W
ATT_K = ATT_K_ROWS * GRID_W
ATT_KBLK = 256
LANES = 128
ATT_WIN = NA_ROWS * GRID_W + LANES
ATT_HEADS_PER_STEP = 2
S5_LANE_GROUPS = LANES // S5_GROUP
RELAYOUT_ROWS = 16

VMEM_LIMIT = 56 * 1024 * 1024


def _params(sem):
    return pltpu.CompilerParams(dimension_semantics=sem, vmem_limit_bytes=VMEM_LIMIT)


def _layer_norm_rows(y, g, b):
    mu = jnp.mean(y, axis=-1, keepdims=True)
    d = y - mu
    var = jnp.mean(d * d, axis=-1, keepdims=True)
    return d * lax.rsqrt(var + LN_EPS) * g + b


def _inproj_kernel(x_ref, w_ref, s_ref, b_ref, o_ref, u_ref, xb_ref, *, u_tile0, gate_tile0):
    j = pl.program_id(1)

    @pl.when(j == 0)
    def _():
        xb_ref[...] = x_ref[...].astype(BF16)

    @pl.when(j < u_tile0)
    def _():
        acc = jnp.dot(xb_ref[...], w_ref[...], preferred_element_type=F32)
        o_ref[...] = (acc * s_ref[...]).astype(o_ref.dtype)

    @pl.when((j >= u_tile0) & (j < gate_tile0))
    def _():
        acc = jnp.dot(xb_ref[...], w_ref[...], preferred_element_type=F32)
        o_ref[...] = acc.astype(o_ref.dtype)
        u_ref[...] = acc

    @pl.when(j >= gate_tile0)
    def _():
        acc = jnp.dot(xb_ref[...], w_ref[...], preferred_element_type=F32)
        o_ref[...] = jax.nn.sigmoid(acc + b_ref[...]).astype(o_ref.dtype)


def _inproj(x, w, col_scale, col_bias, u_col0, gate_col0, tm, tn):
    m, k = x.shape
    n = w.shape[1]
    u_tile0, gate_tile0 = u_col0 // tn, gate_col0 // tn
    return pl.pallas_call(
        functools.partial(_inproj_kernel, u_tile0=u_tile0, gate_tile0=gate_tile0),
        grid=(m // tm, n // tn),
        in_specs=[
            pl.BlockSpec((tm, k), lambda i, j: (i, 0)),
            pl.BlockSpec((k, tn), lambda i, j: (0, j)),
            pl.BlockSpec((1, tn), lambda i, j: (0, j)),
            pl.BlockSpec((1, tn), lambda i, j: (0, j)),
        ],
        out_specs=[pl.BlockSpec((tm, tn), lambda i, j: (i, j)),
                   pl.BlockSpec((tm, tn), lambda i, j: (i, jnp.clip(j - u_tile0, 0, gate_tile0 - u_tile0 - 1)))],
        out_shape=[jax.ShapeDtypeStruct((m, n), BF16),
                   jax.ShapeDtypeStruct((m, gate_col0 - u_col0), F32)],
        scratch_shapes=[pltpu.VMEM((tm, k), BF16)],
        compiler_params=_params(("parallel", "arbitrary")),
        name="inproj",
    )(x, w, col_scale, col_bias)


def _attn_windows(rows):
    nb = rows // ATT_Q_ROWS
    kinds = []
    for b in (0, 1, nb - 1):
        base = min(max(b * ATT_Q_ROWS - NA_ROWS // 2, 0), rows - ATT_K_ROWS)
        geo = []
        for ql in range(ATT_Q_ROWS):
            qr = b * ATT_Q_ROWS + ql
            r0 = min(max(qr - NA_ROWS // 2, 0), rows - NA_ROWS)
            koff = r0 - base
            lane0 = min(LANES * (koff * GRID_W // LANES), ATT_K - ATT_WIN)
            geo.append((lane0, koff * GRID_W - lane0, r0 - qr + NA_ROWS - 1))
        kinds.append(tuple(geo))
    return tuple(kinds)


def _attn_kernel(q_ref, k0, k1, k2, k3, v0, v1, v2, v3, bias_ref, o_ref, p_ref, *, windows, nb):
    b = pl.program_id(1)
    k_refs, v_refs = (k0, k1, k2, k3), (v0, v1, v2, v3)

    def one_head(hh, geo):
        c = slice(hh * NA_HEAD_DIM, (hh + 1) * NA_HEAD_DIM)
        k = jnp.concatenate([r[:, c] for r in k_refs], axis=0)
        v = jnp.concatenate([r[:, c] for r in v_refs], axis=0)
        s = lax.dot_general(q_ref[:, c], k, (((1,), (1,)), ((), ())), preferred_element_type=F32)
        inv = []
        for ql, (lane0, _, _) in enumerate(geo):
            r = slice(ql * GRID_W, (ql + 1) * GRID_W)
            sw = s[r, lane0:lane0 + ATT_WIN] + bias_ref[hh, ql]
            m = jnp.max(sw, axis=-1, keepdims=True)
            p = jnp.exp(sw - m)
            inv.append(1.0 / jnp.sum(p, axis=-1, keepdims=True))
            if lane0 > 0:
                p_ref[hh, r, :lane0] = jnp.zeros((GRID_W, lane0), BF16)
            p_ref[hh, r, lane0:lane0 + ATT_WIN] = p.astype(BF16)
            if lane0 + ATT_WIN < ATT_K:
                p_ref[hh, r, lane0 + ATT_WIN:] = jnp.zeros((GRID_W, ATT_K - lane0 - ATT_WIN), BF16)
        o = jnp.dot(p_ref[hh], v, preferred_element_type=F32)
        o_ref[:, c] = (o * jnp.concatenate(inv, axis=0)).astype(o_ref.dtype)

    def block_kind(geo):
        for hh in range(ATT_HEADS_PER_STEP):
            one_head(hh, geo)

    @pl.when(b == 0)
    def _():
        block_kind(windows[0])

    @pl.when((b > 0) & (b < nb - 1))
    def _():
        block_kind(windows[1])

    @pl.when(b == nb - 1)
    def _():
        block_kind(windows[2])


def _attention(proj, bias, seq):
    nb = seq // ATT_Q
    n_kblk = seq // ATT_KBLK
    per_blk = ATT_K // ATT_KBLK

    def kstart(b):
        return jnp.clip(2 * b - 1, 0, n_kblk - per_blk)

    def kind(b):
        return jnp.where(b == 0, 0, jnp.where(b == nb - 1, 2, 1))

    hps = ATT_HEADS_PER_STEP
    width = hps * NA_HEAD_DIM
    n_groups = NA_HEADS // hps

    def kv_spec(col0, t):
        return pl.BlockSpec((ATT_KBLK, width), lambda h, b: (kstart(b) + t, col0 + h))

    in_specs = [pl.BlockSpec((ATT_Q, width), lambda h, b: (b, h))]
    in_specs += [kv_spec(n_groups, t) for t in range(per_blk)]
    in_specs += [kv_spec(2 * n_groups, t) for t in range(per_blk)]
    in_specs += [pl.BlockSpec((None, hps, ATT_Q_ROWS, GRID_W, ATT_WIN), lambda h, b: (kind(b), h, 0, 0, 0))]
    return pl.pallas_call(
        functools.partial(_attn_kernel, windows=_attn_windows(seq // GRID_W), nb=nb),
        grid=(n_groups, nb),
        in_specs=in_specs,
        out_specs=pl.BlockSpec((ATT_Q, width), lambda h, b: (b, h)),
        out_shape=jax.ShapeDtypeStruct((seq, NA_WIDTH), BF16),
        scratch_shapes=[pltpu.VMEM((hps, ATT_Q, ATT_K), BF16)],
        compiler_params=_params(("parallel", "arbitrary")),
        name="na_attention",
    )(*([proj] * (1 + 2 * per_blk)), bias)


def _attention_bias(rpb, rows):
    cols = jnp.arange(GRID_W)
    col_start = jnp.clip(cols - NA_COLS // 2, 0, GRID_W - NA_COLS)
    col_ok = (cols[None, :] >= col_start[:, None]) & (cols[None, :] < col_start[:, None] + NA_COLS)
    col_idx = jnp.clip(cols[None, :] - cols[:, None], -(NA_COLS - 1), NA_COLS - 1) + (NA_COLS - 1)
    by_col = jnp.take(rpb.astype(F32), col_idx.reshape(-1), axis=2)
    by_col = by_col.reshape(NA_HEADS, 2 * NA_ROWS - 1, GRID_W, GRID_W)
    by_col = jnp.where(col_ok[None, None], by_col, MASK_VALUE).transpose(0, 2, 1, 3)
    kinds = []
    for geo in _attn_windows(rows):
        strips = []
        for _, phase, rfirst in geo:
            strip = by_col[:, :, rfirst:rfirst + NA_ROWS, :].reshape(NA_HEADS, GRID_W, NA_ROWS * GRID_W)
            strips.append(jnp.pad(strip, ((0, 0), (0, 0), (phase, ATT_WIN - NA_ROWS * GRID_W - phase)),
                                  constant_values=MASK_VALUE))
        kinds.append(jnp.stack(strips, axis=1))
    return jnp.stack(kinds)


def _s5_tables(a_re, a_im, log_dt, b_re, b_im, c_re, c_im, d):
    t_len = S5_CHUNK
    groups = a_re.shape[1]
    n = jnp.arange(t_len + 1, dtype=F32)[None, :, None]

    def cmul(xr, xi, yr, yi):
        return xr * yr - xi * yi, xr * yi + xi * yr

    swap = lambda v: jnp.swapaxes(v, 1, 2)
    w_cols, m_rows, lag_kernels, step_r, step_i = [], [], [], [], []
    for k in range(2):
        ar, ai = a_re[k].astype(F32), a_im[k].astype(F32)
        dt = jnp.exp(log_dt[k].astype(F32))[:, None]
        zr, zi = (ar * dt)[:, None, :], (ai * dt)[:, None, :]
        mag = jnp.exp(zr * n)
        pr, pi = mag * jnp.cos(zi * n), mag * jnp.sin(zi * n)
        lr, li = pr[:, 1] - 1.0, pi[:, 1]
        den = ar * ar + ai * ai
        fr, fi = (lr * ar + li * ai) / den, (li * ar - lr * ai) / den
        bbr, bbi = cmul(fr[:, None, :], fi[:, None, :],
                        swap(b_re[k].astype(F32)), swap(b_im[k].astype(F32)))
        crt, cit = swap(c_re[k].astype(F32)), swap(c_im[k].astype(F32))
        prt, pit = swap(pr), swap(pi)
        wp_r = pr[:, :t_len][:, ::-1] if k == 0 else pr[:, :t_len]
        wp_i = pi[:, :t_len][:, ::-1] if k == 0 else pi[:, :t_len]
        wr, wi = cmul(wp_r[:, :, None, :], wp_i[:, :, None, :], bbr[:, None, :, :], bbi[:, None, :, :])
        w_cols.append((wr.reshape(groups, S5_CHUNK_WIDTH, S5_STATE), wi.reshape(groups, S5_CHUNK_WIDTH, S5_STATE)))
        ep_r = prt[:, :, 1:] if k == 0 else prt[:, :, 1:][:, :, ::-1]
        ep_i = pit[:, :, 1:] if k == 0 else pit[:, :, 1:][:, :, ::-1]
        er, ei = cmul(crt[:, :, None, :], cit[:, :, None, :], ep_r[:, :, :, None], ep_i[:, :, :, None])
        m_rows.append((er.reshape(groups, S5_STATE, S5_CHUNK_WIDTH), (-ei).reshape(groups, S5_STATE, S5_CHUNK_WIDTH)))
        xr, xi = cmul(crt[:, :, None, :], cit[:, :, None, :], prt[:, :, :t_len, None], pit[:, :, :t_len, None])
        lhs = jnp.concatenate([bbr, -bbi], axis=-1)
        rhs = jnp.concatenate([xr, xi], axis=1).reshape(groups, 2 * S5_STATE, S5_CHUNK_WIDTH)
        lag_kernels.append(jnp.einsum("gck,gkn->gcn", lhs, rhs, precision=lax.Precision.HIGHEST))
        step_r.append(pr[:, t_len])
        step_i.append(pi[:, t_len])

    w_state = jnp.concatenate([w_cols[0][0], w_cols[1][0], w_cols[0][1], w_cols[1][1]], axis=-1)
    m_state = jnp.concatenate([m_rows[0][0], m_rows[1][0], m_rows[0][1], m_rows[1][1]], axis=1)
    kf, kb = lag_kernels
    kb_rev = kb[:, :, S5_GROUP:].reshape(groups, S5_GROUP, t_len - 1, S5_GROUP)[:, :, ::-1]
    lag0 = kf[:, :, :S5_GROUP] + kb[:, :, :S5_GROUP] + d.astype(F32)[:, None, :] * jnp.eye(S5_GROUP, dtype=F32)[None]
    lags = jnp.concatenate([kb_rev.reshape(groups, S5_GROUP, -1), lag0, kf[:, :, S5_GROUP:]], axis=-1)
    toe = jnp.stack([lags[:, :, S5_GROUP * (t_len - 1 - s):S5_GROUP * (2 * t_len - 1 - s)] for s in range(t_len)],
                    axis=1).reshape(groups, S5_CHUNK_WIDTH, S5_CHUNK_WIDTH)
    m_out = jnp.concatenate([toe, m_state], axis=1)
    return (w_state.astype(BF16), m_out.astype(BF16),
            jnp.concatenate(step_r, axis=-1), jnp.concatenate(step_i, axis=-1))


def _lane_group_ids():
    return lax.broadcasted_iota(jnp.int32, (RELAYOUT_ROWS, LANES), 1) // S5_GROUP


def _transpose_lane_blocks(pieces, grp):
    nblk = S5_LANE_GROUPS
    rolled = []
    for r in range(nblk):
        acc = None
        for b in range(nblk):
            acc = pieces[b] if acc is None else jnp.where(grp == (b - r) % nblk, pieces[b], acc)
        rolled.append(pltpu.roll(acc, S5_GROUP * r, 1) if r else acc)
    out = []
    for k in range(nblk):
        acc = None
        for b in range(nblk):
            z = rolled[(b - k) % nblk]
            acc = z if acc is None else jnp.where(grp == b, z, acc)
        out.append(acc)
    return out


def _s5_state_kernel(u_ref, w_ref, s_ref, ug_ref):
    nch = ug_ref.shape[1]
    grp = _lane_group_ids()

    def regroup(rb, carry):
        r0 = pl.multiple_of(rb * RELAYOUT_ROWS, RELAYOUT_ROWS)
        for half in range(S5_CHUNK_WIDTH // LANES):
            steps = [u_ref[pl.ds(r0 * S5_CHUNK + half * S5_LANE_GROUPS + tp, RELAYOUT_ROWS, stride=S5_CHUNK), :]
                     for tp in range(S5_LANE_GROUPS)]
            for g, v in enumerate(_transpose_lane_blocks(steps, grp)):
                ug_ref[g, pl.ds(r0, RELAYOUT_ROWS), half * LANES:(half + 1) * LANES] = v.astype(BF16)
        return carry

    lax.fori_loop(0, nch // RELAYOUT_ROWS, regroup, 0, unroll=2)
    for g in range(S5_LANE_GROUPS):
        s_ref[:, g * S5_STATE_WIDTH:(g + 1) * S5_STATE_WIDTH] = jnp.dot(
            ug_ref[g], w_ref[g], preferred_element_type=F32)


def _s5_chunk_states(u, w_state, nsplit):
    seq, width = u.shape
    g = width // S5_GROUP
    nc = seq // S5_CHUNK
    nch = nc // nsplit
    lg = S5_LANE_GROUPS
    return pl.pallas_call(
        _s5_state_kernel,
        grid=(width // LANES, nsplit),
        in_specs=[pl.BlockSpec((nch * S5_CHUNK, LANES), lambda j, h: (h, j)),
                  pl.BlockSpec((lg, S5_CHUNK_WIDTH, S5_STATE_WIDTH), lambda j, h: (j, 0, 0))],
        out_specs=[pl.BlockSpec((None, nch, lg * S5_STATE_WIDTH), lambda j, h: (j, h, 0)),
                   pl.BlockSpec((lg, nch, S5_CHUNK_WIDTH), lambda j, h: (j, h, 0))],
        out_shape=[jax.ShapeDtypeStruct((g // lg, nc, lg * S5_STATE_WIDTH), F32),
                   jax.ShapeDtypeStruct((g, nc, S5_CHUNK_WIDTH), BF16)],
        compiler_params=_params(("parallel", "parallel")),
        name="s5_chunk_states",
    )(u, w_state)


def _s5_scan_kernel(s_ref, ar_ref, ai_ref, h_ref, *, nc):
    half = S5_STATE_WIDTH // 2
    lane = lax.broadcasted_iota(jnp.int32, (SCAN_GROUPS, half), 1)
    fwd = lane < S5_STATE
    ar = ar_ref[...]
    ai = ai_ref[...]

    def advance(i, hr, hi):
        sf = s_ref[i]
        sb = s_ref[nc - 1 - i]
        sr = jnp.where(fwd, sf[:, :half], sb[:, :half])
        si = jnp.where(fwd, sf[:, half:], sb[:, half:])
        return ar * hr - ai * hi + sr, ar * hi + ai * hr + si

    def first_touch(i, carry):
        hr, hi = carry
        for row in (i, nc - 1 - i):
            h_ref[row, :, :half] = hr
            h_ref[row, :, half:] = hi
        return advance(i, hr, hi)

    def second_touch(i, carry):
        hr, hi = carry
        own = h_ref[i]
        other = h_ref[nc - 1 - i]
        h_ref[i, :, :half] = jnp.where(fwd, hr, own[:, :half])
        h_ref[i, :, half:] = jnp.where(fwd, hi, own[:, half:])
        h_ref[nc - 1 - i, :, :half] = jnp.where(fwd, other[:, :half], hr)
        h_ref[nc - 1 - i, :, half:] = jnp.where(fwd, other[:, half:], hi)
        return advance(i, hr, hi)

    zero = jnp.zeros((SCAN_GROUPS, half), F32)
    carry = lax.fori_loop(0, nc // 2, first_touch, (zero, zero))
    lax.fori_loop(nc // 2, nc, second_touch, carry)


def _s5_scan(s4, a_re, a_im):
    nt, nc, sg, sw = s4.shape
    return pl.pallas_call(
        functools.partial(_s5_scan_kernel, nc=nc),
        grid=(nt,),
        in_specs=[pl.BlockSpec((None, nc, sg, sw), lambda i: (i, 0, 0, 0)),
                  pl.BlockSpec((sg, sw // 2), lambda i: (i, 0)),
                  pl.BlockSpec((sg, sw // 2), lambda i: (i, 0))],
        out_specs=pl.BlockSpec((None, nc, sg, sw), lambda i: (i, 0, 0, 0)),
        out_shape=jax.ShapeDtypeStruct((nt, nc, sg, sw), F32),
        compiler_params=_params(("parallel",)),
        name="s5_scan",
    )(s4, a_re, a_im)


def _s5_out_kernel(ug_ref, h_ref, m_ref, o_ref, y_ref):
    cw = S5_CHUNK_WIDTH
    nch = ug_ref.shape[1]
    for g in range(S5_LANE_GROUPS):
        hg = h_ref[:, g * S5_STATE_WIDTH:(g + 1) * S5_STATE_WIDTH].astype(BF16)
        y = jnp.dot(ug_ref[g], m_ref[g, :cw, :], preferred_element_type=F32)
        y = y + jnp.dot(hg, m_ref[g, cw:, :], preferred_element_type=F32)
        y_ref[g] = jax.nn.gelu(y, approximate=True)
    grp = _lane_group_ids()

    def regroup(rb, carry):
        r0 = pl.multiple_of(rb * RELAYOUT_ROWS, RELAYOUT_ROWS)
        for half in range(cw // LANES):
            groups = [y_ref[g, pl.ds(r0, RELAYOUT_ROWS), half * LANES:(half + 1) * LANES]
                      for g in range(S5_LANE_GROUPS)]
            for tp, v in enumerate(_transpose_lane_blocks(groups, grp)):
                t = half * S5_LANE_GROUPS + tp
                o_ref[pl.ds(r0 * S5_CHUNK + t, RELAYOUT_ROWS, stride=S5_CHUNK), :] = v
        return carry

    lax.fori_loop(0, nch // RELAYOUT_ROWS, regroup, 0, unroll=2)


def _s5_outputs(ug, h3, m_out, nsplit):
    g, nc, cw = ug.shape
    nch = nc // nsplit
    lg = S5_LANE_GROUPS
    return pl.pallas_call(
        _s5_out_kernel,
        grid=(g // lg, nsplit),
        in_specs=[pl.BlockSpec((lg, nch, cw), lambda j, h: (j, h, 0)),
                  pl.BlockSpec((None, nch, lg * S5_STATE_WIDTH), lambda j, h: (j, h, 0)),
                  pl.BlockSpec((lg, cw + S5_STATE_WIDTH, cw), lambda j, h: (j, 0, 0))],
        out_specs=pl.BlockSpec((nch * S5_CHUNK, LANES), lambda j, h: (h, j)),
        out_shape=jax.ShapeDtypeStruct((nc * S5_CHUNK, g * S5_GROUP), F32),
        scratch_shapes=[pltpu.VMEM((lg, nch, cw), F32)],
        compiler_params=_params(("parallel", "parallel")),
        name="s5_outputs",
    )(ug, h3, m_out)


def _glu_kernel(a_ref, w_ref, b_ref, y_ref, o_ref, ab_ref):
    @pl.when(pl.program_id(1) == 0)
    def _():
        ab_ref[...] = a_ref[...].astype(BF16)

    t = jnp.dot(ab_ref[...], w_ref[...], preferred_element_type=F32) + b_ref[...]
    o_ref[...] = (y_ref[...] * jax.nn.sigmoid(t)).astype(o_ref.dtype)


def _glu(ys, w, b, tm, tn):
    m, k = ys.shape
    n = w.shape[1]
    return pl.pallas_call(
        _glu_kernel,
        grid=(m // tm, n // tn),
        in_specs=[pl.BlockSpec((tm, k), lambda i, j: (i, 0)),
                  pl.BlockSpec((k, tn), lambda i, j: (0, j)),
                  pl.BlockSpec((1, tn), lambda i, j: (0, j)),
                  pl.BlockSpec((tm, tn), lambda i, j: (i, j))],
        out_specs=pl.BlockSpec((tm, tn), lambda i, j: (i, j)),
        out_shape=jax.ShapeDtypeStruct((m, n), BF16),
        scratch_shapes=[pltpu.VMEM((tm, k), BF16)],
        compiler_params=_params(("parallel", "arbitrary")),
        name="s5_glu",
    )(ys, w, b, ys)


def _merge_kernel(a1_ref, w1_ref, a2_ref, w2_ref, g1_ref, g2_ref, o_ref):
    y1 = jnp.dot(a1_ref[...], w1_ref[...], preferred_element_type=F32)
    y2 = jnp.dot(a2_ref[...], w2_ref[...], preferred_element_type=F32)
    o_ref[...] = (g1_ref[...].astype(F32) * y1 + g2_ref[...].astype(F32) * y2).astype(o_ref.dtype)


def _merge(att, w_na, z, w_s5, proj, gate_col0, tm, tn):
    m, k1 = att.shape
    k2 = z.shape[1]
    n = w_na.shape[1]
    g1 = gate_col0 // tn
    g2 = (gate_col0 + n) // tn
    return pl.pallas_call(
        _merge_kernel,
        grid=(m // tm, n // tn),
        in_specs=[pl.BlockSpec((tm, k1), lambda i, j: (i, 0)),
                  pl.BlockSpec((k1, tn), lambda i, j: (0, j)),
                  pl.BlockSpec((tm, k2), lambda i, j: (i, 0)),
                  pl.BlockSpec((k2, tn), lambda i, j: (0, j)),
                  pl.BlockSpec((tm, tn), lambda i, j: (i, g1 + j)),
                  pl.BlockSpec((tm, tn), lambda i, j: (i, g2 + j))],
        out_specs=pl.BlockSpec((tm, tn), lambda i, j: (i, j)),
        out_shape=jax.ShapeDtypeStruct((m, n), BF16),
        compiler_params=_params(("parallel", "arbitrary")),
        name="gated_merge",
    )(att, w_na, z, w_s5, proj, proj)


def _outproj_ln_kernel(a_ref, w_ref, x_ref, g_ref, b_ref, h_ref, *, alpha):
    mix = jnp.dot(a_ref[...], w_ref[...], preferred_element_type=F32)
    h_ref[...] = _layer_norm_rows(alpha * x_ref[...] + mix, g_ref[...], b_ref[...])


def _outproj_ln(a, w, x, g, b, alpha, tm):
    m, k = a.shape
    n = w.shape[1]
    return pl.pallas_call(
        functools.partial(_outproj_ln_kernel, alpha=alpha),
        grid=(m // tm,),
        in_specs=[pl.BlockSpec((tm, k), lambda i: (i, 0)),
                  pl.BlockSpec((k, n), lambda i: (0, 0)),
                  pl.BlockSpec((tm, n), lambda i: (i, 0)),
                  pl.BlockSpec((1, n), lambda i: (0, 0)),
                  pl.BlockSpec((1, n), lambda i: (0, 0))],
        out_specs=pl.BlockSpec((tm, n), lambda i: (i, 0)),
        out_shape=jax.ShapeDtypeStruct((m, n), F32),
        compiler_params=_params(("parallel",)),
        name="outproj_ln1",
    )(a, w, x, g, b)


def _ffn_act_kernel(h_ref, wg_ref, wu_ref, o_ref, hb_ref):
    @pl.when(pl.program_id(1) == 0)
    def _():
        hb_ref[...] = h_ref[...].astype(BF16)

    hb = hb_ref[...]
    gate = jnp.dot(hb, wg_ref[...], preferred_element_type=F32)
    up = jnp.dot(hb, wu_ref[...], preferred_element_type=F32)
    o_ref[...] = (jax.nn.silu(gate) * up).astype(o_ref.dtype)


def _ffn_act(h, wg, wu, tm, tf):
    m, d = h.shape
    f = wg.shape[1]
    return pl.pallas_call(
        _ffn_act_kernel,
        grid=(m // tm, f // tf),
        in_specs=[pl.BlockSpec((tm, d), lambda i, j: (i, 0)),
                  pl.BlockSpec((d, tf), lambda i, j: (0, j)),
                  pl.BlockSpec((d, tf), lambda i, j: (0, j))],
        out_specs=pl.BlockSpec((tm, tf), lambda i, j: (i, j)),
        out_shape=jax.ShapeDtypeStruct((m, f), BF16),
        scratch_shapes=[pltpu.VMEM((tm, d), BF16)],
        compiler_params=_params(("parallel", "arbitrary")),
        name="swiglu_act",
    )(h, wg, wu)


def _ffn_down_ln_kernel(a_ref, w_ref, h_ref, g_ref, b_ref, o_ref, *, alpha):
    ff = jnp.dot(a_ref[...], w_ref[...], preferred_element_type=F32)
    o_ref[...] = _layer_norm_rows(alpha * h_ref[...] + ff, g_ref[...], b_ref[...])


def _ffn_down_ln(act, wd, h, g, b, alpha, tm):
    m, f = act.shape
    d = wd.shape[1]
    return pl.pallas_call(
        functools.partial(_ffn_down_ln_kernel, alpha=alpha),
        grid=(m // tm,),
        in_specs=[pl.BlockSpec((tm, f), lambda i: (i, 0)),
                  pl.BlockSpec((f, d), lambda i: (0, 0), pipeline_mode=pl.Buffered(1)),
                  pl.BlockSpec((tm, d), lambda i: (i, 0)),
                  pl.BlockSpec((1, d), lambda i: (0, 0)),
                  pl.BlockSpec((1, d), lambda i: (0, 0))],
        out_specs=pl.BlockSpec((tm, d), lambda i: (i, 0)),
        out_shape=jax.ShapeDtypeStruct((m, d), F32),
        compiler_params=_params(("parallel",)),
        name="swiglu_down_ln2",
    )(act, wd, h, g, b)


def _layer(x2, w_in, b_gate, na_rpb, w_na_out, s5_a_re, s5_a_im, s5_log_dt, s5_b_re, s5_b_im,
           s5_c_re, s5_c_im, s5_d, w_glu, b_glu, w_s5_out, w_out, ln1_g, ln1_b,
           w_ffn_gate, w_ffn_up, w_ffn_down, ln2_g, ln2_b, alpha):
    seq, d_model = x2.shape
    rows = seq // GRID_W
    nc = seq // S5_CHUNK
    gate_col0 = 3 * NA_WIDTH + S5_WIDTH
    in_cols = gate_col0 + 2 * d_model
    row = lambda v: v.astype(F32).reshape(1, -1)

    col_scale = jnp.where(jnp.arange(in_cols) < NA_WIDTH, NA_HEAD_DIM ** -0.5, 1.0).astype(F32).reshape(1, -1)
    col_bias = jnp.concatenate([jnp.zeros((gate_col0,), F32), b_gate.astype(F32)]).reshape(1, -1)
    proj, u = _inproj(x2, w_in.astype(BF16), col_scale, col_bias, 3 * NA_WIDTH, gate_col0, tm=1024, tn=1024)

    att = _attention(proj, _attention_bias(na_rpb, rows), seq)

    w_state, m_out, a_re, a_im = _s5_tables(s5_a_re, s5_a_im, s5_log_dt, s5_b_re, s5_b_im,
                                            s5_c_re, s5_c_im, s5_d)
    s, ug = _s5_chunk_states(u, w_state, nsplit=2)
    n_tiles = S5_WIDTH // LANES
    h_states = _s5_scan(s.reshape(n_tiles, nc, SCAN_GROUPS, S5_STATE_WIDTH), a_re, a_im)
    ys = _s5_outputs(ug, h_states.reshape(n_tiles, nc, SCAN_GROUPS * S5_STATE_WIDTH), m_out, nsplit=2)

    z = _glu(ys, w_glu.astype(BF16), row(b_glu), tm=1024, tn=512)
    merged = _merge(att, w_na_out.astype(BF16), z, w_s5_out.astype(BF16), proj, gate_col0, tm=1024, tn=512)
    h = _outproj_ln(merged, w_out.astype(BF16), x2, row(ln1_g), row(ln1_b), alpha, tm=512)
    act = _ffn_act(h, w_ffn_gate.astype(BF16), w_ffn_up.astype(BF16), tm=1024, tf=512)
    return _ffn_down_ln(act, w_ffn_down.astype(BF16), h, row(ln2_g), row(ln2_b), alpha, tm=512)


def kernel(x, w_in, b_gate, na_rpb, w_na_out, s5_a_re, s5_a_im, s5_log_dt, s5_b_re, s5_b_im, s5_c_re, s5_c_im, s5_d, w_glu, b_glu, w_s5_out, w_out, ln1_g, ln1_b, w_ffn_gate, w_ffn_up, w_ffn_down, ln2_g, ln2_b):
    bsz, seq, d_model = x.shape
    depth = w_in.shape[0]
    alpha = (2.0 * depth) ** 0.25
    outs = []
    for bi in range(bsz):
        xb = x[bi]
        for l in range(depth):
            xb = _layer(xb, w_in[l], b_gate[l], na_rpb[l], w_na_out[l], s5_a_re[l], s5_a_im[l], s5_log_dt[l],
                        s5_b_re[l], s5_b_im[l], s5_c_re[l], s5_c_im[l], s5_d[l], w_glu[l], b_glu[l],
                        w_s5_out[l], w_out[l], ln1_g[l], ln1_b[l], w_ffn_gate[l], w_ffn_up[l],
                        w_ffn_down[l], ln2_g[l], ln2_b[l], alpha)
        outs.append(xb)
    return jnp.stack(outs)
```

```python
import functools
import math

import jax
import jax.numpy as jnp
from jax import lax
from jax.experimental import pallas as pl
from jax.experimental.pallas import tpu as pltpu

F32 = jnp.float32
BF16 = jnp.bfloat16

GRID_W = 64
NA_HEADS = 8
NA_HEAD_DIM = 128
NA_WIDTH = NA_HEADS * NA_HEAD_DIM
NA_ROWS = 8
NA_COLS = 16
S5_GROUP = 16
S5_GROUPS = 64
S5_WIDTH = S5_GROUP * S5_GROUPS
S5_STATE = 64
LN_EPS = 1e-5
MASK_VALUE = -1e30

S5_CHUNK = 16
S5_CHUNK_WIDTH = S5_CHUNK * S5_GROUP
S5_STATE_WIDTH = 4 * S5_STATE
SCAN_GROUPS = 8

ATT_Q_ROWS = 8
ATT_K_ROWS = 16
ATT_Q = ATT_Q_ROWS * GRID_W
ATT_K = ATT_K_ROWS * GRID_W
ATT_KBLK = 256
LANES = 128
ATT_WIN = NA_ROWS * GRID_W + LANES
ATT_HEADS_PER_STEP = 2
S5_LANE_GROUPS = LANES // S5_GROUP
RELAYOUT_ROWS = 16

VMEM_LIMIT = 56 * 1024 * 1024


def _params(sem):
    return pltpu.CompilerParams(dimension_semantics=sem, vmem_limit_bytes=VMEM_LIMIT)


def _layer_norm_rows(y, g, b):
    mu = jnp.mean(y, axis=-1, keepdims=True)
    d = y - mu
    var = jnp.mean(d * d, axis=-1, keepdims=True)
    return d * lax.rsqrt(var + LN_EPS) * g + b


def _inproj_kernel(x_ref, w_ref, s_ref, b_ref, o_ref, u_ref, xb_ref, *, u_tile0, gate_tile0):
    j = pl.program_id(1)

    @pl.when(j == 0)
    def _():
        xb_ref[...] = x_ref[...].astype(BF16)

    @pl.when(j < u_tile0)
    def _():
        acc = jnp.dot(xb_ref[...], w_ref[...], preferred_element_type=F32)
        o_ref[...] = (acc * s_ref[...]).astype(o_ref.dtype)

    @pl.when((j >= u_tile0) & (j < gate_tile0))
    def _():
        acc = jnp.dot(xb_ref[...], w_ref[...], preferred_element_type=F32)
        o_ref[...] = acc.astype(o_ref.dtype)
        u_ref[...] = acc

    @pl.when(j >= gate_tile0)
    def _():
        acc = jnp.dot(xb_ref[...], w_ref[...], preferred_element_type=F32)
        o_ref[...] = jax.nn.sigmoid(acc + b_ref[...]).astype(o_ref.dtype)


def _inproj(x, w, col_scale, col_bias, u_col0, gate_col0, tm, tn):
    m, k = x.shape
    n = w.shape[1]
    u_tile0, gate_tile0 = u_col0 // tn, gate_col0 // tn
    return pl.pallas_call(
        functools.partial(_inproj_kernel, u_tile0=u_tile0, gate_tile0=gate_tile0),
        grid=(m // tm, n // tn),
        in_specs=[
            pl.BlockSpec((tm, k), lambda i, j: (i, 0)),
            pl.BlockSpec((k, tn), lambda i, j: (0, j)),
            pl.BlockSpec((1, tn), lambda i, j: (0, j)),
            pl.BlockSpec((1, tn), lambda i, j: (0, j)),
        ],
        out_specs=[pl.BlockSpec((tm, tn), lambda i, j: (i, j)),
                   pl.BlockSpec((tm, tn), lambda i, j: (i, jnp.clip(j - u_tile0, 0, gate_tile0 - u_tile0 - 1)))],
        out_shape=[jax.ShapeDtypeStruct((m, n), BF16),
                   jax.ShapeDtypeStruct((m, gate_col0 - u_col0), F32)],
        scratch_shapes=[pltpu.VMEM((tm, k), BF16)],
        compiler_params=_params(("parallel", "arbitrary")),
        name="inproj",
    )(x, w, col_scale, col_bias)


def _attn_windows(rows):
    nb = rows // ATT_Q_ROWS
    kinds = []
    for b in (0, 1, nb - 1):
        base = min(max(b * ATT_Q_ROWS - NA_ROWS // 2, 0), rows - ATT_K_ROWS)
        geo = []
        for ql in range(ATT_Q_ROWS):
            qr = b * ATT_Q_ROWS + ql
            r0 = min(max(qr - NA_ROWS // 2, 0), rows - NA_ROWS)
            koff = r0 - base
            lane0 = min(LANES * (koff * GRID_W // LANES), ATT_K - ATT_WIN)
            geo.append((lane0, koff * GRID_W - lane0, r0 - qr + NA_ROWS - 1))
        kinds.append(tuple(geo))
    return tuple(kinds)


def _attn_kernel(q_ref, k0, k1, k2, k3, v0, v1, v2, v3, bias_ref, o_ref, p_ref, *, windows, nb):
    b = pl.program_id(1)
    k_refs, v_refs = (k0, k1, k2, k3), (v0, v1, v2, v3)

    def one_head(hh, geo):
        c = slice(hh * NA_HEAD_DIM, (hh + 1) * NA_HEAD_DIM)
        k = jnp.concatenate([r[:, c] for r in k_refs], axis=0)
        v = jnp.concatenate([r[:, c] for r in v_refs], axis=0)
        s = lax.dot_general(q_ref[:, c], k, (((1,), (1,)), ((), ())), preferred_element_type=F32)
        inv = []
        for ql, (lane0, _, _) in enumerate(geo):
            r = slice(ql * GRID_W, (ql + 1) * GRID_W)
            sw = s[r, lane0:lane0 + ATT_WIN] + bias_ref[hh, ql]
            m = jnp.max(sw, axis=-1, keepdims=True)
            p = jnp.exp(sw - m)
            inv.append(1.0 / jnp.sum(p, axis=-1, keepdims=True))
            if lane0 > 0:
                p_ref[hh, r, :lane0] = jnp.zeros((GRID_W, lane0), BF16)
            p_ref[hh, r, lane0:lane0 + ATT_WIN] = p.astype(BF16)
            if lane0 + ATT_WIN < ATT_K:
                p_ref[hh, r, lane0 + ATT_WIN:] = jnp.zeros((GRID_W, ATT_K - lane0 - ATT_WIN), BF16)
        o = jnp.dot(p_ref[hh], v, preferred_element_type=F32)
        o_ref[:, c] = (o * jnp.concatenate(inv, axis=0)).astype(o_ref.dtype)

    def block_kind(geo):
        for hh in range(ATT_HEADS_PER_STEP):
            one_head(hh, geo)

    @pl.when(b == 0)
    def _():
        block_kind(windows[0])

    @pl.when((b > 0) & (b < nb - 1))
    def _():
        block_kind(windows[1])

    @pl.when(b == nb - 1)
    def _():
        block_kind(windows[2])


def _attention(proj, bias, seq):
    nb = seq // ATT_Q
    n_kblk = seq // ATT_KBLK
    per_blk = ATT_K // ATT_KBLK

    def kstart(b):
        return jnp.clip(2 * b - 1, 0, n_kblk - per_blk)

    def kind(b):
        return jnp.where(b == 0, 0, jnp.where(b == nb - 1, 2, 1))

    hps = ATT_HEADS_PER_STEP
    width = hps * NA_HEAD_DIM
    n_groups = NA_HEADS // hps

    def kv_spec(col0, t):
        return pl.BlockSpec((ATT_KBLK, width), lambda h, b: (kstart(b) + t, col0 + h))

    in_specs = [pl.BlockSpec((ATT_Q, width), lambda h, b: (b, h))]
    in_specs += [kv_spec(n_groups, t) for t in range(per_blk)]
    in_specs += [kv_spec(2 * n_groups, t) for t in range(per_blk)]
    in_specs += [pl.BlockSpec((None, hps, ATT_Q_ROWS, GRID_W, ATT_WIN), lambda h, b: (kind(b), h, 0, 0, 0))]
    return pl.pallas_call(
        functools.partial(_attn_kernel, windows=_attn_windows(seq // GRID_W), nb=nb),
        grid=(n_groups, nb),
        in_specs=in_specs,
        out_specs=pl.BlockSpec((ATT_Q, width), lambda h, b: (b, h)),
        out_shape=jax.ShapeDtypeStruct((seq, NA_WIDTH), BF16),
        scratch_shapes=[pltpu.VMEM((hps, ATT_Q, ATT_K), BF16)],
        compiler_params=_params(("parallel", "arbitrary")),
        name="na_attention",
    )(*([proj] * (1 + 2 * per_blk)), bias)


def _attention_bias(rpb, rows):
    cols = jnp.arange(GRID_W)
    col_start = jnp.clip(cols - NA_COLS // 2, 0, GRID_W - NA_COLS)
    col_ok = (cols[None, :] >= col_start[:, None]) & (cols[None, :] < col_start[:, None] + NA_COLS)
    r32 = rpb.astype(F32)
    edge = GRID_W - NA_COLS
    ext = jnp.concatenate([jnp.repeat(r32[..., :1], edge, axis=-1), r32,
                           jnp.repeat(r32[..., -1:], edge, axis=-1)], axis=-1)
    by_col = jnp.stack([ext[..., GRID_W - 1 - qc:2 * GRID_W - 1 - qc] for qc in range(GRID_W)], axis=2)
    by_col = jnp.where(col_ok[None, None], by_col, MASK_VALUE).transpose(0, 2, 1, 3)
    kinds = []
    for geo in _attn_windows(rows):
        strips = []
        for _, phase, rfirst in geo:
            strip = by_col[:, :, rfirst:rfirst + NA_ROWS, :].reshape(NA_HEADS, GRID_W, NA_ROWS * GRID_W)
            strips.append(jnp.pad(strip, ((0, 0), (0, 0), (phase, ATT_WIN - NA_ROWS * GRID_W - phase)),
                                  constant_values=MASK_VALUE))
        kinds.append(jnp.stack(strips, axis=1))
    return jnp.stack(kinds)


def _s5_tables(a_re, a_im, log_dt, b_re, b_im, c_re, c_im, d):
    t_len = S5_CHUNK
    groups = a_re.shape[1]
    n = jnp.arange(t_len + 1, dtype=F32)[None, :, None]

    def cmul(xr, xi, yr, yi):
        return xr * yr - xi * yi, xr * yi + xi * yr

    swap = lambda v: jnp.swapaxes(v, 1, 2)
    w_cols, m_rows, lag_kernels, step_r, step_i = [], [], [], [], []
    for k in range(2):
        ar, ai = a_re[k].astype(F32), a_im[k].astype(F32)
        dt = jnp.exp(log_dt[k].astype(F32))[:, None]
        zr, zi = (ar * dt)[:, None, :], (ai * dt)[:, None, :]
        mag = jnp.exp(zr * n)
        pr, pi = mag * jnp.cos(zi * n), mag * jnp.sin(zi * n)
        lr, li = pr[:, 1] - 1.0, pi[:, 1]
        den = ar * ar + ai * ai
        fr, fi = (lr * ar + li * ai) / den, (li * ar - lr * ai) / den
        bbr, bbi = cmul(fr[:, None, :], fi[:, None, :],
                        swap(b_re[k].astype(F32)), swap(b_im[k].astype(F32)))
        crt, cit = swap(c_re[k].astype(F32)), swap(c_im[k].astype(F32))
        prt, pit = swap(pr), swap(pi)
        wp_r = pr[:, :t_len][:, ::-1] if k == 0 else pr[:, :t_len]
        wp_i = pi[:, :t_len][:, ::-1] if k == 0 else pi[:, :t_len]
        wr, wi = cmul(wp_r[:, :, None, :], wp_i[:, :, None, :], bbr[:, None, :, :], bbi[:, None, :, :])
        w_cols.append((wr.reshape(groups, S5_CHUNK_WIDTH, S5_STATE), wi.reshape(groups, S5_CHUNK_WIDTH, S5_STATE)))
        ep_r = prt[:, :, 1:] if k == 0 else prt[:, :, 1:][:, :, ::-1]
        ep_i = pit[:, :, 1:] if k == 0 else pit[:, :, 1:][:, :, ::-1]
        er, ei = cmul(crt[:, :, None, :], cit[:, :, None, :], ep_r[:, :, :, None], ep_i[:, :, :, None])
        m_rows.append((er.reshape(groups, S5_STATE, S5_CHUNK_WIDTH), (-ei).reshape(groups, S5_STATE, S5_CHUNK_WIDTH)))
        xr, xi = cmul(crt[:, :, None, :], cit[:, :, None, :], prt[:, :, :t_len, None], pit[:, :, :t_len, None])
        lhs = jnp.concatenate([bbr, -bbi], axis=-1)
        rhs = jnp.concatenate([xr, xi], axis=1).reshape(groups, 2 * S5_STATE, S5_CHUNK_WIDTH)
        lag_kernels.append(jnp.einsum("gck,gkn->gcn", lhs, rhs, precision=lax.Precision.HIGHEST))
        step_r.append(pr[:, t_len])
        step_i.append(pi[:, t_len])

    w_state = jnp.concatenate([w_cols[0][0], w_cols[1][0], w_cols[0][1], w_cols[1][1]], axis=-1)
    m_state = jnp.concatenate([m_rows[0][0], m_rows[1][0], m_rows[0][1], m_rows[1][1]], axis=1)
    kf, kb = lag_kernels
    kb_rev = kb[:, :, S5_GROUP:].reshape(groups, S5_GROUP, t_len - 1, S5_GROUP)[:, :, ::-1]
    lag0 = kf[:, :, :S5_GROUP] + kb[:, :, :S5_GROUP] + d.astype(F32)[:, None, :] * jnp.eye(S5_GROUP, dtype=F32)[None]
    lags = jnp.concatenate([kb_rev.reshape(groups, S5_GROUP, -1), lag0, kf[:, :, S5_GROUP:]], axis=-1)
    toe = jnp.stack([lags[:, :, S5_GROUP * (t_len - 1 - s):S5_GROUP * (2 * t_len - 1 - s)] for s in range(t_len)],
                    axis=1).reshape(groups, S5_CHUNK_WIDTH, S5_CHUNK_WIDTH)
    m_out = jnp.concatenate([toe, m_state], axis=1)
    return (w_state.astype(BF16), m_out.astype(BF16),
            jnp.concatenate(step_r, axis=-1), jnp.concatenate(step_i, axis=-1))


def _lane_group_ids():
    return lax.broadcasted_iota(jnp.int32, (RELAYOUT_ROWS, LANES), 1) // S5_GROUP


def _transpose_lane_blocks(pieces, grp):
    nblk = S5_LANE_GROUPS
    rolled = []
    for r in range(nblk):
        acc = None
        for b in range(nblk):
            acc = pieces[b] if acc is None else jnp.where(grp == (b - r) % nblk, pieces[b], acc)
        rolled.append(pltpu.roll(acc, S5_GROUP * r, 1) if r else acc)
    out = []
    for k in range(nblk):
        acc = None
        for b in range(nblk):
            z = rolled[(b - k) % nblk]
            acc = z if acc is None else jnp.where(grp == b, z, acc)
        out.append(acc)
    return out


def _s5_state_kernel(u_ref, w_ref, ar_ref, ai_ref, ug_ref, h_ref, s_ref):
    lg = S5_LANE_GROUPS
    nc = ug_ref.shape[1]
    grp = _lane_group_ids()

    def regroup(rb, carry):
        r0 = pl.multiple_of(rb * RELAYOUT_ROWS, RELAYOUT_ROWS)
        for half in range(S5_CHUNK_WIDTH // LANES):
            steps = [u_ref[pl.ds(r0 * S5_CHUNK + half * S5_LANE_GROUPS + tp, RELAYOUT_ROWS, stride=S5_CHUNK), :]
                     for tp in range(S5_LANE_GROUPS)]
            for g, v in enumerate(_transpose_lane_blocks(steps, grp)):
                ug_ref[g, pl.ds(r0, RELAYOUT_ROWS), half * LANES:(half + 1) * LANES] = v.astype(BF16)
        return carry

    lax.fori_loop(0, nc // RELAYOUT_ROWS, regroup, 0, unroll=2)

    half = S5_STATE_WIDTH // 2
    for g in range(lg):
        sg = jnp.dot(ug_ref[g], w_ref[g], preferred_element_type=F32)
        s_ref[0, pl.ds(g, nc, stride=lg), :] = sg[:, :half]
        s_ref[1, pl.ds(g, nc, stride=lg), :] = sg[:, half:]

    lane = lax.broadcasted_iota(jnp.int32, (lg, half), 1)
    fwd = lane < S5_STATE
    ar = ar_ref[...]
    ai = ai_ref[...]
    rows = lambda i: pl.ds(pl.multiple_of(i * lg, lg), lg)

    def advance(i, hr, hi):
        sr = jnp.where(fwd, s_ref[0, rows(i), :], s_ref[0, rows(nc - 1 - i), :])
        si = jnp.where(fwd, s_ref[1, rows(i), :], s_ref[1, rows(nc - 1 - i), :])
        return ar * hr - ai * hi + sr, ar * hi + ai * hr + si

    def first_touch(i, carry):
        hr, hi = carry
        for c in (i, nc - 1 - i):
            h_ref[0, rows(c), :] = hr
            h_ref[1, rows(c), :] = hi
        return advance(i, hr, hi)

    def second_touch(i, carry):
        hr, hi = carry
        j = nc - 1 - i
        h_ref[0, rows(i), :] = jnp.where(fwd, hr, h_ref[0, rows(i), :])
        h_ref[1, rows(i), :] = jnp.where(fwd, hi, h_ref[1, rows(i), :])
        h_ref[0, rows(j), :] = jnp.where(fwd, h_ref[0, rows(j), :], hr)
        h_ref[1, rows(j), :] = jnp.where(fwd, h_ref[1, rows(j), :], hi)
        return advance(i, hr, hi)

    zero = jnp.zeros((lg, half), F32)
    carry = lax.fori_loop(0, nc // 2, first_touch, (zero, zero))
    lax.fori_loop(nc // 2, nc, second_touch, carry)


def _s5_states(u, w_state, a_re, a_im):
    seq, width = u.shape
    g = width // S5_GROUP
    nc = seq // S5_CHUNK
    lg = S5_LANE_GROUPS
    half = S5_STATE_WIDTH // 2
    return pl.pallas_call(
        _s5_state_kernel,
        grid=(width // LANES,),
        in_specs=[pl.BlockSpec((seq, LANES), lambda j: (0, j)),
                  pl.BlockSpec((lg, S5_CHUNK_WIDTH, S5_STATE_WIDTH), lambda j: (j, 0, 0)),
                  pl.BlockSpec((lg, half), lambda j: (j, 0)),
                  pl.BlockSpec((lg, half), lambda j: (j, 0))],
        out_specs=[pl.BlockSpec((lg, nc, S5_CHUNK_WIDTH), lambda j: (j, 0, 0)),
                   pl.BlockSpec((None, 2, nc * lg, half), lambda j: (j, 0, 0, 0))],
        out_shape=[jax.ShapeDtypeStruct((g, nc, S5_CHUNK_WIDTH), BF16),
                   jax.ShapeDtypeStruct((g // lg, 2, nc * lg, half), F32)],
        scratch_shapes=[pltpu.VMEM((2, nc * lg, half), F32)],
        compiler_params=_params(("parallel",)),
        name="s5_states",
    )(u, w_state, a_re, a_im)


def _s5_out_kernel(ug_ref, h_ref, m_ref, o_ref, y_ref):
    cw = S5_CHUNK_WIDTH
    nch = ug_ref.shape[1]
    for g in range(S5_LANE_GROUPS):
        hg = jnp.concatenate([h_ref[part, pl.ds(g, nch, stride=S5_LANE_GROUPS), :] for part in range(2)],
                             axis=1).astype(BF16)
        y = jnp.dot(ug_ref[g], m_ref[g, :cw, :], preferred_element_type=F32)
        y = y + jnp.dot(hg, m_ref[g, cw:, :], preferred_element_type=F32)
        y_ref[g] = jax.nn.gelu(y, approximate=True)
    grp = _lane_group_ids()

    def regroup(rb, carry):
        r0 = pl.multiple_of(rb * RELAYOUT_ROWS, RELAYOUT_ROWS)
        for half in range(cw // LANES):
            groups = [y_ref[g, pl.ds(r0, RELAYOUT_ROWS), half * LANES:(half + 1) * LANES]
                      for g in range(S5_LANE_GROUPS)]
            for tp, v in enumerate(_transpose_lane_blocks(groups, grp)):
                t = half * S5_LANE_GROUPS + tp
                o_ref[pl.ds(r0 * S5_CHUNK + t, RELAYOUT_ROWS, stride=S5_CHUNK), :] = v
        return carry

    lax.fori_loop(0, nch // RELAYOUT_ROWS, regroup, 0, unroll=2)


def _s5_outputs(ug, h3, m_out, nsplit):
    g, nc, cw = ug.shape
    nch = nc // nsplit
    lg = S5_LANE_GROUPS
    return pl.pallas_call(
        _s5_out_kernel,
        grid=(g // lg, nsplit),
        in_specs=[pl.BlockSpec((lg, nch, cw), lambda j, h: (j, h, 0)),
                  pl.BlockSpec((None, 2, nch * lg, S5_STATE_WIDTH // 2), lambda j, h: (j, 0, h, 0)),
                  pl.BlockSpec((lg, cw + S5_STATE_WIDTH, cw), lambda j, h: (j, 0, 0))],
        out_specs=pl.BlockSpec((nch * S5_CHUNK, LANES), lambda j, h: (h, j)),
        out_shape=jax.ShapeDtypeStruct((nc * S5_CHUNK, g * S5_GROUP), F32),
        scratch_shapes=[pltpu.VMEM((lg, nch, cw), F32)],
        compiler_params=_params(("parallel", "parallel")),
        name="s5_outputs",
    )(ug, h3, m_out)


def _glu_kernel(a_ref, w_ref, b_ref, y_ref, o_ref, ab_ref):
    @pl.when(pl.program_id(1) == 0)
    def _():
        ab_ref[...] = a_ref[...].astype(BF16)

    t = jnp.dot(ab_ref[...], w_ref[...], preferred_element_type=F32) + b_ref[...]
    o_ref[...] = (y_ref[...] * jax.nn.sigmoid(t)).astype(o_ref.dtype)


def _glu(ys, w, b, tm, tn):
    m, k = ys.shape
    n = w.shape[1]
    return pl.pallas_call(
        _glu_kernel,
        grid=(m // tm, n // tn),
        in_specs=[pl.BlockSpec((tm, k), lambda i, j: (i, 0)),
                  pl.BlockSpec((k, tn), lambda i, j: (0, j)),
                  pl.BlockSpec((1, tn), lambda i, j: (0, j)),
                  pl.BlockSpec((tm, tn), lambda i, j: (i, j))],
        out_specs=pl.BlockSpec((tm, tn), lambda i, j: (i, j)),
        out_shape=jax.ShapeDtypeStruct((m, n), BF16),
        scratch_shapes=[pltpu.VMEM((tm, k), BF16)],
        compiler_params=_params(("parallel", "arbitrary")),
        name="s5_glu",
    )(ys, w, b, ys)


def _merge_kernel(a1_ref, w1_ref, a2_ref, w2_ref, g1_ref, g2_ref, o_ref):
    y1 = jnp.dot(a1_ref[...], w1_ref[...], preferred_element_type=F32)
    y2 = jnp.dot(a2_ref[...], w2_ref[...], preferred_element_type=F32)
    o_ref[...] = (g1_ref[...].astype(F32) * y1 + g2_ref[...].astype(F32) * y2).astype(o_ref.dtype)


def _merge(att, w_na, z, w_s5, proj, gate_col0, tm, tn):
    m, k1 = att.shape
    k2 = z.shape[1]
    n = w_na.shape[1]
    g1 = gate_col0 // tn
    g2 = (gate_col0 + n) // tn
    return pl.pallas_call(
        _merge_kernel,
        grid=(m // tm, n // tn),
        in_specs=[pl.BlockSpec((tm, k1), lambda i, j: (i, 0)),
                  pl.BlockSpec((k1, tn), lambda i, j: (0, j)),
                  pl.BlockSpec((tm, k2), lambda i, j: (i, 0)),
                  pl.BlockSpec((k2, tn), lambda i, j: (0, j)),
                  pl.BlockSpec((tm, tn), lambda i, j: (i, g1 + j)),
                  pl.BlockSpec((tm, tn), lambda i, j: (i, g2 + j))],
        out_specs=pl.BlockSpec((tm, tn), lambda i, j: (i, j)),
        out_shape=jax.ShapeDtypeStruct((m, n), BF16),
        compiler_params=_params(("parallel", "arbitrary")),
        name="gated_merge",
    )(att, w_na, z, w_s5, proj, proj)


def _outproj_ln_kernel(a_ref, w_ref, x_ref, g_ref, b_ref, h_ref, *, alpha):
    mix = jnp.dot(a_ref[...], w_ref[...], preferred_element_type=F32)
    h_ref[...] = _layer_norm_rows(alpha * x_ref[...] + mix, g_ref[...], b_ref[...])


def _outproj_ln(a, w, x, g, b, alpha, tm):
    m, k = a.shape
    n = w.shape[1]
    return pl.pallas_call(
        functools.partial(_outproj_ln_kernel, alpha=alpha),
        grid=(m // tm,),
        in_specs=[pl.BlockSpec((tm, k), lambda i: (i, 0)),
                  pl.BlockSpec((k, n), lambda i: (0, 0)),
                  pl.BlockSpec((tm, n), lambda i: (i, 0)),
                  pl.BlockSpec((1, n), lambda i: (0, 0)),
                  pl.BlockSpec((1, n), lambda i: (0, 0))],
        out_specs=pl.BlockSpec((tm, n), lambda i: (i, 0)),
        out_shape=jax.ShapeDtypeStruct((m, n), F32),
        compiler_params=_params(("parallel",)),
        name="outproj_ln1",
    )(a, w, x, g, b)


def _ffn_act_kernel(h_ref, wg_ref, wu_ref, o_ref, hb_ref):
    @pl.when(pl.program_id(1) == 0)
    def _():
        hb_ref[...] = h_ref[...].astype(BF16)

    hb = hb_ref[...]
    gate = jnp.dot(hb, wg_ref[...], preferred_element_type=F32)
    up = jnp.dot(hb, wu_ref[...], preferred_element_type=F32)
    o_ref[...] = (jax.nn.silu(gate) * up).astype(o_ref.dtype)


def _ffn_act(h, wg, wu, tm, tf):
    m, d = h.shape
    f = wg.shape[1]
    return pl.pallas_call(
        _ffn_act_kernel,
        grid=(m // tm, f // tf),
        in_specs=[pl.BlockSpec((tm, d), lambda i, j: (i, 0)),
                  pl.BlockSpec((d, tf), lambda i, j: (0, j)),
                  pl.BlockSpec((d, tf), lambda i, j: (0, j))],
        out_specs=pl.BlockSpec((tm, tf), lambda i, j: (i, j)),
        out_shape=jax.ShapeDtypeStruct((m, f), BF16),
        scratch_shapes=[pltpu.VMEM((tm, d), BF16)],
        compiler_params=_params(("parallel", "arbitrary")),
        name="swiglu_act",
    )(h, wg, wu)


def _ffn_down_ln_kernel(a_ref, w_ref, h_ref, g_ref, b_ref, o_ref, *, alpha):
    ff = jnp.dot(a_ref[...], w_ref[...], preferred_element_type=F32)
    o_ref[...] = _layer_norm_rows(alpha * h_ref[...] + ff, g_ref[...], b_ref[...])


def _ffn_down_ln(act, wd, h, g, b, alpha, tm):
    m, f = act.shape
    d = wd.shape[1]
    return pl.pallas_call(
        functools.partial(_ffn_down_ln_kernel, alpha=alpha),
        grid=(m // tm,),
        in_specs=[pl.BlockSpec((tm, f), lambda i: (i, 0)),
                  pl.BlockSpec((f, d), lambda i: (0, 0), pipeline_mode=pl.Buffered(1)),
                  pl.BlockSpec((tm, d), lambda i: (i, 0)),
                  pl.BlockSpec((1, d), lambda i: (0, 0)),
                  pl.BlockSpec((1, d), lambda i: (0, 0))],
        out_specs=pl.BlockSpec((tm, d), lambda i: (i, 0)),
        out_shape=jax.ShapeDtypeStruct((m, d), F32),
        compiler_params=_params(("parallel",)),
        name="swiglu_down_ln2",
    )(act, wd, h, g, b)


def _layer(x2, w_in, b_gate, na_rpb, w_na_out, s5_a_re, s5_a_im, s5_log_dt, s5_b_re, s5_b_im,
           s5_c_re, s5_c_im, s5_d, w_glu, b_glu, w_s5_out, w_out, ln1_g, ln1_b,
           w_ffn_gate, w_ffn_up, w_ffn_down, ln2_g, ln2_b, alpha):
    seq, d_model = x2.shape
    rows = seq // GRID_W
    nc = seq // S5_CHUNK
    gate_col0 = 3 * NA_WIDTH + S5_WIDTH
    in_cols = gate_col0 + 2 * d_model
    row = lambda v: v.astype(F32).reshape(1, -1)

    col_scale = jnp.where(jnp.arange(in_cols) < NA_WIDTH, NA_HEAD_DIM ** -0.5, 1.0).astype(F32).reshape(1, -1)
    col_bias = jnp.concatenate([jnp.zeros((gate_col0,), F32), b_gate.astype(F32)]).reshape(1, -1)
    proj, u = _inproj(x2, w_in.astype(BF16), col_scale, col_bias, 3 * NA_WIDTH, gate_col0, tm=1024, tn=1024)

    att = _attention(proj, _attention_bias(na_rpb, rows), seq)

    w_state, m_out, a_re, a_im = _s5_tables(s5_a_re, s5_a_im, s5_log_dt, s5_b_re, s5_b_im,
                                            s5_c_re, s5_c_im, s5_d)
    ug, h_states = _s5_states(u, w_state, a_re, a_im)
    ys = _s5_outputs(ug, h_states, m_out, nsplit=2)

    z = _glu(ys, w_glu.astype(BF16), row(b_glu), tm=1024, tn=512)
    merged = _merge(att, w_na_out.astype(BF16), z, w_s5_out.astype(BF16), proj, gate_col0, tm=1024, tn=512)
    h = _outproj_ln(merged, w_out.astype(BF16), x2, row(ln1_g), row(ln1_b), alpha, tm=512)
    act = _ffn_act(h, w_ffn_gate.astype(BF16), w_ffn_up.astype(BF16), tm=1024, tf=512)
    return _ffn_down_ln(act, w_ffn_down.astype(BF16), h, row(ln2_g), row(ln2_b), alpha, tm=512)


def kernel(x, w_in, b_gate, na_rpb, w_na_out, s5_a_re, s5_a_im, s5_log_dt, s5_b_re, s5_b_im, s5_c_re, s5_c_im, s5_d, w_glu, b_glu, w_s5_out, w_out, ln1_g, ln1_b, w_ffn_gate, w_ffn_up, w_ffn_down, ln2_g, ln2_b):
    bsz, seq, d_model = x.shape
    depth = w_in.shape[0]
    alpha = (2.0 * depth) ** 0.25
    outs = []
    for bi in range(bsz):
        xb = x[bi]
        for l in range(depth):
            xb = _layer(xb, w_in[l], b_gate[l], na_rpb[l], w_na_out[l], s5_a_re[l], s5_a_im[l], s5_log_dt[l],
                        s5_b_re[l], s5_b_im[l], s5_c_re[l], s5_c_im[l], s5_d[l], w_glu[l], b_glu[l],
                        w_s5_out[l], w_out[l], ln1_g[l], ln1_b[l], w_ffn_gate[l], w_ffn_up[l],
                        w_ffn_down[l], ln2_g[l], ln2_b[l], alpha)
        outs.append(xb)
    return jnp.stack(outs)
```

```python
import functools
import math

import jax
import jax.numpy as jnp
from jax import lax
from jax.experimental import pallas as pl
from jax.experimental.pallas import tpu as pltpu

F32 = jnp.float32
BF16 = jnp.bfloat16

GRID_W = 64
NA_HEADS = 8
NA_HEAD_DIM = 128
NA_WIDTH = NA_HEADS * NA_HEAD_DIM
NA_ROWS = 8
NA_COLS = 16
S5_GROUP = 16
S5_GROUPS = 64
S5_WIDTH = S5_GROUP * S5_GROUPS
S5_STATE = 64
LN_EPS = 1e-5
MASK_VALUE = -1e30

S5_CHUNK = 16
S5_CHUNK_WIDTH = S5_CHUNK * S5_GROUP
S5_STATE_WIDTH = 4 * S5_STATE
SCAN_GROUPS = 8

ATT_Q_ROWS = 8
ATT_K_ROWS = 16
ATT_Q = ATT_Q_ROWS * GRID_W
ATT_K = ATT_K_ROWS * GRID_W
ATT_KBLK = 256
LANES = 128
ATT_WIN = NA_ROWS * GRID_W + LANES
ATT_HEADS_PER_STEP = 2
LN_ROW_SPLIT = 4
S5_LANE_GROUPS = LANES // S5_GROUP
RELAYOUT_ROWS = 16

VMEM_LIMIT = 56 * 1024 * 1024


def _params(sem):
    return pltpu.CompilerParams(dimension_semantics=sem, vmem_limit_bytes=VMEM_LIMIT)


def _layer_norm_rows(y, g, b):
    mu = jnp.mean(y, axis=-1, keepdims=True)
    d = y - mu
    var = jnp.mean(d * d, axis=-1, keepdims=True)
    return d * lax.rsqrt(var + LN_EPS) * g + b


def _inproj_kernel(x_ref, w_ref, s_ref, b_ref, o_ref, u_ref, xb_ref, *, u_tile0, gate_tile0):
    j = pl.program_id(1)

    @pl.when(j == 0)
    def _():
        xb_ref[...] = x_ref[...].astype(BF16)

    @pl.when(j < u_tile0)
    def _():
        acc = jnp.dot(xb_ref[...], w_ref[...], preferred_element_type=F32)
        o_ref[...] = (acc * s_ref[...]).astype(o_ref.dtype)

    @pl.when((j >= u_tile0) & (j < gate_tile0))
    def _():
        acc = jnp.dot(xb_ref[...], w_ref[...], preferred_element_type=F32)
        o_ref[...] = acc.astype(o_ref.dtype)
        u_ref[...] = acc

    @pl.when(j >= gate_tile0)
    def _():
        acc = jnp.dot(xb_ref[...], w_ref[...], preferred_element_type=F32)
        o_ref[...] = jax.nn.sigmoid(acc + b_ref[...]).astype(o_ref.dtype)


def _inproj(x, w, col_scale, col_bias, u_col0, gate_col0, tm, tn):
    m, k = x.shape
    n = w.shape[1]
    u_tile0, gate_tile0 = u_col0 // tn, gate_col0 // tn
    return pl.pallas_call(
        functools.partial(_inproj_kernel, u_tile0=u_tile0, gate_tile0=gate_tile0),
        grid=(m // tm, n // tn),
        in_specs=[
            pl.BlockSpec((tm, k), lambda i, j: (i, 0)),
            pl.BlockSpec((k, tn), lambda i, j: (0, j)),
            pl.BlockSpec((1, tn), lambda i, j: (0, j)),
            pl.BlockSpec((1, tn), lambda i, j: (0, j)),
        ],
        out_specs=[pl.BlockSpec((tm, tn), lambda i, j: (i, j)),
                   pl.BlockSpec((tm, tn), lambda i, j: (i, jnp.clip(j - u_tile0, 0, gate_tile0 - u_tile0 - 1)))],
        out_shape=[jax.ShapeDtypeStruct((m, n), BF16),
                   jax.ShapeDtypeStruct((m, gate_col0 - u_col0), F32)],
        scratch_shapes=[pltpu.VMEM((tm, k), BF16)],
        compiler_params=_params(("parallel", "arbitrary")),
        name="inproj",
    )(x, w, col_scale, col_bias)


def _attn_windows(rows):
    nb = rows // ATT_Q_ROWS
    kinds = []
    for b in (0, 1, nb - 1):
        base = min(max(b * ATT_Q_ROWS - NA_ROWS // 2, 0), rows - ATT_K_ROWS)
        geo = []
        for ql in range(ATT_Q_ROWS):
            qr = b * ATT_Q_ROWS + ql
            r0 = min(max(qr - NA_ROWS // 2, 0), rows - NA_ROWS)
            koff = r0 - base
            lane0 = min(LANES * (koff * GRID_W // LANES), ATT_K - ATT_WIN)
            geo.append((lane0, koff * GRID_W - lane0, r0 - qr + NA_ROWS - 1))
        kinds.append(tuple(geo))
    return tuple(kinds)


def _attn_kernel(q_ref, k0, k1, k2, k3, v0, v1, v2, v3, bias_ref, o_ref, p_ref, *, windows, nb):
    b = pl.program_id(1)
    k_refs, v_refs = (k0, k1, k2, k3), (v0, v1, v2, v3)

    def one_head(hh, geo):
        c = slice(hh * NA_HEAD_DIM, (hh + 1) * NA_HEAD_DIM)
        k = jnp.concatenate([r[:, c] for r in k_refs], axis=0)
        v = jnp.concatenate([r[:, c] for r in v_refs], axis=0)
        s = lax.dot_general(q_ref[:, c], k, (((1,), (1,)), ((), ())), preferred_element_type=F32)
        inv = []
        for ql, (lane0, _, _) in enumerate(geo):
            r = slice(ql * GRID_W, (ql + 1) * GRID_W)
            sw = s[r, lane0:lane0 + ATT_WIN] + bias_ref[hh, ql]
            m = jnp.max(sw, axis=-1, keepdims=True)
            p = jnp.exp(sw - m)
            inv.append(1.0 / jnp.sum(p, axis=-1, keepdims=True))
            if lane0 > 0:
                p_ref[hh, r, :lane0] = jnp.zeros((GRID_W, lane0), BF16)
            p_ref[hh, r, lane0:lane0 + ATT_WIN] = p.astype(BF16)
            if lane0 + ATT_WIN < ATT_K:
                p_ref[hh, r, lane0 + ATT_WIN:] = jnp.zeros((GRID_W, ATT_K - lane0 - ATT_WIN), BF16)
        o = jnp.dot(p_ref[hh], v, preferred_element_type=F32)
        o_ref[:, c] = (o * jnp.concatenate(inv, axis=0)).astype(o_ref.dtype)

    def block_kind(geo):
        for hh in range(ATT_HEADS_PER_STEP):
            one_head(hh, geo)

    @pl.when(b == 0)
    def _():
        block_kind(windows[0])

    @pl.when((b > 0) & (b < nb - 1))
    def _():
        block_kind(windows[1])

    @pl.when(b == nb - 1)
    def _():
        block_kind(windows[2])


def _attention(proj, bias, seq):
    nb = seq // ATT_Q
    n_kblk = seq // ATT_KBLK
    per_blk = ATT_K // ATT_KBLK

    def kstart(b):
        return jnp.clip(2 * b - 1, 0, n_kblk - per_blk)

    def kind(b):
        return jnp.where(b == 0, 0, jnp.where(b == nb - 1, 2, 1))

    hps = ATT_HEADS_PER_STEP
    width = hps * NA_HEAD_DIM
    n_groups = NA_HEADS // hps

    def kv_spec(col0, t):
        return pl.BlockSpec((ATT_KBLK, width), lambda h, b: (kstart(b) + t, col0 + h))

    in_specs = [pl.BlockSpec((ATT_Q, width), lambda h, b: (b, h))]
    in_specs += [kv_spec(n_groups, t) for t in range(per_blk)]
    in_specs += [kv_spec(2 * n_groups, t) for t in range(per_blk)]
    in_specs += [pl.BlockSpec((None, hps, ATT_Q_ROWS, GRID_W, ATT_WIN), lambda h, b: (kind(b), h, 0, 0, 0))]
    return pl.pallas_call(
        functools.partial(_attn_kernel, windows=_attn_windows(seq // GRID_W), nb=nb),
        grid=(n_groups, nb),
        in_specs=in_specs,
        out_specs=pl.BlockSpec((ATT_Q, width), lambda h, b: (b, h)),
        out_shape=jax.ShapeDtypeStruct((seq, NA_WIDTH), BF16),
        scratch_shapes=[pltpu.VMEM((hps, ATT_Q, ATT_K), BF16)],
        compiler_params=_params(("parallel", "arbitrary")),
        name="na_attention",
    )(*([proj] * (1 + 2 * per_blk)), bias)


def _attention_bias(rpb, rows):
    cols = jnp.arange(GRID_W)
    col_start = jnp.clip(cols - NA_COLS // 2, 0, GRID_W - NA_COLS)
    col_ok = (cols[None, :] >= col_start[:, None]) & (cols[None, :] < col_start[:, None] + NA_COLS)
    r32 = rpb.astype(F32)
    edge = GRID_W - NA_COLS
    ext = jnp.concatenate([jnp.repeat(r32[..., :1], edge, axis=-1), r32,
                           jnp.repeat(r32[..., -1:], edge, axis=-1)], axis=-1)
    by_col = jnp.stack([ext[..., GRID_W - 1 - qc:2 * GRID_W - 1 - qc] for qc in range(GRID_W)], axis=2)
    by_col = jnp.where(col_ok[None, None], by_col, MASK_VALUE).transpose(0, 2, 1, 3)
    kinds = []
    for geo in _attn_windows(rows):
        strips = []
        for _, phase, rfirst in geo:
            strip = by_col[:, :, rfirst:rfirst + NA_ROWS, :].reshape(NA_HEADS, GRID_W, NA_ROWS * GRID_W)
            strips.append(jnp.pad(strip, ((0, 0), (0, 0), (phase, ATT_WIN - NA_ROWS * GRID_W - phase)),
                                  constant_values=MASK_VALUE))
        kinds.append(jnp.stack(strips, axis=1))
    return jnp.stack(kinds)


def _s5_tables(a_re, a_im, log_dt, b_re, b_im, c_re, c_im, d):
    t_len = S5_CHUNK
    groups = a_re.shape[1]
    n = jnp.arange(t_len + 1, dtype=F32)[None, :, None]

    def cmul(xr, xi, yr, yi):
        return xr * yr - xi * yi, xr * yi + xi * yr

    swap = lambda v: jnp.swapaxes(v, 1, 2)
    w_cols, m_rows, lag_kernels, step_r, step_i = [], [], [], [], []
    for k in range(2):
        ar, ai = a_re[k].astype(F32), a_im[k].astype(F32)
        dt = jnp.exp(log_dt[k].astype(F32))[:, None]
        zr, zi = (ar * dt)[:, None, :], (ai * dt)[:, None, :]
        mag = jnp.exp(zr * n)
        pr, pi = mag * jnp.cos(zi * n), mag * jnp.sin(zi * n)
        lr, li = pr[:, 1] - 1.0, pi[:, 1]
        den = ar * ar + ai * ai
        fr, fi = (lr * ar + li * ai) / den, (li * ar - lr * ai) / den
        bbr, bbi = cmul(fr[:, None, :], fi[:, None, :],
                        swap(b_re[k].astype(F32)), swap(b_im[k].astype(F32)))
        crt, cit = swap(c_re[k].astype(F32)), swap(c_im[k].astype(F32))
        prt, pit = swap(pr), swap(pi)
        wp_r = pr[:, :t_len][:, ::-1] if k == 0 else pr[:, :t_len]
        wp_i = pi[:, :t_len][:, ::-1] if k == 0 else pi[:, :t_len]
        wr, wi = cmul(wp_r[:, :, None, :], wp_i[:, :, None, :], bbr[:, None, :, :], bbi[:, None, :, :])
        w_cols.append((wr.reshape(groups, S5_CHUNK_WIDTH, S5_STATE), wi.reshape(groups, S5_CHUNK_WIDTH, S5_STATE)))
        ep_r = prt[:, :, 1:] if k == 0 else prt[:, :, 1:][:, :, ::-1]
        ep_i = pit[:, :, 1:] if k == 0 else pit[:, :, 1:][:, :, ::-1]
        er, ei = cmul(crt[:, :, None, :], cit[:, :, None, :], ep_r[:, :, :, None], ep_i[:, :, :, None])
        m_rows.append((er.reshape(groups, S5_STATE, S5_CHUNK_WIDTH), (-ei).reshape(groups, S5_STATE, S5_CHUNK_WIDTH)))
        xr, xi = cmul(crt[:, :, None, :], cit[:, :, None, :], prt[:, :, :t_len, None], pit[:, :, :t_len, None])
        lhs = jnp.concatenate([bbr, -bbi], axis=-1)
        rhs = jnp.concatenate([xr, xi], axis=1).reshape(groups, 2 * S5_STATE, S5_CHUNK_WIDTH)
        lag_kernels.append(jnp.einsum("gck,gkn->gcn", lhs, rhs, precision=lax.Precision.HIGHEST))
        step_r.append(pr[:, t_len])
        step_i.append(pi[:, t_len])

    w_state = jnp.concatenate([w_cols[0][0], w_cols[1][0], w_cols[0][1], w_cols[1][1]], axis=-1)
    m_state = jnp.concatenate([m_rows[0][0], m_rows[1][0], m_rows[0][1], m_rows[1][1]], axis=1)
    kf, kb = lag_kernels
    kb_rev = kb[:, :, S5_GROUP:].reshape(groups, S5_GROUP, t_len - 1, S5_GROUP)[:, :, ::-1]
    lag0 = kf[:, :, :S5_GROUP] + kb[:, :, :S5_GROUP] + d.astype(F32)[:, None, :] * jnp.eye(S5_GROUP, dtype=F32)[None]
    lags = jnp.concatenate([kb_rev.reshape(groups, S5_GROUP, -1), lag0, kf[:, :, S5_GROUP:]], axis=-1)
    toe = jnp.stack([lags[:, :, S5_GROUP * (t_len - 1 - s):S5_GROUP * (2 * t_len - 1 - s)] for s in range(t_len)],
                    axis=1).reshape(groups, S5_CHUNK_WIDTH, S5_CHUNK_WIDTH)
    m_out = jnp.concatenate([toe, m_state], axis=1)
    return (w_state.astype(BF16), m_out.astype(BF16),
            jnp.concatenate(step_r, axis=-1), jnp.concatenate(step_i, axis=-1))


def _lane_group_ids():
    return lax.broadcasted_iota(jnp.int32, (RELAYOUT_ROWS, LANES), 1) // S5_GROUP


def _transpose_lane_blocks(pieces, grp):
    nblk = S5_LANE_GROUPS
    rolled = []
    for r in range(nblk):
        acc = None
        for b in range(nblk):
            acc = pieces[b] if acc is None else jnp.where(grp == (b - r) % nblk, pieces[b], acc)
        rolled.append(pltpu.roll(acc, S5_GROUP * r, 1) if r else acc)
    out = []
    for k in range(nblk):
        acc = None
        for b in range(nblk):
            z = rolled[(b - k) % nblk]
            acc = z if acc is None else jnp.where(grp == b, z, acc)
        out.append(acc)
    return out


def _s5_state_kernel(u_ref, w_ref, ar_ref, ai_ref, ug_ref, h_ref, s_ref):
    lg = S5_LANE_GROUPS
    nc = ug_ref.shape[1]
    grp = _lane_group_ids()

    def regroup(rb, carry):
        r0 = pl.multiple_of(rb * RELAYOUT_ROWS, RELAYOUT_ROWS)
        for half in range(S5_CHUNK_WIDTH // LANES):
            steps = [u_ref[pl.ds(r0 * S5_CHUNK + half * S5_LANE_GROUPS + tp, RELAYOUT_ROWS, stride=S5_CHUNK), :]
                     for tp in range(S5_LANE_GROUPS)]
            for g, v in enumerate(_transpose_lane_blocks(steps, grp)):
                ug_ref[g, pl.ds(r0, RELAYOUT_ROWS), half * LANES:(half + 1) * LANES] = v.astype(BF16)
        return carry

    lax.fori_loop(0, nc // RELAYOUT_ROWS, regroup, 0, unroll=2)

    half = S5_STATE_WIDTH // 2
    for g in range(lg):
        sg = jnp.dot(ug_ref[g], w_ref[g], preferred_element_type=F32)
        s_ref[0, pl.ds(g, nc, stride=lg), :] = sg[:, :half]
        s_ref[1, pl.ds(g, nc, stride=lg), :] = sg[:, half:]

    lane = lax.broadcasted_iota(jnp.int32, (lg, half), 1)
    fwd = lane < S5_STATE
    ar = ar_ref[...]
    ai = ai_ref[...]
    rows = lambda i: pl.ds(pl.multiple_of(i * lg, lg), lg)

    def advance(i, hr, hi):
        sr = jnp.where(fwd, s_ref[0, rows(i), :], s_ref[0, rows(nc - 1 - i), :])
        si = jnp.where(fwd, s_ref[1, rows(i), :], s_ref[1, rows(nc - 1 - i), :])
        return ar * hr - ai * hi + sr, ar * hi + ai * hr + si

    def first_touch(i, carry):
        hr, hi = carry
        for c in (i, nc - 1 - i):
            h_ref[0, rows(c), :] = hr
            h_ref[1, rows(c), :] = hi
        return advance(i, hr, hi)

    def second_touch(i, carry):
        hr, hi = carry
        j = nc - 1 - i
        h_ref[0, rows(i), :] = jnp.where(fwd, hr, h_ref[0, rows(i), :])
        h_ref[1, rows(i), :] = jnp.where(fwd, hi, h_ref[1, rows(i), :])
        h_ref[0, rows(j), :] = jnp.where(fwd, h_ref[0, rows(j), :], hr)
        h_ref[1, rows(j), :] = jnp.where(fwd, h_ref[1, rows(j), :], hi)
        return advance(i, hr, hi)

    zero = jnp.zeros((lg, half), F32)
    carry = lax.fori_loop(0, nc // 2, first_touch, (zero, zero))
    lax.fori_loop(nc // 2, nc, second_touch, carry)


def _s5_states(u, w_state, a_re, a_im):
    seq, width = u.shape
    g = width // S5_GROUP
    nc = seq // S5_CHUNK
    lg = S5_LANE_GROUPS
    half = S5_STATE_WIDTH // 2
    return pl.pallas_call(
        _s5_state_kernel,
        grid=(width // LANES,),
        in_specs=[pl.BlockSpec((seq, LANES), lambda j: (0, j)),
                  pl.BlockSpec((lg, S5_CHUNK_WIDTH, S5_STATE_WIDTH), lambda j: (j, 0, 0)),
                  pl.BlockSpec((lg, half), lambda j: (j, 0)),
                  pl.BlockSpec((lg, half), lambda j: (j, 0))],
        out_specs=[pl.BlockSpec((lg, nc, S5_CHUNK_WIDTH), lambda j: (j, 0, 0)),
                   pl.BlockSpec((None, 2, nc * lg, half), lambda j: (j, 0, 0, 0))],
        out_shape=[jax.ShapeDtypeStruct((g, nc, S5_CHUNK_WIDTH), BF16),
                   jax.ShapeDtypeStruct((g // lg, 2, nc * lg, half), F32)],
        scratch_shapes=[pltpu.VMEM((2, nc * lg, half), F32)],
        compiler_params=_params(("parallel",)),
        name="s5_states",
    )(u, w_state, a_re, a_im)


def _s5_out_kernel(ug_ref, h_ref, m_ref, o_ref, y_ref):
    cw = S5_CHUNK_WIDTH
    nch = ug_ref.shape[1]
    for g in range(S5_LANE_GROUPS):
        hg = jnp.concatenate([h_ref[part, pl.ds(g, nch, stride=S5_LANE_GROUPS), :] for part in range(2)],
                             axis=1).astype(BF16)
        y = jnp.dot(ug_ref[g], m_ref[g, :cw, :], preferred_element_type=F32)
        y = y + jnp.dot(hg, m_ref[g, cw:, :], preferred_element_type=F32)
        y_ref[g] = jax.nn.gelu(y, approximate=True)
    grp = _lane_group_ids()

    def regroup(rb, carry):
        r0 = pl.multiple_of(rb * RELAYOUT_ROWS, RELAYOUT_ROWS)
        for half in range(cw // LANES):
            groups = [y_ref[g, pl.ds(r0, RELAYOUT_ROWS), half * LANES:(half + 1) * LANES]
                      for g in range(S5_LANE_GROUPS)]
            for tp, v in enumerate(_transpose_lane_blocks(groups, grp)):
                t = half * S5_LANE_GROUPS + tp
                o_ref[pl.ds(r0 * S5_CHUNK + t, RELAYOUT_ROWS, stride=S5_CHUNK), :] = v
        return carry

    lax.fori_loop(0, nch // RELAYOUT_ROWS, regroup, 0, unroll=2)


def _s5_outputs(ug, h3, m_out, nsplit):
    g, nc, cw = ug.shape
    nch = nc // nsplit
    lg = S5_LANE_GROUPS
    return pl.pallas_call(
        _s5_out_kernel,
        grid=(g // lg, nsplit),
        in_specs=[pl.BlockSpec((lg, nch, cw), lambda j, h: (j, h, 0)),
                  pl.BlockSpec((None, 2, nch * lg, S5_STATE_WIDTH // 2), lambda j, h: (j, 0, h, 0)),
                  pl.BlockSpec((lg, cw + S5_STATE_WIDTH, cw), lambda j, h: (j, 0, 0))],
        out_specs=pl.BlockSpec((nch * S5_CHUNK, LANES), lambda j, h: (h, j)),
        out_shape=jax.ShapeDtypeStruct((nc * S5_CHUNK, g * S5_GROUP), F32),
        scratch_shapes=[pltpu.VMEM((lg, nch, cw), F32)],
        compiler_params=_params(("parallel", "parallel")),
        name="s5_outputs",
    )(ug, h3, m_out)


def _merge_kernel(a1_ref, w1_ref, ys_ref, wg_ref, bg_ref, w2_ref, g1_ref, g2_ref, o_ref, ysb_ref, z_ref, *, tg):
    @pl.when(pl.program_id(1) == 0)
    def _():
        ysb_ref[...] = ys_ref[...].astype(BF16)
        for c in range(z_ref.shape[1] // tg):
            cols = slice(c * tg, (c + 1) * tg)
            t = jnp.dot(ysb_ref[...], wg_ref[:, cols], preferred_element_type=F32) + bg_ref[:, cols]
            z_ref[:, cols] = (ys_ref[:, cols] * jax.nn.sigmoid(t)).astype(BF16)

    y1 = jnp.dot(a1_ref[...], w1_ref[...], preferred_element_type=F32)
    y2 = jnp.dot(z_ref[...], w2_ref[...], preferred_element_type=F32)
    o_ref[...] = (g1_ref[...].astype(F32) * y1 + g2_ref[...].astype(F32) * y2).astype(o_ref.dtype)


def _glu_merge(att, w_na, ys, w_glu, b_glu, w_s5, proj, gate_col0, tm, tn):
    m, k1 = att.shape
    k2 = ys.shape[1]
    n = w_na.shape[1]
    g1 = gate_col0 // tn
    g2 = (gate_col0 + n) // tn
    return pl.pallas_call(
        functools.partial(_merge_kernel, tg=tn),
        grid=(m // tm, n // tn),
        in_specs=[pl.BlockSpec((tm, k1), lambda i, j: (i, 0)),
                  pl.BlockSpec((k1, tn), lambda i, j: (0, j)),
                  pl.BlockSpec((tm, k2), lambda i, j: (i, 0)),
                  pl.BlockSpec((k2, k2), lambda i, j: (0, 0)),
                  pl.BlockSpec((1, k2), lambda i, j: (0, 0)),
                  pl.BlockSpec((k2, tn), lambda i, j: (0, j)),
                  pl.BlockSpec((tm, tn), lambda i, j: (i, g1 + j)),
                  pl.BlockSpec((tm, tn), lambda i, j: (i, g2 + j))],
        out_specs=pl.BlockSpec((tm, tn), lambda i, j: (i, j)),
        out_shape=jax.ShapeDtypeStruct((m, n), BF16),
        scratch_shapes=[pltpu.VMEM((tm, k2), BF16), pltpu.VMEM((tm, k2), BF16)],
        compiler_params=_params(("parallel", "arbitrary")),
        name="glu_gated_merge",
    )(att, w_na, ys, w_glu, b_glu, w_s5, proj, proj)


def _matmul_residual_ln(a_ref, w_ref, res_ref, g_ref, b_ref, o_ref, alpha):
    half = a_ref.shape[0] // LN_ROW_SPLIT
    for r in range(LN_ROW_SPLIT):
        rows = slice(r * half, (r + 1) * half)
        y = jnp.dot(a_ref[rows, :], w_ref[...], preferred_element_type=F32)
        o_ref[rows, :] = _layer_norm_rows(alpha * res_ref[rows, :] + y, g_ref[...], b_ref[...])


def _outproj_ln_kernel(a_ref, w_ref, x_ref, g_ref, b_ref, h_ref, *, alpha):
    _matmul_residual_ln(a_ref, w_ref, x_ref, g_ref, b_ref, h_ref, alpha)


def _outproj_ln(a, w, x, g, b, alpha, tm):
    m, k = a.shape
    n = w.shape[1]
    return pl.pallas_call(
        functools.partial(_outproj_ln_kernel, alpha=alpha),
        grid=(m // tm,),
        in_specs=[pl.BlockSpec((tm, k), lambda i: (i, 0)),
                  pl.BlockSpec((k, n), lambda i: (0, 0)),
                  pl.BlockSpec((tm, n), lambda i: (i, 0)),
                  pl.BlockSpec((1, n), lambda i: (0, 0)),
                  pl.BlockSpec((1, n), lambda i: (0, 0))],
        out_specs=pl.BlockSpec((tm, n), lambda i: (i, 0)),
        out_shape=jax.ShapeDtypeStruct((m, n), F32),
        compiler_params=_params(("parallel",)),
        name="outproj_ln1",
    )(a, w, x, g, b)


def _ffn_act_kernel(h_ref, wg_ref, wu_ref, o_ref, hb_ref):
    @pl.when(pl.program_id(1) == 0)
    def _():
        hb_ref[...] = h_ref[...].astype(BF16)

    hb = hb_ref[...]
    gate = jnp.dot(hb, wg_ref[...], preferred_element_type=F32)
    up = jnp.dot(hb, wu_ref[...], preferred_element_type=F32)
    o_ref[...] = (jax.nn.silu(gate) * up).astype(o_ref.dtype)


def _ffn_act(h, wg, wu, tm, tf):
    m, d = h.shape
    f = wg.shape[1]
    return pl.pallas_call(
        _ffn_act_kernel,
        grid=(m // tm, f // tf),
        in_specs=[pl.BlockSpec((tm, d), lambda i, j: (i, 0)),
                  pl.BlockSpec((d, tf), lambda i, j: (0, j)),
                  pl.BlockSpec((d, tf), lambda i, j: (0, j))],
        out_specs=pl.BlockSpec((tm, tf), lambda i, j: (i, j)),
        out_shape=jax.ShapeDtypeStruct((m, f), BF16),
        scratch_shapes=[pltpu.VMEM((tm, d), BF16)],
        compiler_params=_params(("parallel", "arbitrary")),
        name="swiglu_act",
    )(h, wg, wu)


def _ffn_down_ln_kernel(a_ref, w_ref, h_ref, g_ref, b_ref, o_ref, *, alpha):
    _matmul_residual_ln(a_ref, w_ref, h_ref, g_ref, b_ref, o_ref, alpha)


def _ffn_down_ln(act, wd, h, g, b, alpha, tm):
    m, f = act.shape
    d = wd.shape[1]
    return pl.pallas_call(
        functools.partial(_ffn_down_ln_kernel, alpha=alpha),
        grid=(m // tm,),
        in_specs=[pl.BlockSpec((tm, f), lambda i: (i, 0)),
                  pl.BlockSpec((f, d), lambda i: (0, 0), pipeline_mode=pl.Buffered(1)),
                  pl.BlockSpec((tm, d), lambda i: (i, 0)),
                  pl.BlockSpec((1, d), lambda i: (0, 0)),
                  pl.BlockSpec((1, d), lambda i: (0, 0))],
        out_specs=pl.BlockSpec((tm, d), lambda i: (i, 0)),
        out_shape=jax.ShapeDtypeStruct((m, d), F32),
        compiler_params=_params(("parallel",)),
        name="swiglu_down_ln2",
    )(act, wd, h, g, b)


def _layer(x2, w_in, b_gate, na_rpb, w_na_out, s5_a_re, s5_a_im, s5_log_dt, s5_b_re, s5_b_im,
           s5_c_re, s5_c_im, s5_d, w_glu, b_glu, w_s5_out, w_out, ln1_g, ln1_b,
           w_ffn_gate, w_ffn_up, w_ffn_down, ln2_g, ln2_b, alpha):
    seq, d_model = x2.shape
    rows = seq // GRID_W
    nc = seq // S5_CHUNK
    gate_col0 = 3 * NA_WIDTH + S5_WIDTH
    in_cols = gate_col0 + 2 * d_model
    row = lambda v: v.astype(F32).reshape(1, -1)

    col_scale = jnp.where(jnp.arange(in_cols) < NA_WIDTH, NA_HEAD_DIM ** -0.5, 1.0).astype(F32).reshape(1, -1)
    col_bias = jnp.concatenate([jnp.zeros((gate_col0,), F32), b_gate.astype(F32)]).reshape(1, -1)
    proj, u = _inproj(x2, w_in.astype(BF16), col_scale, col_bias, 3 * NA_WIDTH, gate_col0, tm=1024, tn=1024)

    att = _attention(proj, _attention_bias(na_rpb, rows), seq)

    w_state, m_out, a_re, a_im = _s5_tables(s5_a_re, s5_a_im, s5_log_dt, s5_b_re, s5_b_im,
                                            s5_c_re, s5_c_im, s5_d)
    ug, h_states = _s5_states(u, w_state, a_re, a_im)
    ys = _s5_outputs(ug, h_states, m_out, nsplit=2)

    merged = _glu_merge(att, w_na_out.astype(BF16), ys, w_glu.astype(BF16), row(b_glu), w_s5_out.astype(BF16),
                        proj, gate_col0, tm=1024, tn=512)
    h = _outproj_ln(merged, w_out.astype(BF16), x2, row(ln1_g), row(ln1_b), alpha, tm=512)
    act = _ffn_act(h, w_ffn_gate.astype(BF16), w_ffn_up.astype(BF16), tm=1024, tf=512)
    return _ffn_down_ln(act, w_ffn_down.astype(BF16), h, row(ln2_g), row(ln2_b), alpha, tm=512)


def kernel(x, w_in, b_gate, na_rpb, w_na_out, s5_a_re, s5_a_im, s5_log_dt, s5_b_re, s5_b_im, s5_c_re, s5_c_im, s5_d, w_glu, b_glu, w_s5_out, w_out, ln1_g, ln1_b, w_ffn_gate, w_ffn_up, w_ffn_down, ln2_g, ln2_b):
    bsz, seq, d_model = x.shape
    depth = w_in.shape[0]
    alpha = (2.0 * depth) ** 0.25
    outs = []
    for bi in range(bsz):
        xb = x[bi]
        for l in range(depth):
            xb = _layer(xb, w_in[l], b_gate[l], na_rpb[l], w_na_out[l], s5_a_re[l], s5_a_im[l], s5_log_dt[l],
                        s5_b_re[l], s5_b_im[l], s5_c_re[l], s5_c_im[l], s5_d[l], w_glu[l], b_glu[l],
                        w_s5_out[l], w_out[l], ln1_g[l], ln1_b[l], w_ffn_gate[l], w_ffn_up[l],
                        w_ffn_down[l], ln2_g[l], ln2_b[l], alpha)
        outs.append(xb)
    return jnp.stack(outs)
```

```python
import functools
import math

import jax
import jax.numpy as jnp
from jax import lax
from jax.experimental import pallas as pl
from jax.experimental.pallas import tpu as pltpu

F32 = jnp.float32
BF16 = jnp.bfloat16

GRID_W = 64
NA_HEADS = 8
NA_HEAD_DIM = 128
NA_WIDTH = NA_HEADS * NA_HEAD_DIM
NA_ROWS = 8
NA_COLS = 16
S5_GROUP = 16
S5_GROUPS = 64
S5_WIDTH = S5_GROUP * S5_GROUPS
S5_STATE = 64
LN_EPS = 1e-5
MASK_VALUE = -1e30

S5_CHUNK = 16
S5_CHUNK_WIDTH = S5_CHUNK * S5_GROUP
S5_STATE_WIDTH = 4 * S5_STATE
SCAN_GROUPS = 8

ATT_Q_ROWS = 8
ATT_K_ROWS = 16
ATT_Q = ATT_Q_ROWS * GRID_W
ATT_K = ATT_K_ROWS * GRID_W
ATT_KBLK = 256
LANES = 128
ATT_WIN = NA_ROWS * GRID_W + LANES
ATT_HEADS_PER_STEP = 4
ATT_ROW_SPLIT = 2
LN_ROW_SPLIT = 4
S5_LANE_GROUPS = LANES // S5_GROUP
RELAYOUT_ROWS = 16

VMEM_LIMIT = 56 * 1024 * 1024


def _params(sem):
    return pltpu.CompilerParams(dimension_semantics=sem, vmem_limit_bytes=VMEM_LIMIT)


def _layer_norm_rows(y, g, b):
    mu = jnp.mean(y, axis=-1, keepdims=True)
    d = y - mu
    var = jnp.mean(d * d, axis=-1, keepdims=True)
    return d * lax.rsqrt(var + LN_EPS) * g + b


def _inproj_kernel(x_ref, w_ref, s_ref, b_ref, o_ref, u_ref, xb_ref, *, u_tile0, gate_tile0):
    j = pl.program_id(1)

    @pl.when(j == 0)
    def _():
        xb_ref[...] = x_ref[...].astype(BF16)

    @pl.when(j < u_tile0)
    def _():
        acc = jnp.dot(xb_ref[...], w_ref[...], preferred_element_type=F32)
        o_ref[...] = (acc * s_ref[...]).astype(o_ref.dtype)

    @pl.when((j >= u_tile0) & (j < gate_tile0))
    def _():
        acc = jnp.dot(xb_ref[...], w_ref[...], preferred_element_type=F32)
        o_ref[...] = acc.astype(o_ref.dtype)
        u_ref[...] = acc

    @pl.when(j >= gate_tile0)
    def _():
        acc = jnp.dot(xb_ref[...], w_ref[...], preferred_element_type=F32)
        o_ref[...] = jax.nn.sigmoid(acc + b_ref[...]).astype(o_ref.dtype)


def _inproj(x, w, col_scale, col_bias, u_col0, gate_col0, tm, tn):
    m, k = x.shape
    n = w.shape[1]
    u_tile0, gate_tile0 = u_col0 // tn, gate_col0 // tn
    return pl.pallas_call(
        functools.partial(_inproj_kernel, u_tile0=u_tile0, gate_tile0=gate_tile0),
        grid=(m // tm, n // tn),
        in_specs=[
            pl.BlockSpec((tm, k), lambda i, j: (i, 0)),
            pl.BlockSpec((k, tn), lambda i, j: (0, j)),
            pl.BlockSpec((1, tn), lambda i, j: (0, j)),
            pl.BlockSpec((1, tn), lambda i, j: (0, j)),
        ],
        out_specs=[pl.BlockSpec((tm, tn), lambda i, j: (i, j)),
                   pl.BlockSpec((tm, tn), lambda i, j: (i, jnp.clip(j - u_tile0, 0, gate_tile0 - u_tile0 - 1)))],
        out_shape=[jax.ShapeDtypeStruct((m, n), BF16),
                   jax.ShapeDtypeStruct((m, gate_col0 - u_col0), F32)],
        scratch_shapes=[pltpu.VMEM((tm, k), BF16)],
        compiler_params=_params(("parallel", "arbitrary")),
        name="inproj",
    )(x, w, col_scale, col_bias)


def _attn_windows(rows):
    nb = rows // ATT_Q_ROWS
    kinds = []
    for b in (0, 1, nb - 1):
        base = min(max(b * ATT_Q_ROWS - NA_ROWS // 2, 0), rows - ATT_K_ROWS)
        geo = []
        for ql in range(ATT_Q_ROWS):
            qr = b * ATT_Q_ROWS + ql
            r0 = min(max(qr - NA_ROWS // 2, 0), rows - NA_ROWS)
            koff = r0 - base
            lane0 = min(LANES * (koff * GRID_W // LANES), ATT_K - ATT_WIN)
            geo.append((lane0, koff * GRID_W - lane0, r0 - qr + NA_ROWS - 1))
        kinds.append(tuple(geo))
    return tuple(kinds)


def _attn_kernel(q_ref, k0, k1, k2, k3, v0, v1, v2, v3, bias_ref, o_ref, p_ref, *, windows, nb):
    b = pl.program_id(1)
    k_refs, v_refs = (k0, k1, k2, k3), (v0, v1, v2, v3)

    def one_head(hh, geo):
        c = slice(hh * NA_HEAD_DIM, (hh + 1) * NA_HEAD_DIM)
        k = jnp.concatenate([r[:, c] for r in k_refs], axis=0)
        v = jnp.concatenate([r[:, c] for r in v_refs], axis=0)
        per = ATT_Q_ROWS // ATT_ROW_SPLIT
        for part in range(ATT_ROW_SPLIT):
            sub = geo[part * per:(part + 1) * per]
            lo = min(lane0 for lane0, _, _ in sub)
            hi = max(lane0 for lane0, _, _ in sub) + ATT_WIN
            rows = slice(part * per * GRID_W, (part + 1) * per * GRID_W)
            s = lax.dot_general(q_ref[rows, c], k[lo:hi], (((1,), (1,)), ((), ())), preferred_element_type=F32)
            inv = []
            for i, (lane0, _, _) in enumerate(sub):
                ql = part * per + i
                r = slice(ql * GRID_W, (ql + 1) * GRID_W)
                sw = s[i * GRID_W:(i + 1) * GRID_W, lane0 - lo:lane0 - lo + ATT_WIN] + bias_ref[hh, ql]
                m = jnp.max(sw, axis=-1, keepdims=True)
                p = jnp.exp(sw - m)
                inv.append(1.0 / jnp.sum(p, axis=-1, keepdims=True))
                if lane0 > lo:
                    p_ref[hh, r, lo:lane0] = jnp.zeros((GRID_W, lane0 - lo), BF16)
                p_ref[hh, r, lane0:lane0 + ATT_WIN] = p.astype(BF16)
                if lane0 + ATT_WIN < hi:
                    p_ref[hh, r, lane0 + ATT_WIN:hi] = jnp.zeros((GRID_W, hi - lane0 - ATT_WIN), BF16)
            o = jnp.dot(p_ref[hh, rows, lo:hi], v[lo:hi], preferred_element_type=F32)
            o_ref[rows, c] = (o * jnp.concatenate(inv, axis=0)).astype(o_ref.dtype)

    def block_kind(geo):
        for hh in range(ATT_HEADS_PER_STEP):
            one_head(hh, geo)

    @pl.when(b == 0)
    def _():
        block_kind(windows[0])

    @pl.when((b > 0) & (b < nb - 1))
    def _():
        block_kind(windows[1])

    @pl.when(b == nb - 1)
    def _():
        block_kind(windows[2])


def _attention(proj, bias, seq):
    nb = seq // ATT_Q
    n_kblk = seq // ATT_KBLK
    per_blk = ATT_K // ATT_KBLK

    def kstart(b):
        return jnp.clip(2 * b - 1, 0, n_kblk - per_blk)

    def kind(b):
        return jnp.where(b == 0, 0, jnp.where(b == nb - 1, 2, 1))

    hps = ATT_HEADS_PER_STEP
    width = hps * NA_HEAD_DIM
    n_groups = NA_HEADS // hps

    def kv_spec(col0, t):
        return pl.BlockSpec((ATT_KBLK, width), lambda h, b: (kstart(b) + t, col0 + h))

    in_specs = [pl.BlockSpec((ATT_Q, width), lambda h, b: (b, h))]
    in_specs += [kv_spec(n_groups, t) for t in range(per_blk)]
    in_specs += [kv_spec(2 * n_groups, t) for t in range(per_blk)]
    in_specs += [pl.BlockSpec((None, hps, ATT_Q_ROWS, GRID_W, ATT_WIN), lambda h, b: (kind(b), h, 0, 0, 0))]
    return pl.pallas_call(
        functools.partial(_attn_kernel, windows=_attn_windows(seq // GRID_W), nb=nb),
        grid=(n_groups, nb),
        in_specs=in_specs,
        out_specs=pl.BlockSpec((ATT_Q, width), lambda h, b: (b, h)),
        out_shape=jax.ShapeDtypeStruct((seq, NA_WIDTH), BF16),
        scratch_shapes=[pltpu.VMEM((hps, ATT_Q, ATT_K), BF16)],
        compiler_params=_params(("parallel", "arbitrary")),
        name="na_attention",
    )(*([proj] * (1 + 2 * per_blk)), bias)


def _attention_bias(rpb, rows):
    cols = jnp.arange(GRID_W)
    col_start = jnp.clip(cols - NA_COLS // 2, 0, GRID_W - NA_COLS)
    col_ok = (cols[None, :] >= col_start[:, None]) & (cols[None, :] < col_start[:, None] + NA_COLS)
    r32 = rpb.astype(F32)
    edge = GRID_W - NA_COLS
    ext = jnp.concatenate([jnp.repeat(r32[..., :1], edge, axis=-1), r32,
                           jnp.repeat(r32[..., -1:], edge, axis=-1)], axis=-1)
    by_col = jnp.stack([ext[..., GRID_W - 1 - qc:2 * GRID_W - 1 - qc] for qc in range(GRID_W)], axis=2)
    by_col = jnp.where(col_ok[None, None], by_col, MASK_VALUE).transpose(0, 2, 1, 3)
    kinds = []
    for geo in _attn_windows(rows):
        strips = []
        for _, phase, rfirst in geo:
            strip = by_col[:, :, rfirst:rfirst + NA_ROWS, :].reshape(NA_HEADS, GRID_W, NA_ROWS * GRID_W)
            strips.append(jnp.pad(strip, ((0, 0), (0, 0), (phase, ATT_WIN - NA_ROWS * GRID_W - phase)),
                                  constant_values=MASK_VALUE))
        kinds.append(jnp.stack(strips, axis=1))
    return jnp.stack(kinds)


def _s5_tables(a_re, a_im, log_dt, b_re, b_im, c_re, c_im, d):
    t_len = S5_CHUNK
    groups = a_re.shape[1]
    steps = jnp.arange(t_len, dtype=F32)
    flip = t_len - 1 - steps

    def cmul(xr, xi, yr, yi):
        return xr * yr - xi * yi, xr * yi + xi * yr

    def powers(zr, zi, n):
        mag = jnp.exp(zr * n)
        return mag * jnp.cos(zi * n), mag * jnp.sin(zi * n)

    ar, ai = a_re.astype(F32), a_im.astype(F32)
    dt = jnp.exp(log_dt.astype(F32))[..., None]
    zr, zi = ar * dt, ai * dt
    lr, li = powers(zr, zi, 1.0)
    lr = lr - 1.0
    den = ar * ar + ai * ai
    fr, fi = (lr * ar + li * ai) / den, (li * ar - lr * ai) / den
    swap = lambda v: jnp.swapaxes(v.astype(F32), 2, 3)
    bbr, bbi = cmul(fr[:, :, None, :], fi[:, :, None, :], swap(b_re), swap(b_im))
    crt, cit = swap(c_re), swap(c_im)
    n_w = jnp.stack([flip, steps])[:, None, :, None]
    wp_r, wp_i = powers(zr[:, :, None, :], zi[:, :, None, :], n_w)
    wr, wi = cmul(wp_r[:, :, :, None, :], wp_i[:, :, :, None, :], bbr[:, :, None], bbi[:, :, None])
    wr = wr.reshape(2, groups, S5_CHUNK_WIDTH, S5_STATE)
    wi = wi.reshape(2, groups, S5_CHUNK_WIDTH, S5_STATE)
    n_e = jnp.stack([steps + 1.0, flip + 1.0])[:, None, None, :]
    ep_r, ep_i = powers(zr[..., None], zi[..., None], n_e)
    er, ei = cmul(crt[:, :, :, None, :], cit[:, :, :, None, :], ep_r[..., None], ep_i[..., None])
    er = er.reshape(2, groups, S5_STATE, S5_CHUNK_WIDTH)
    ei = ei.reshape(2, groups, S5_STATE, S5_CHUNK_WIDTH)
    lp_r, lp_i = powers(zr[..., None], zi[..., None], steps)
    xr, xi = cmul(crt[:, :, :, None, :], cit[:, :, :, None, :], lp_r[..., None], lp_i[..., None])
    lhs = jnp.concatenate([bbr, -bbi], axis=-1)
    rhs = jnp.concatenate([xr, xi], axis=2).reshape(2, groups, 2 * S5_STATE, S5_CHUNK_WIDTH)
    lag_kernels = jnp.einsum("dgck,dgkn->dgcn", lhs, rhs, precision=lax.Precision.HIGHEST)
    step_r, step_i = powers(zr, zi, float(t_len))

    w_state = jnp.concatenate([wr[0], wr[1], wi[0], wi[1]], axis=-1)
    m_state = jnp.concatenate([er[0], er[1], -ei[0], -ei[1]], axis=1)
    kf, kb = lag_kernels[0], lag_kernels[1]
    kb_rev = kb[:, :, S5_GROUP:].reshape(groups, S5_GROUP, t_len - 1, S5_GROUP)[:, :, ::-1]
    lag0 = kf[:, :, :S5_GROUP] + kb[:, :, :S5_GROUP] + d.astype(F32)[:, None, :] * jnp.eye(S5_GROUP, dtype=F32)[None]
    lags = jnp.concatenate([kb_rev.reshape(groups, S5_GROUP, -1), lag0, kf[:, :, S5_GROUP:]], axis=-1)
    toe = jnp.stack([lags[:, :, S5_GROUP * (t_len - 1 - s):S5_GROUP * (2 * t_len - 1 - s)] for s in range(t_len)],
                    axis=1).reshape(groups, S5_CHUNK_WIDTH, S5_CHUNK_WIDTH)
    m_out = jnp.concatenate([toe, m_state], axis=1)
    return (w_state.astype(BF16), m_out.astype(BF16),
            jnp.concatenate([step_r[0], step_r[1]], axis=-1), jnp.concatenate([step_i[0], step_i[1]], axis=-1))


def _lane_group_ids():
    return lax.broadcasted_iota(jnp.int32, (RELAYOUT_ROWS, LANES), 1) // S5_GROUP


def _transpose_lane_blocks(pieces, grp):
    nblk = S5_LANE_GROUPS
    rolled = []
    for r in range(nblk):
        acc = None
        for b in range(nblk):
            acc = pieces[b] if acc is None else jnp.where(grp == (b - r) % nblk, pieces[b], acc)
        rolled.append(pltpu.roll(acc, S5_GROUP * r, 1) if r else acc)
    out = []
    for k in range(nblk):
        acc = None
        for b in range(nblk):
            z = rolled[(b - k) % nblk]
            acc = z if acc is None else jnp.where(grp == b, z, acc)
        out.append(acc)
    return out


def _s5_state_kernel(u_ref, w_ref, ar_ref, ai_ref, ug_ref, h_ref, s_ref):
    lg = S5_LANE_GROUPS
    nc = ug_ref.shape[1]
    grp = _lane_group_ids()

    def regroup(rb, carry):
        r0 = pl.multiple_of(rb * RELAYOUT_ROWS, RELAYOUT_ROWS)
        for half in range(S5_CHUNK_WIDTH // LANES):
            steps = [u_ref[pl.ds(r0 * S5_CHUNK + half * S5_LANE_GROUPS + tp, RELAYOUT_ROWS, stride=S5_CHUNK), :]
                     for tp in range(S5_LANE_GROUPS)]
            for g, v in enumerate(_transpose_lane_blocks(steps, grp)):
                ug_ref[g, pl.ds(r0, RELAYOUT_ROWS), half * LANES:(half + 1) * LANES] = v.astype(BF16)
        return carry

    lax.fori_loop(0, nc // RELAYOUT_ROWS, regroup, 0, unroll=2)

    half = S5_STATE_WIDTH // 2
    for g in range(lg):
        sg = jnp.dot(ug_ref[g], w_ref[g], preferred_element_type=F32)
        s_ref[0, pl.ds(g, nc, stride=lg), :] = sg[:, :half]
        s_ref[1, pl.ds(g, nc, stride=lg), :] = sg[:, half:]

    lane = lax.broadcasted_iota(jnp.int32, (lg, half), 1)
    fwd = lane < S5_STATE
    ar = ar_ref[...]
    ai = ai_ref[...]
    rows = lambda i: pl.ds(pl.multiple_of(i * lg, lg), lg)

    def advance(i, hr, hi):
        sr = jnp.where(fwd, s_ref[0, rows(i), :], s_ref[0, rows(nc - 1 - i), :])
        si = jnp.where(fwd, s_ref[1, rows(i), :], s_ref[1, rows(nc - 1 - i), :])
        return ar * hr - ai * hi + sr, ar * hi + ai * hr + si

    def first_touch(i, carry):
        hr, hi = carry
        for c in (i, nc - 1 - i):
            h_ref[0, rows(c), :] = hr
            h_ref[1, rows(c), :] = hi
        return advance(i, hr, hi)

    def second_touch(i, carry):
        hr, hi = carry
        j = nc - 1 - i
        h_ref[0, rows(i), :] = jnp.where(fwd, hr, h_ref[0, rows(i), :])
        h_ref[1, rows(i), :] = jnp.where(fwd, hi, h_ref[1, rows(i), :])
        h_ref[0, rows(j), :] = jnp.where(fwd, h_ref[0, rows(j), :], hr)
        h_ref[1, rows(j), :] = jnp.where(fwd, h_ref[1, rows(j), :], hi)
        return advance(i, hr, hi)

    zero = jnp.zeros((lg, half), F32)
    carry = lax.fori_loop(0, nc // 2, first_touch, (zero, zero))
    lax.fori_loop(nc // 2, nc, second_touch, carry)


def _s5_states(u, w_state, a_re, a_im):
    seq, width = u.shape
    g = width // S5_GROUP
    nc = seq // S5_CHUNK
    lg = S5_LANE_GROUPS
    half = S5_STATE_WIDTH // 2
    return pl.pallas_call(
        _s5_state_kernel,
        grid=(width // LANES,),
        in_specs=[pl.BlockSpec((seq, LANES), lambda j: (0, j)),
                  pl.BlockSpec((lg, S5_CHUNK_WIDTH, S5_STATE_WIDTH), lambda j: (j, 0, 0)),
                  pl.BlockSpec((lg, half), lambda j: (j, 0)),
                  pl.BlockSpec((lg, half), lambda j: (j, 0))],
        out_specs=[pl.BlockSpec((lg, nc, S5_CHUNK_WIDTH), lambda j: (j, 0, 0)),
                   pl.BlockSpec((None, 2, nc * lg, half), lambda j: (j, 0, 0, 0))],
        out_shape=[jax.ShapeDtypeStruct((g, nc, S5_CHUNK_WIDTH), BF16),
                   jax.ShapeDtypeStruct((g // lg, 2, nc * lg, half), F32)],
        scratch_shapes=[pltpu.VMEM((2, nc * lg, half), F32)],
        compiler_params=_params(("parallel",)),
        name="s5_states",
    )(u, w_state, a_re, a_im)


def _s5_out_kernel(ug_ref, h_ref, m_ref, o_ref, y_ref):
    cw = S5_CHUNK_WIDTH
    nch = ug_ref.shape[1]
    for g in range(S5_LANE_GROUPS):
        hg = jnp.concatenate([h_ref[part, pl.ds(g, nch, stride=S5_LANE_GROUPS), :] for part in range(2)],
                             axis=1).astype(BF16)
        y = jnp.dot(ug_ref[g], m_ref[g, :cw, :], preferred_element_type=F32)
        y = y + jnp.dot(hg, m_ref[g, cw:, :], preferred_element_type=F32)
        y_ref[g] = jax.nn.gelu(y, approximate=True)
    grp = _lane_group_ids()

    def regroup(rb, carry):
        r0 = pl.multiple_of(rb * RELAYOUT_ROWS, RELAYOUT_ROWS)
        for half in range(cw // LANES):
            groups = [y_ref[g, pl.ds(r0, RELAYOUT_ROWS), half * LANES:(half + 1) * LANES]
                      for g in range(S5_LANE_GROUPS)]
            for tp, v in enumerate(_transpose_lane_blocks(groups, grp)):
                t = half * S5_LANE_GROUPS + tp
                o_ref[pl.ds(r0 * S5_CHUNK + t, RELAYOUT_ROWS, stride=S5_CHUNK), :] = v
        return carry

    lax.fori_loop(0, nch // RELAYOUT_ROWS, regroup, 0, unroll=2)


def _s5_outputs(ug, h3, m_out, nsplit):
    g, nc, cw = ug.shape
    nch = nc // nsplit
    lg = S5_LANE_GROUPS
    return pl.pallas_call(
        _s5_out_kernel,
        grid=(g // lg, nsplit),
        in_specs=[pl.BlockSpec((lg, nch, cw), lambda j, h: (j, h, 0)),
                  pl.BlockSpec((None, 2, nch * lg, S5_STATE_WIDTH // 2), lambda j, h: (j, 0, h, 0)),
                  pl.BlockSpec((lg, cw + S5_STATE_WIDTH, cw), lambda j, h: (j, 0, 0))],
        out_specs=pl.BlockSpec((nch * S5_CHUNK, LANES), lambda j, h: (h, j)),
        out_shape=jax.ShapeDtypeStruct((nc * S5_CHUNK, g * S5_GROUP), F32),
        scratch_shapes=[pltpu.VMEM((lg, nch, cw), F32)],
        compiler_params=_params(("parallel", "parallel")),
        name="s5_outputs",
    )(ug, h3, m_out)


def _merge_kernel(a1_ref, w1_ref, ys_ref, wg_ref, bg_ref, w2_ref, g1_ref, g2_ref, o_ref, ysb_ref, z_ref, *, tg):
    @pl.when(pl.program_id(1) == 0)
    def _():
        ysb_ref[...] = ys_ref[...].astype(BF16)
        for c in range(z_ref.shape[1] // tg):
            cols = slice(c * tg, (c + 1) * tg)
            t = jnp.dot(ysb_ref[...], wg_ref[:, cols], preferred_element_type=F32) + bg_ref[:, cols]
            z_ref[:, cols] = (ys_ref[:, cols] * jax.nn.sigmoid(t)).astype(BF16)

    y1 = jnp.dot(a1_ref[...], w1_ref[...], preferred_element_type=F32)
    y2 = jnp.dot(z_ref[...], w2_ref[...], preferred_element_type=F32)
    o_ref[...] = (g1_ref[...].astype(F32) * y1 + g2_ref[...].astype(F32) * y2).astype(o_ref.dtype)


def _glu_merge(att, w_na, ys, w_glu, b_glu, w_s5, proj, gate_col0, tm, tn):
    m, k1 = att.shape
    k2 = ys.shape[1]
    n = w_na.shape[1]
    g1 = gate_col0 // tn
    g2 = (gate_col0 + n) // tn
    return pl.pallas_call(
        functools.partial(_merge_kernel, tg=tn),
        grid=(m // tm, n // tn),
        in_specs=[pl.BlockSpec((tm, k1), lambda i, j: (i, 0)),
                  pl.BlockSpec((k1, tn), lambda i, j: (0, j)),
                  pl.BlockSpec((tm, k2), lambda i, j: (i, 0)),
                  pl.BlockSpec((k2, k2), lambda i, j: (0, 0)),
                  pl.BlockSpec((1, k2), lambda i, j: (0, 0)),
                  pl.BlockSpec((k2, tn), lambda i, j: (0, j)),
                  pl.BlockSpec((tm, tn), lambda i, j: (i, g1 + j)),
                  pl.BlockSpec((tm, tn), lambda i, j: (i, g2 + j))],
        out_specs=pl.BlockSpec((tm, tn), lambda i, j: (i, j)),
        out_shape=jax.ShapeDtypeStruct((m, n), BF16),
        scratch_shapes=[pltpu.VMEM((tm, k2), BF16), pltpu.VMEM((tm, k2), BF16)],
        compiler_params=_params(("parallel", "arbitrary")),
        name="glu_gated_merge",
    )(att, w_na, ys, w_glu, b_glu, w_s5, proj, proj)


def _matmul_residual_ln(a_ref, w_ref, res_ref, g_ref, b_ref, o_ref, alpha, ob_ref=None):
    sub = a_ref.shape[0] // LN_ROW_SPLIT
    for r in range(LN_ROW_SPLIT):
        rows = slice(r * sub, (r + 1) * sub)
        y = jnp.dot(a_ref[rows, :], w_ref[...], preferred_element_type=F32)
        out = _layer_norm_rows(alpha * res_ref[rows, :] + y, g_ref[...], b_ref[...])
        o_ref[rows, :] = out
        if ob_ref is not None:
            ob_ref[rows, :] = out.astype(ob_ref.dtype)


def _outproj_ln_kernel(a_ref, w_ref, x_ref, g_ref, b_ref, h_ref, hb_ref, *, alpha):
    _matmul_residual_ln(a_ref, w_ref, x_ref, g_ref, b_ref, h_ref, alpha, hb_ref)


def _outproj_ln(a, w, x, g, b, alpha, tm):
    m, k = a.shape
    n = w.shape[1]
    return pl.pallas_call(
        functools.partial(_outproj_ln_kernel, alpha=alpha),
        grid=(m // tm,),
        in_specs=[pl.BlockSpec((tm, k), lambda i: (i, 0)),
                  pl.BlockSpec((k, n), lambda i: (0, 0)),
                  pl.BlockSpec((tm, n), lambda i: (i, 0)),
                  pl.BlockSpec((1, n), lambda i: (0, 0)),
                  pl.BlockSpec((1, n), lambda i: (0, 0))],
        out_specs=[pl.BlockSpec((tm, n), lambda i: (i, 0)),
                   pl.BlockSpec((tm, n), lambda i: (i, 0))],
        out_shape=[jax.ShapeDtypeStruct((m, n), F32), jax.ShapeDtypeStruct((m, n), BF16)],
        compiler_params=_params(("parallel",)),
        name="outproj_ln1",
    )(a, w, x, g, b)


def _ffn_act_kernel(hb_ref, wg_ref, wu_ref, o_ref):
    hb = hb_ref[...]
    gate = jnp.dot(hb, wg_ref[...], preferred_element_type=F32)
    up = jnp.dot(hb, wu_ref[...], preferred_element_type=F32)
    o_ref[...] = (jax.nn.silu(gate) * up).astype(o_ref.dtype)


def _ffn_act(hb, wg, wu, tm, tf):
    m, d = hb.shape
    f = wg.shape[1]
    return pl.pallas_call(
        _ffn_act_kernel,
        grid=(m // tm, f // tf),
        in_specs=[pl.BlockSpec((tm, d), lambda i, j: (i, 0)),
                  pl.BlockSpec((d, tf), lambda i, j: (0, j)),
                  pl.BlockSpec((d, tf), lambda i, j: (0, j))],
        out_specs=pl.BlockSpec((tm, tf), lambda i, j: (i, j)),
        out_shape=jax.ShapeDtypeStruct((m, f), BF16),
        compiler_params=_params(("parallel", "arbitrary")),
        name="swiglu_act",
    )(hb, wg, wu)


def _ffn_down_ln_kernel(a_ref, w_ref, h_ref, g_ref, b_ref, o_ref, *, alpha):
    _matmul_residual_ln(a_ref, w_ref, h_ref, g_ref, b_ref, o_ref, alpha)


def _ffn_down_ln(act, wd, h, g, b, alpha, tm):
    m, f = act.shape
    d = wd.shape[1]
    return pl.pallas_call(
        functools.partial(_ffn_down_ln_kernel, alpha=alpha),
        grid=(m // tm,),
        in_specs=[pl.BlockSpec((tm, f), lambda i: (i, 0)),
                  pl.BlockSpec((f, d), lambda i: (0, 0), pipeline_mode=pl.Buffered(1)),
                  pl.BlockSpec((tm, d), lambda i: (i, 0)),
                  pl.BlockSpec((1, d), lambda i: (0, 0)),
                  pl.BlockSpec((1, d), lambda i: (0, 0))],
        out_specs=pl.BlockSpec((tm, d), lambda i: (i, 0)),
        out_shape=jax.ShapeDtypeStruct((m, d), F32),
        compiler_params=_params(("parallel",)),
        name="swiglu_down_ln2",
    )(act, wd, h, g, b)


def _layer(x2, w_in, b_gate, na_rpb, w_na_out, s5_a_re, s5_a_im, s5_log_dt, s5_b_re, s5_b_im,
           s5_c_re, s5_c_im, s5_d, w_glu, b_glu, w_s5_out, w_out, ln1_g, ln1_b,
           w_ffn_gate, w_ffn_up, w_ffn_down, ln2_g, ln2_b, alpha):
    seq, d_model = x2.shape
    rows = seq // GRID_W
    nc = seq // S5_CHUNK
    gate_col0 = 3 * NA_WIDTH + S5_WIDTH
    in_cols = gate_col0 + 2 * d_model
    row = lambda v: v.astype(F32).reshape(1, -1)

    col_scale = jnp.where(jnp.arange(in_cols) < NA_WIDTH, NA_HEAD_DIM ** -0.5, 1.0).astype(F32).reshape(1, -1)
    col_bias = jnp.concatenate([jnp.zeros((gate_col0,), F32), b_gate.astype(F32)]).reshape(1, -1)
    proj, u = _inproj(x2, w_in.astype(BF16), col_scale, col_bias, 3 * NA_WIDTH, gate_col0, tm=1024, tn=1024)

    att = _attention(proj, _attention_bias(na_rpb, rows), seq)

    w_state, m_out, a_re, a_im = _s5_tables(s5_a_re, s5_a_im, s5_log_dt, s5_b_re, s5_b_im,
                                            s5_c_re, s5_c_im, s5_d)
    ug, h_states = _s5_states(u, w_state, a_re, a_im)
    ys = _s5_outputs(ug, h_states, m_out, nsplit=2)

    merged = _glu_merge(att, w_na_out.astype(BF16), ys, w_glu.astype(BF16), row(b_glu), w_s5_out.astype(BF16),
                        proj, gate_col0, tm=1024, tn=512)
    h, hb = _outproj_ln(merged, w_out.astype(BF16), x2, row(ln1_g), row(ln1_b), alpha, tm=512)
    act = _ffn_act(hb, w_ffn_gate.astype(BF16), w_ffn_up.astype(BF16), tm=1024, tf=512)
    return _ffn_down_ln(act, w_ffn_down.astype(BF16), h, row(ln2_g), row(ln2_b), alpha, tm=512)


def kernel(x, w_in, b_gate, na_rpb, w_na_out, s5_a_re, s5_a_im, s5_log_dt, s5_b_re, s5_b_im, s5_c_re, s5_c_im, s5_d, w_glu, b_glu, w_s5_out, w_out, ln1_g, ln1_b, w_ffn_gate, w_ffn_up, w_ffn_down, ln2_g, ln2_b):
    bsz, seq, d_model = x.shape
    depth = w_in.shape[0]
    alpha = (2.0 * depth) ** 0.25
    outs = []
    for bi in range(bsz):
        xb = x[bi]
        for l in range(depth):
            xb = _layer(xb, w_in[l], b_gate[l], na_rpb[l], w_na_out[l], s5_a_re[l], s5_a_im[l], s5_log_dt[l],
                        s5_b_re[l], s5_b_im[l], s5_c_re[l], s5_c_im[l], s5_d[l], w_glu[l], b_glu[l],
                        w_s5_out[l], w_out[l], ln1_g[l], ln1_b[l], w_ffn_gate[l], w_ffn_up[l],
                        w_ffn_down[l], ln2_g[l], ln2_b[l], alpha)
        outs.append(xb)
    return jnp.stack(outs)
```

```python
import functools
import math

import jax
import jax.numpy as jnp
from jax import lax
from jax.experimental import pallas as pl
from jax.experimental.pallas import tpu as pltpu

F32 = jnp.float32
BF16 = jnp.bfloat16

GRID_W = 64
NA_HEADS = 8
NA_HEAD_DIM = 128
NA_WIDTH = NA_HEADS * NA_HEAD_DIM
NA_ROWS = 8
NA_COLS = 16
S5_GROUP = 16
S5_GROUPS = 64
S5_WIDTH = S5_GROUP * S5_GROUPS
S5_STATE = 64
LN_EPS = 1e-5
MASK_VALUE = -1e30

S5_CHUNK = 16
S5_CHUNK_WIDTH = S5_CHUNK * S5_GROUP
S5_STATE_WIDTH = 4 * S5_STATE
SCAN_GROUPS = 8

ATT_Q_ROWS = 8
ATT_K_ROWS = 16
ATT_Q = ATT_Q_ROWS * GRID_W
ATT_K = ATT_K_ROWS * GRID_W
ATT_KBLK = 256
LANES = 128
ATT_WIN = NA_ROWS * GRID_W + LANES
ATT_HEADS_PER_STEP = 4
ATT_ROW_SPLIT = 2
LN_ROW_SPLIT = 4
S5_LANE_GROUPS = LANES // S5_GROUP
RELAYOUT_ROWS = 16

VMEM_LIMIT = 56 * 1024 * 1024


def _params(sem):
    return pltpu.CompilerParams(dimension_semantics=sem, vmem_limit_bytes=VMEM_LIMIT)


def _layer_norm_rows(y, g, b):
    mu = jnp.mean(y, axis=-1, keepdims=True)
    d = y - mu
    var = jnp.mean(d * d, axis=-1, keepdims=True)
    return d * lax.rsqrt(var + LN_EPS) * g + b


def _inproj_kernel(x_ref, w_ref, s_ref, b_ref, qkv_ref, u_ref, g_ref, xb_ref, *, u_tile0, gate_tile0):
    j = pl.program_id(1)

    @pl.when(j == 0)
    def _():
        xb_ref[...] = x_ref[...].astype(BF16)

    @pl.when(j < u_tile0)
    def _():
        acc = (jnp.dot(xb_ref[...], w_ref[...], preferred_element_type=F32) * s_ref[...]).astype(qkv_ref.dtype)
        for h in range(qkv_ref.shape[0]):
            qkv_ref[h] = acc[:, h * NA_HEAD_DIM:(h + 1) * NA_HEAD_DIM]

    @pl.when((j >= u_tile0) & (j < gate_tile0))
    def _():
        u_ref[...] = jnp.dot(xb_ref[...], w_ref[...], preferred_element_type=F32)

    @pl.when(j >= gate_tile0)
    def _():
        acc = jnp.dot(xb_ref[...], w_ref[...], preferred_element_type=F32)
        g_ref[...] = jax.nn.sigmoid(acc + b_ref[...]).astype(g_ref.dtype)


def _inproj(x, w, col_scale, col_bias, u_col0, gate_col0, tm, tn):
    m, k = x.shape
    n = w.shape[1]
    u_tile0, gate_tile0 = u_col0 // tn, gate_col0 // tn
    heads_per_tile = tn // NA_HEAD_DIM
    last_u = gate_tile0 - u_tile0 - 1
    last_g = n // tn - gate_tile0 - 1
    return pl.pallas_call(
        functools.partial(_inproj_kernel, u_tile0=u_tile0, gate_tile0=gate_tile0),
        grid=(m // tm, n // tn),
        in_specs=[
            pl.BlockSpec((tm, k), lambda i, j: (i, 0)),
            pl.BlockSpec((k, tn), lambda i, j: (0, j)),
            pl.BlockSpec((1, tn), lambda i, j: (0, j)),
            pl.BlockSpec((1, tn), lambda i, j: (0, j)),
        ],
        out_specs=[pl.BlockSpec((heads_per_tile, tm, NA_HEAD_DIM), lambda i, j: (jnp.minimum(j, u_tile0 - 1), i, 0)),
                   pl.BlockSpec((tm, tn), lambda i, j: (i, jnp.clip(j - u_tile0, 0, last_u))),
                   pl.BlockSpec((tm, tn), lambda i, j: (i, jnp.clip(j - gate_tile0, 0, last_g)))],
        out_shape=[jax.ShapeDtypeStruct((u_col0 // NA_HEAD_DIM, m, NA_HEAD_DIM), BF16),
                   jax.ShapeDtypeStruct((m, gate_col0 - u_col0), F32),
                   jax.ShapeDtypeStruct((m, n - gate_col0), BF16)],
        scratch_shapes=[pltpu.VMEM((tm, k), BF16)],
        compiler_params=_params(("parallel", "arbitrary")),
        name="inproj",
    )(x, w, col_scale, col_bias)


def _attn_windows(rows):
    nb = rows // ATT_Q_ROWS
    kinds = []
    for b in (0, 1, nb - 1):
        base = min(max(b * ATT_Q_ROWS - NA_ROWS // 2, 0), rows - ATT_K_ROWS)
        geo = []
        for ql in range(ATT_Q_ROWS):
            qr = b * ATT_Q_ROWS + ql
            r0 = min(max(qr - NA_ROWS // 2, 0), rows - NA_ROWS)
            koff = r0 - base
            lane0 = min(LANES * (koff * GRID_W // LANES), ATT_K - ATT_WIN)
            geo.append((lane0, koff * GRID_W - lane0, r0 - qr + NA_ROWS - 1))
        kinds.append(tuple(geo))
    return tuple(kinds)


def _attn_kernel(q_ref, k0, k1, k2, k3, v0, v1, v2, v3, bias_ref, o_ref, p_ref, *, windows, nb):
    b = pl.program_id(1)
    k_refs, v_refs = (k0, k1, k2, k3), (v0, v1, v2, v3)

    def one_head(hh, geo):
        c = slice(hh * NA_HEAD_DIM, (hh + 1) * NA_HEAD_DIM)
        k = jnp.concatenate([r[hh] for r in k_refs], axis=0)
        v = jnp.concatenate([r[hh] for r in v_refs], axis=0)
        per = ATT_Q_ROWS // ATT_ROW_SPLIT
        for part in range(ATT_ROW_SPLIT):
            sub = geo[part * per:(part + 1) * per]
            lo = min(lane0 for lane0, _, _ in sub)
            hi = max(lane0 for lane0, _, _ in sub) + ATT_WIN
            rows = slice(part * per * GRID_W, (part + 1) * per * GRID_W)
            s = lax.dot_general(q_ref[hh, rows, :], k[lo:hi], (((1,), (1,)), ((), ())),
                                preferred_element_type=F32)
            inv = []
            for i, (lane0, _, _) in enumerate(sub):
                ql = part * per + i
                r = slice(ql * GRID_W, (ql + 1) * GRID_W)
                sw = s[i * GRID_W:(i + 1) * GRID_W, lane0 - lo:lane0 - lo + ATT_WIN] + bias_ref[hh, ql]
                m = jnp.max(sw, axis=-1, keepdims=True)
                p = jnp.exp(sw - m)
                inv.append(1.0 / jnp.sum(p, axis=-1, keepdims=True))
                if lane0 > lo:
                    p_ref[hh, r, lo:lane0] = jnp.zeros((GRID_W, lane0 - lo), BF16)
                p_ref[hh, r, lane0:lane0 + ATT_WIN] = p.astype(BF16)
                if lane0 + ATT_WIN < hi:
                    p_ref[hh, r, lane0 + ATT_WIN:hi] = jnp.zeros((GRID_W, hi - lane0 - ATT_WIN), BF16)
            o = jnp.dot(p_ref[hh, rows, lo:hi], v[lo:hi], preferred_element_type=F32)
            o_ref[rows, c] = (o * jnp.concatenate(inv, axis=0)).astype(o_ref.dtype)

    def block_kind(geo):
        for hh in range(ATT_HEADS_PER_STEP):
            one_head(hh, geo)

    @pl.when(b == 0)
    def _():
        block_kind(windows[0])

    @pl.when((b > 0) & (b < nb - 1))
    def _():
        block_kind(windows[1])

    @pl.when(b == nb - 1)
    def _():
        block_kind(windows[2])


def _attention(qkv, bias, seq):
    nb = seq // ATT_Q
    n_kblk = seq // ATT_KBLK
    per_blk = ATT_K // ATT_KBLK

    def kstart(b):
        return jnp.clip(2 * b - 1, 0, n_kblk - per_blk)

    def kind(b):
        return jnp.where(b == 0, 0, jnp.where(b == nb - 1, 2, 1))

    hps = ATT_HEADS_PER_STEP
    width = hps * NA_HEAD_DIM
    n_groups = NA_HEADS // hps

    def kv_spec(group0, t):
        return pl.BlockSpec((hps, ATT_KBLK, NA_HEAD_DIM), lambda h, b: (group0 + h, kstart(b) + t, 0))

    in_specs = [pl.BlockSpec((hps, ATT_Q, NA_HEAD_DIM), lambda h, b: (h, b, 0))]
    in_specs += [kv_spec(n_groups, t) for t in range(per_blk)]
    in_specs += [kv_spec(2 * n_groups, t) for t in range(per_blk)]
    in_specs += [pl.BlockSpec((None, hps, ATT_Q_ROWS, GRID_W, ATT_WIN), lambda h, b: (kind(b), h, 0, 0, 0))]
    return pl.pallas_call(
        functools.partial(_attn_kernel, windows=_attn_windows(seq // GRID_W), nb=nb),
        grid=(n_groups, nb),
        in_specs=in_specs,
        out_specs=pl.BlockSpec((ATT_Q, width), lambda h, b: (b, h)),
        out_shape=jax.ShapeDtypeStruct((seq, NA_WIDTH), BF16),
        scratch_shapes=[pltpu.VMEM((hps, ATT_Q, ATT_K), BF16)],
        compiler_params=_params(("parallel", "arbitrary")),
        name="na_attention",
    )(*([qkv] * (1 + 2 * per_blk)), bias)


def _attention_bias(rpb, rows):
    cols = jnp.arange(GRID_W)
    col_start = jnp.clip(cols - NA_COLS // 2, 0, GRID_W - NA_COLS)
    col_ok = (cols[None, :] >= col_start[:, None]) & (cols[None, :] < col_start[:, None] + NA_COLS)
    col_idx = jnp.clip(cols[None, :] - cols[:, None], -(NA_COLS - 1), NA_COLS - 1) + (NA_COLS - 1)
    pick = (jnp.arange(2 * NA_COLS - 1)[:, None, None] == col_idx[None]).astype(F32)
    by_col = jnp.einsum("hrj,jqk->hqrk", rpb.astype(F32), pick, precision=lax.Precision.HIGHEST)
    by_col = jnp.where(col_ok[None, :, None, :], by_col, MASK_VALUE)
    kinds = []
    for geo in _attn_windows(rows):
        strips = []
        for _, phase, rfirst in geo:
            strip = by_col[:, :, rfirst:rfirst + NA_ROWS, :].reshape(NA_HEADS, GRID_W, NA_ROWS * GRID_W)
            strips.append(jnp.pad(strip, ((0, 0), (0, 0), (phase, ATT_WIN - NA_ROWS * GRID_W - phase)),
                                  constant_values=MASK_VALUE))
        kinds.append(jnp.stack(strips, axis=1))
    return jnp.stack(kinds)


def _s5_tables(a_re, a_im, log_dt, b_re, b_im, c_re, c_im, d):
    t_len = S5_CHUNK
    groups = a_re.shape[1]
    steps = jnp.arange(t_len, dtype=F32)
    flip = t_len - 1 - steps

    def cmul(xr, xi, yr, yi):
        return xr * yr - xi * yi, xr * yi + xi * yr

    def powers(zr, zi, n):
        mag = jnp.exp(zr * n)
        return mag * jnp.cos(zi * n), mag * jnp.sin(zi * n)

    ar, ai = a_re.astype(F32), a_im.astype(F32)
    dt = jnp.exp(log_dt.astype(F32))[..., None]
    zr, zi = ar * dt, ai * dt
    lr, li = powers(zr, zi, 1.0)
    lr = lr - 1.0
    den = ar * ar + ai * ai
    fr, fi = (lr * ar + li * ai) / den, (li * ar - lr * ai) / den
    swap = lambda v: jnp.swapaxes(v.astype(F32), 2, 3)
    bbr, bbi = cmul(fr[:, :, None, :], fi[:, :, None, :], swap(b_re), swap(b_im))
    crt, cit = swap(c_re), swap(c_im)
    n_w = jnp.stack([flip, steps])[:, None, :, None]
    wp_r, wp_i = powers(zr[:, :, None, :], zi[:, :, None, :], n_w)
    wr, wi = cmul(wp_r[:, :, :, None, :], wp_i[:, :, :, None, :], bbr[:, :, None], bbi[:, :, None])
    wr = wr.reshape(2, groups, S5_CHUNK_WIDTH, S5_STATE)
    wi = wi.reshape(2, groups, S5_CHUNK_WIDTH, S5_STATE)
    n_e = jnp.stack([steps + 1.0, flip + 1.0])[:, None, None, :]
    ep_r, ep_i = powers(zr[..., None], zi[..., None], n_e)
    er, ei = cmul(crt[:, :, :, None, :], cit[:, :, :, None, :], ep_r[..., None], ep_i[..., None])
    er = er.reshape(2, groups, S5_STATE, S5_CHUNK_WIDTH)
    ei = ei.reshape(2, groups, S5_STATE, S5_CHUNK_WIDTH)
    lp_r, lp_i = powers(zr[..., None], zi[..., None], steps)
    xr, xi = cmul(crt[:, :, :, None, :], cit[:, :, :, None, :], lp_r[..., None], lp_i[..., None])
    lhs = jnp.concatenate([bbr, -bbi], axis=-1)
    rhs = jnp.concatenate([xr, xi], axis=2).reshape(2, groups, 2 * S5_STATE, S5_CHUNK_WIDTH)
    lag_kernels = jnp.einsum("dgck,dgkn->dgcn", lhs, rhs, precision=lax.Precision.HIGHEST)
    step_r, step_i = powers(zr, zi, float(t_len))

    w_state = jnp.concatenate([wr[0], wr[1], wi[0], wi[1]], axis=-1)
    m_state = jnp.concatenate([er[0], er[1], -ei[0], -ei[1]], axis=1)
    kf, kb = lag_kernels[0], lag_kernels[1]
    kb_rev = kb[:, :, S5_GROUP:].reshape(groups, S5_GROUP, t_len - 1, S5_GROUP)[:, :, ::-1]
    lag0 = kf[:, :, :S5_GROUP] + kb[:, :, :S5_GROUP] + d.astype(F32)[:, None, :] * jnp.eye(S5_GROUP, dtype=F32)[None]
    lags = jnp.concatenate([kb_rev.reshape(groups, S5_GROUP, -1), lag0, kf[:, :, S5_GROUP:]], axis=-1)
    nlag = 2 * t_len - 1
    padded = jnp.pad(lags, ((0, 0), (0, 0), (0, S5_GROUP)))
    skew = jnp.tile(padded, (1, 1, t_len))[:, :, :t_len * nlag * S5_GROUP]
    skew = skew.reshape(groups, S5_GROUP, t_len, nlag, S5_GROUP)[:, :, :, t_len - 1:, :]
    toe = skew.transpose(0, 2, 1, 3, 4).reshape(groups, S5_CHUNK_WIDTH, S5_CHUNK_WIDTH)
    m_out = jnp.concatenate([toe, m_state], axis=1)
    return (w_state.astype(BF16), m_out.astype(BF16),
            jnp.concatenate([step_r[0], step_r[1]], axis=-1), jnp.concatenate([step_i[0], step_i[1]], axis=-1))


def _lane_group_ids():
    return lax.broadcasted_iota(jnp.int32, (RELAYOUT_ROWS, LANES), 1) // S5_GROUP


def _transpose_lane_blocks(pieces, grp):
    nblk = S5_LANE_GROUPS
    rolled = []
    for r in range(nblk):
        acc = None
        for b in range(nblk):
            acc = pieces[b] if acc is None else jnp.where(grp == (b - r) % nblk, pieces[b], acc)
        rolled.append(pltpu.roll(acc, S5_GROUP * r, 1) if r else acc)
    out = []
    for k in range(nblk):
        acc = None
        for b in range(nblk):
            z = rolled[(b - k) % nblk]
            acc = z if acc is None else jnp.where(grp == b, z, acc)
        out.append(acc)
    return out


def _s5_state_kernel(u_ref, w_ref, ar_ref, ai_ref, ug_ref, h_ref, s_ref):
    lg = S5_LANE_GROUPS
    nc = ug_ref.shape[1]
    grp = _lane_group_ids()

    def regroup(rb, carry):
        r0 = pl.multiple_of(rb * RELAYOUT_ROWS, RELAYOUT_ROWS)
        for half in range(S5_CHUNK_WIDTH // LANES):
            steps = [u_ref[pl.ds(r0 * S5_CHUNK + half * S5_LANE_GROUPS + tp, RELAYOUT_ROWS, stride=S5_CHUNK), :]
                     for tp in range(S5_LANE_GROUPS)]
            for g, v in enumerate(_transpose_lane_blocks(steps, grp)):
                ug_ref[g, pl.ds(r0, RELAYOUT_ROWS), half * LANES:(half + 1) * LANES] = v.astype(BF16)
        return carry

    lax.fori_loop(0, nc // RELAYOUT_ROWS, regroup, 0, unroll=2)

    half = S5_STATE_WIDTH // 2
    for g in range(lg):
        sg = jnp.dot(ug_ref[g], w_ref[g], preferred_element_type=F32)
        s_ref[0, pl.ds(g, nc, stride=lg), :] = sg[:, :half]
        s_ref[1, pl.ds(g, nc, stride=lg), :] = sg[:, half:]

    lane = lax.broadcasted_iota(jnp.int32, (lg, half), 1)
    fwd = lane < S5_STATE
    ar = ar_ref[...]
    ai = ai_ref[...]
    rows = lambda i: pl.ds(pl.multiple_of(i * lg, lg), lg)

    def advance(i, hr, hi):
        sr = jnp.where(fwd, s_ref[0, rows(i), :], s_ref[0, rows(nc - 1 - i), :])
        si = jnp.where(fwd, s_ref[1, rows(i), :], s_ref[1, rows(nc - 1 - i), :])
        return ar * hr - ai * hi + sr, ar * hi + ai * hr + si

    def first_touch(i, carry):
        hr, hi = carry
        for c in (i, nc - 1 - i):
            h_ref[0, rows(c), :] = hr
            h_ref[1, rows(c), :] = hi
        return advance(i, hr, hi)

    def second_touch(i, carry):
        hr, hi = carry
        j = nc - 1 - i
        h_ref[0, rows(i), :] = jnp.where(fwd, hr, h_ref[0, rows(i), :])
        h_ref[1, rows(i), :] = jnp.where(fwd, hi, h_ref[1, rows(i), :])
        h_ref[0, rows(j), :] = jnp.where(fwd, h_ref[0, rows(j), :], hr)
        h_ref[1, rows(j), :] = jnp.where(fwd, h_ref[1, rows(j), :], hi)
        return advance(i, hr, hi)

    zero = jnp.zeros((lg, half), F32)
    carry = lax.fori_loop(0, nc // 2, first_touch, (zero, zero))
    lax.fori_loop(nc // 2, nc, second_touch, carry)


def _s5_states(u, w_state, a_re, a_im):
    seq, width = u.shape
    g = width // S5_GROUP
    nc = seq // S5_CHUNK
    lg = S5_LANE_GROUPS
    half = S5_STATE_WIDTH // 2
    return pl.pallas_call(
        _s5_state_kernel,
        grid=(width // LANES,),
        in_specs=[pl.BlockSpec((seq, LANES), lambda j: (0, j)),
                  pl.BlockSpec((lg, S5_CHUNK_WIDTH, S5_STATE_WIDTH), lambda j: (j, 0, 0)),
                  pl.BlockSpec((lg, half), lambda j: (j, 0)),
                  pl.BlockSpec((lg, half), lambda j: (j, 0))],
        out_specs=[pl.BlockSpec((lg, nc, S5_CHUNK_WIDTH), lambda j: (j, 0, 0)),
                   pl.BlockSpec((None, 2, nc * lg, half), lambda j: (j, 0, 0, 0))],
        out_shape=[jax.ShapeDtypeStruct((g, nc, S5_CHUNK_WIDTH), BF16),
                   jax.ShapeDtypeStruct((g // lg, 2, nc * lg, half), F32)],
        scratch_shapes=[pltpu.VMEM((2, nc * lg, half), F32)],
        compiler_params=_params(("parallel",)),
        name="s5_states",
    )(u, w_state, a_re, a_im)


def _s5_out_kernel(ug_ref, h_ref, m_ref, o_ref, y_ref):
    cw = S5_CHUNK_WIDTH
    nch = ug_ref.shape[1]
    for g in range(S5_LANE_GROUPS):
        hg = jnp.concatenate([h_ref[part, pl.ds(g, nch, stride=S5_LANE_GROUPS), :] for part in range(2)],
                             axis=1).astype(BF16)
        y = jnp.dot(ug_ref[g], m_ref[g, :cw, :], preferred_element_type=F32)
        y = y + jnp.dot(hg, m_ref[g, cw:, :], preferred_element_type=F32)
        y_ref[g] = jax.nn.gelu(y, approximate=True)
    grp = _lane_group_ids()

    def regroup(rb, carry):
        r0 = pl.multiple_of(rb * RELAYOUT_ROWS, RELAYOUT_ROWS)
        for half in range(cw // LANES):
            groups = [y_ref[g, pl.ds(r0, RELAYOUT_ROWS), half * LANES:(half + 1) * LANES]
                      for g in range(S5_LANE_GROUPS)]
            for tp, v in enumerate(_transpose_lane_blocks(groups, grp)):
                t = half * S5_LANE_GROUPS + tp
                o_ref[pl.ds(r0 * S5_CHUNK + t, RELAYOUT_ROWS, stride=S5_CHUNK), :] = v
        return carry

    lax.fori_loop(0, nch // RELAYOUT_ROWS, regroup, 0, unroll=2)


def _s5_outputs(ug, h3, m_out, nsplit):
    g, nc, cw = ug.shape
    nch = nc // nsplit
    lg = S5_LANE_GROUPS
    return pl.pallas_call(
        _s5_out_kernel,
        grid=(g // lg, nsplit),
        in_specs=[pl.BlockSpec((lg, nch, cw), lambda j, h: (j, h, 0)),
                  pl.BlockSpec((None, 2, nch * lg, S5_STATE_WIDTH // 2), lambda j, h: (j, 0, h, 0)),
                  pl.BlockSpec((lg, cw + S5_STATE_WIDTH, cw), lambda j, h: (j, 0, 0))],
        out_specs=pl.BlockSpec((nch * S5_CHUNK, LANES), lambda j, h: (h, j)),
        out_shape=jax.ShapeDtypeStruct((nc * S5_CHUNK, g * S5_GROUP), F32),
        scratch_shapes=[pltpu.VMEM((lg, nch, cw), F32)],
        compiler_params=_params(("parallel", "parallel")),
        name="s5_outputs",
    )(ug, h3, m_out)


def _merge_kernel(a1_ref, w1_ref, ys_ref, wg_ref, bg_ref, w2_ref, g1_ref, g2_ref, o_ref, ysb_ref, z_ref, *, tg):
    @pl.when(pl.program_id(1) == 0)
    def _():
        ysb_ref[...] = ys_ref[...].astype(BF16)
        for c in range(z_ref.shape[1] // tg):
            cols = slice(c * tg, (c + 1) * tg)
            t = jnp.dot(ysb_ref[...], wg_ref[:, cols], preferred_element_type=F32) + bg_ref[:, cols]
            z_ref[:, cols] = (ys_ref[:, cols] * jax.nn.sigmoid(t)).astype(BF16)

    y1 = jnp.dot(a1_ref[...], w1_ref[...], preferred_element_type=F32)
    y2 = jnp.dot(z_ref[...], w2_ref[...], preferred_element_type=F32)
    o_ref[...] = (g1_ref[...].astype(F32) * y1 + g2_ref[...].astype(F32) * y2).astype(o_ref.dtype)


def _glu_merge(att, w_na, ys, w_glu, b_glu, w_s5, gates, tm, tn):
    m, k1 = att.shape
    k2 = ys.shape[1]
    n = w_na.shape[1]
    g1 = 0
    g2 = n // tn
    return pl.pallas_call(
        functools.partial(_merge_kernel, tg=tn),
        grid=(m // tm, n // tn),
        in_specs=[pl.BlockSpec((tm, k1), lambda i, j: (i, 0)),
                  pl.BlockSpec((k1, tn), lambda i, j: (0, j)),
                  pl.BlockSpec((tm, k2), lambda i, j: (i, 0)),
                  pl.BlockSpec((k2, k2), lambda i, j: (0, 0)),
                  pl.BlockSpec((1, k2), lambda i, j: (0, 0)),
                  pl.BlockSpec((k2, tn), lambda i, j: (0, j)),
                  pl.BlockSpec((tm, tn), lambda i, j: (i, g1 + j)),
                  pl.BlockSpec((tm, tn), lambda i, j: (i, g2 + j))],
        out_specs=pl.BlockSpec((tm, tn), lambda i, j: (i, j)),
        out_shape=jax.ShapeDtypeStruct((m, n), BF16),
        scratch_shapes=[pltpu.VMEM((tm, k2), BF16), pltpu.VMEM((tm, k2), BF16)],
        compiler_params=_params(("parallel", "arbitrary")),
        name="glu_gated_merge",
    )(att, w_na, ys, w_glu, b_glu, w_s5, gates, gates)


def _matmul_residual_ln(a_ref, w_ref, res_ref, g_ref, b_ref, o_ref, alpha, ob_ref=None):
    sub = a_ref.shape[0] // LN_ROW_SPLIT
    for r in range(LN_ROW_SPLIT):
        rows = slice(r * sub, (r + 1) * sub)
        y = jnp.dot(a_ref[rows, :], w_ref[...], preferred_element_type=F32)
        out = _layer_norm_rows(alpha * res_ref[rows, :] + y, g_ref[...], b_ref[...])
        o_ref[rows, :] = out
        if ob_ref is not None:
            ob_ref[rows, :] = out.astype(ob_ref.dtype)


def _outproj_ln_kernel(a_ref, w_ref, x_ref, g_ref, b_ref, h_ref, hb_ref, *, alpha):
    _matmul_residual_ln(a_ref, w_ref, x_ref, g_ref, b_ref, h_ref, alpha, hb_ref)


def _outproj_ln(a, w, x, g, b, alpha, tm):
    m, k = a.shape
    n = w.shape[1]
    return pl.pallas_call(
        functools.partial(_outproj_ln_kernel, alpha=alpha),
        grid=(m // tm,),
        in_specs=[pl.BlockSpec((tm, k), lambda i: (i, 0)),
                  pl.BlockSpec((k, n), lambda i: (0, 0)),
                  pl.BlockSpec((tm, n), lambda i: (i, 0)),
                  pl.BlockSpec((1, n), lambda i: (0, 0)),
                  pl.BlockSpec((1, n), lambda i: (0, 0))],
        out_specs=[pl.BlockSpec((tm, n), lambda i: (i, 0)),
                   pl.BlockSpec((tm, n), lambda i: (i, 0))],
        out_shape=[jax.ShapeDtypeStruct((m, n), F32), jax.ShapeDtypeStruct((m, n), BF16)],
        compiler_params=_params(("parallel",)),
        name="outproj_ln1",
    )(a, w, x, g, b)


def _ffn_act_kernel(hb_ref, wg_ref, wu_ref, o_ref):
    hb = hb_ref[...]
    gate = jnp.dot(hb, wg_ref[...], preferred_element_type=F32)
    up = jnp.dot(hb, wu_ref[...], preferred_element_type=F32)
    o_ref[...] = (jax.nn.silu(gate) * up).astype(o_ref.dtype)


def _ffn_act(hb, wg, wu, tm, tf):
    m, d = hb.shape
    f = wg.shape[1]
    return pl.pallas_call(
        _ffn_act_kernel,
        grid=(m // tm, f // tf),
        in_specs=[pl.BlockSpec((tm, d), lambda i, j: (i, 0)),
                  pl.BlockSpec((d, tf), lambda i, j: (0, j)),
                  pl.BlockSpec((d, tf), lambda i, j: (0, j))],
        out_specs=pl.BlockSpec((tm, tf), lambda i, j: (i, j)),
        out_shape=jax.ShapeDtypeStruct((m, f), BF16),
        compiler_params=_params(("parallel", "arbitrary")),
        name="swiglu_act",
    )(hb, wg, wu)


def _ffn_down_ln_kernel(a_ref, w_ref, h_ref, g_ref, b_ref, o_ref, *, alpha):
    _matmul_residual_ln(a_ref, w_ref, h_ref, g_ref, b_ref, o_ref, alpha)


def _ffn_down_ln(act, wd, h, g, b, alpha, tm):
    m, f = act.shape
    d = wd.shape[1]
    return pl.pallas_call(
        functools.partial(_ffn_down_ln_kernel, alpha=alpha),
        grid=(m // tm,),
        in_specs=[pl.BlockSpec((tm, f), lambda i: (i, 0)),
                  pl.BlockSpec((f, d), lambda i: (0, 0), pipeline_mode=pl.Buffered(1)),
                  pl.BlockSpec((tm, d), lambda i: (i, 0)),
                  pl.BlockSpec((1, d), lambda i: (0, 0)),
                  pl.BlockSpec((1, d), lambda i: (0, 0))],
        out_specs=pl.BlockSpec((tm, d), lambda i: (i, 0)),
        out_shape=jax.ShapeDtypeStruct((m, d), F32),
        compiler_params=_params(("parallel",)),
        name="swiglu_down_ln2",
    )(act, wd, h, g, b)


def _layer(x2, w_in, b_gate, na_rpb, w_na_out, s5_a_re, s5_a_im, s5_log_dt, s5_b_re, s5_b_im,
           s5_c_re, s5_c_im, s5_d, w_glu, b_glu, w_s5_out, w_out, ln1_g, ln1_b,
           w_ffn_gate, w_ffn_up, w_ffn_down, ln2_g, ln2_b, alpha):
    seq, d_model = x2.shape
    rows = seq // GRID_W
    nc = seq // S5_CHUNK
    gate_col0 = 3 * NA_WIDTH + S5_WIDTH
    in_cols = gate_col0 + 2 * d_model
    row = lambda v: v.astype(F32).reshape(1, -1)

    col_scale = jnp.where(jnp.arange(in_cols) < NA_WIDTH, NA_HEAD_DIM ** -0.5, 1.0).astype(F32).reshape(1, -1)
    col_bias = jnp.concatenate([jnp.zeros((gate_col0,), F32), b_gate.astype(F32)]).reshape(1, -1)
    qkv, u, gates = _inproj(x2, w_in.astype(BF16), col_scale, col_bias, 3 * NA_WIDTH, gate_col0, tm=1024, tn=1024)

    att = _attention(qkv, _attention_bias(na_rpb, rows), seq)

    w_state, m_out, a_re, a_im = _s5_tables(s5_a_re, s5_a_im, s5_log_dt, s5_b_re, s5_b_im,
                                            s5_c_re, s5_c_im, s5_d)
    ug, h_states = _s5_states(u, w_state, a_re, a_im)
    ys = _s5_outputs(ug, h_states, m_out, nsplit=2)

    merged = _glu_merge(att, w_na_out.astype(BF16), ys, w_glu.astype(BF16), row(b_glu), w_s5_out.astype(BF16),
                        gates, tm=1024, tn=512)
    h, hb = _outproj_ln(merged, w_out.astype(BF16), x2, row(ln1_g), row(ln1_b), alpha, tm=512)
    act = _ffn_act(hb, w_ffn_gate.astype(BF16), w_ffn_up.astype(BF16), tm=1024, tf=512)
    return _ffn_down_ln(act, w_ffn_down.astype(BF16), h, row(ln2_g), row(ln2_b), alpha, tm=512)


def kernel(x, w_in, b_gate, na_rpb, w_na_out, s5_a_re, s5_a_im, s5_log_dt, s5_b_re, s5_b_im, s5_c_re, s5_c_im, s5_d, w_glu, b_glu, w_s5_out, w_out, ln1_g, ln1_b, w_ffn_gate, w_ffn_up, w_ffn_down, ln2_g, ln2_b):
    bsz, seq, d_model = x.shape
    depth = w_in.shape[0]
    alpha = (2.0 * depth) ** 0.25
    outs = []
    for bi in range(bsz):
        xb = x[bi]
        for l in range(depth):
            xb = _layer(xb, w_in[l], b_gate[l], na_rpb[l], w_na_out[l], s5_a_re[l], s5_a_im[l], s5_log_dt[l],
                        s5_b_re[l], s5_b_im[l], s5_c_re[l], s5_c_im[l], s5_d[l], w_glu[l], b_glu[l],
                        w_s5_out[l], w_out[l], ln1_g[l], ln1_b[l], w_ffn_gate[l], w_ffn_up[l],
                        w_ffn_down[l], ln2_g[l], ln2_b[l], alpha)
        outs.append(xb)
    return jnp.stack(outs)
```

```python
import functools
import math

import jax
import jax.numpy as jnp
from jax import lax
from jax.experimental import pallas as pl
from jax.experimental.pallas import tpu as pltpu

F32 = jnp.float32
BF16 = jnp.bfloat16

GRID_W = 64
NA_HEADS = 8
NA_HEAD_DIM = 128
NA_WIDTH = NA_HEADS * NA_HEAD_DIM
NA_ROWS = 8
NA_COLS = 16
S5_GROUP = 16
S5_GROUPS = 64
S5_WIDTH = S5_GROUP * S5_GROUPS
S5_STATE = 64
LN_EPS = 1e-5
MASK_VALUE = -1e30

S5_CHUNK = 16
S5_CHUNK_WIDTH = S5_CHUNK * S5_GROUP
S5_STATE_WIDTH = 4 * S5_STATE
SCAN_GROUPS = 8

ATT_Q_ROWS = 8
ATT_K_ROWS = 16
ATT_Q = ATT_Q_ROWS * GRID_W
ATT_K = ATT_K_ROWS * GRID_W
ATT_KBLK = 256
LANES = 128
ATT_WIN = NA_ROWS * GRID_W + LANES
ATT_HEADS_PER_STEP = 4
ATT_ROW_SPLIT = 2
BF16_SUBLANES = 16
LN_ROW_SPLIT = 4
S5_LANE_GROUPS = LANES // S5_GROUP
RELAYOUT_ROWS = 16

VMEM_LIMIT = 56 * 1024 * 1024


def _params(sem):
    return pltpu.CompilerParams(dimension_semantics=sem, vmem_limit_bytes=VMEM_LIMIT)


def _layer_norm_rows(y, g, b):
    mu = jnp.mean(y, axis=-1, keepdims=True)
    d = y - mu
    var = jnp.mean(d * d, axis=-1, keepdims=True)
    return d * lax.rsqrt(var + LN_EPS) * g + b


def _inproj_kernel(x_ref, w_ref, s_ref, b_ref, qkv_ref, u_ref, g_ref, xb_ref, *, u_tile0, gate_tile0):
    j = pl.program_id(1)

    @pl.when(j == 0)
    def _():
        xb_ref[...] = x_ref[...].astype(BF16)

    @pl.when(j < u_tile0)
    def _():
        acc = (jnp.dot(xb_ref[...], w_ref[...], preferred_element_type=F32) * s_ref[...]).astype(qkv_ref.dtype)
        for h in range(qkv_ref.shape[0]):
            qkv_ref[h] = acc[:, h * NA_HEAD_DIM:(h + 1) * NA_HEAD_DIM]

    @pl.when((j >= u_tile0) & (j < gate_tile0))
    def _():
        u_ref[...] = jnp.dot(xb_ref[...], w_ref[...], preferred_element_type=F32)

    @pl.when(j >= gate_tile0)
    def _():
        acc = jnp.dot(xb_ref[...], w_ref[...], preferred_element_type=F32)
        g_ref[...] = jax.nn.sigmoid(acc + b_ref[...]).astype(g_ref.dtype)


def _inproj(x, w, col_scale, col_bias, u_col0, gate_col0, tm, tn):
    m, k = x.shape
    n = w.shape[1]
    u_tile0, gate_tile0 = u_col0 // tn, gate_col0 // tn
    heads_per_tile = tn // NA_HEAD_DIM
    last_u = gate_tile0 - u_tile0 - 1
    last_g = n // tn - gate_tile0 - 1
    return pl.pallas_call(
        functools.partial(_inproj_kernel, u_tile0=u_tile0, gate_tile0=gate_tile0),
        grid=(m // tm, n // tn),
        in_specs=[
            pl.BlockSpec((tm, k), lambda i, j: (i, 0)),
            pl.BlockSpec((k, tn), lambda i, j: (0, j)),
            pl.BlockSpec((1, tn), lambda i, j: (0, j)),
            pl.BlockSpec((1, tn), lambda i, j: (0, j)),
        ],
        out_specs=[pl.BlockSpec((heads_per_tile, tm, NA_HEAD_DIM), lambda i, j: (jnp.minimum(j, u_tile0 - 1), i, 0)),
                   pl.BlockSpec((tm, tn), lambda i, j: (i, jnp.clip(j - u_tile0, 0, last_u))),
                   pl.BlockSpec((tm, tn), lambda i, j: (i, jnp.clip(j - gate_tile0, 0, last_g)))],
        out_shape=[jax.ShapeDtypeStruct((u_col0 // NA_HEAD_DIM, m, NA_HEAD_DIM), BF16),
                   jax.ShapeDtypeStruct((m, gate_col0 - u_col0), F32),
                   jax.ShapeDtypeStruct((m, n - gate_col0), BF16)],
        scratch_shapes=[pltpu.VMEM((tm, k), BF16)],
        compiler_params=_params(("parallel", "arbitrary")),
        name="inproj",
    )(x, w, col_scale, col_bias)


def _attn_windows(rows):
    nb = rows // ATT_Q_ROWS
    kinds = []
    for b in (0, 1, nb - 1):
        base = min(max(b * ATT_Q_ROWS - NA_ROWS // 2, 0), rows - ATT_K_ROWS)
        geo = []
        for ql in range(ATT_Q_ROWS):
            qr = b * ATT_Q_ROWS + ql
            r0 = min(max(qr - NA_ROWS // 2, 0), rows - NA_ROWS)
            koff = r0 - base
            lane0 = min(LANES * (koff * GRID_W // LANES), ATT_K - ATT_WIN)
            geo.append((lane0, koff * GRID_W - lane0, r0 - qr + NA_ROWS - 1))
        kinds.append(tuple(geo))
    return tuple(kinds)


def _attn_kernel(q_ref, k0, k1, k2, k3, v0, v1, v2, v3, bias_ref, *rest, windows, nb, n_cast):
    cast_src, o_ref, cast_dst, p_ref = rest[:n_cast], rest[n_cast], rest[n_cast + 1:2 * n_cast + 1], rest[-1]
    b = pl.program_id(1)
    k_refs, v_refs = (k0, k1, k2, k3), (v0, v1, v2, v3)

    for src, dst in zip(cast_src, cast_dst):
        dst[...] = src[...].astype(dst.dtype)

    def one_head(hh, geo):
        c = slice(hh * NA_HEAD_DIM, (hh + 1) * NA_HEAD_DIM)
        k = jnp.concatenate([r[hh] for r in k_refs], axis=0)
        v = jnp.concatenate([r[hh] for r in v_refs], axis=0)
        per = ATT_Q_ROWS // ATT_ROW_SPLIT
        for part in range(ATT_ROW_SPLIT):
            sub = geo[part * per:(part + 1) * per]
            lo = min(lane0 for lane0, _, _ in sub)
            hi = max(lane0 for lane0, _, _ in sub) + ATT_WIN
            rows = slice(part * per * GRID_W, (part + 1) * per * GRID_W)
            s = lax.dot_general(q_ref[hh, rows, :], k[lo:hi], (((1,), (1,)), ((), ())),
                                preferred_element_type=F32)
            inv = []
            for i, (lane0, _, _) in enumerate(sub):
                ql = part * per + i
                r = slice(ql * GRID_W, (ql + 1) * GRID_W)
                sw = s[i * GRID_W:(i + 1) * GRID_W, lane0 - lo:lane0 - lo + ATT_WIN] + bias_ref[hh, ql]
                m = jnp.max(sw, axis=-1, keepdims=True)
                p = jnp.exp(sw - m)
                inv.append(1.0 / jnp.sum(p, axis=-1, keepdims=True))
                if lane0 > lo:
                    p_ref[hh, r, lo:lane0] = jnp.zeros((GRID_W, lane0 - lo), BF16)
                p_ref[hh, r, lane0:lane0 + ATT_WIN] = p.astype(BF16)
                if lane0 + ATT_WIN < hi:
                    p_ref[hh, r, lane0 + ATT_WIN:hi] = jnp.zeros((GRID_W, hi - lane0 - ATT_WIN), BF16)
            o = jnp.dot(p_ref[hh, rows, lo:hi], v[lo:hi], preferred_element_type=F32)
            o_ref[rows, c] = (o * jnp.concatenate(inv, axis=0)).astype(o_ref.dtype)

    def block_kind(geo):
        for hh in range(ATT_HEADS_PER_STEP):
            one_head(hh, geo)

    @pl.when(b == 0)
    def _():
        block_kind(windows[0])

    @pl.when((b > 0) & (b < nb - 1))
    def _():
        block_kind(windows[1])

    @pl.when(b == nb - 1)
    def _():
        block_kind(windows[2])


def _attention(qkv, bias, seq, weights):
    nb = seq // ATT_Q
    n_kblk = seq // ATT_KBLK
    per_blk = ATT_K // ATT_KBLK

    def kstart(b):
        return jnp.clip(2 * b - 1, 0, n_kblk - per_blk)

    def kind(b):
        return jnp.where(b == 0, 0, jnp.where(b == nb - 1, 2, 1))

    hps = ATT_HEADS_PER_STEP
    width = hps * NA_HEAD_DIM
    n_groups = NA_HEADS // hps

    def kv_spec(group0, t):
        return pl.BlockSpec((hps, ATT_KBLK, NA_HEAD_DIM), lambda h, b: (group0 + h, kstart(b) + t, 0))

    in_specs = [pl.BlockSpec((hps, ATT_Q, NA_HEAD_DIM), lambda h, b: (h, b, 0))]
    in_specs += [kv_spec(n_groups, t) for t in range(per_blk)]
    in_specs += [kv_spec(2 * n_groups, t) for t in range(per_blk)]
    in_specs += [pl.BlockSpec((None, hps, ATT_Q_ROWS, GRID_W, ATT_WIN), lambda h, b: (kind(b), h, 0, 0, 0))]

    n_steps = n_groups * nb

    def slab_spec(w):
        rows_w, cols_w = w.shape
        for rb in range(n_steps, 0, -1):
            cb = n_steps // rb
            if (rb * cb == n_steps and rows_w % rb == 0 and cols_w % cb == 0
                    and (rows_w // rb) % BF16_SUBLANES == 0 and (cols_w // cb) % LANES == 0):
                return pl.BlockSpec((rows_w // rb, cols_w // cb),
                                    lambda h, b, cb=cb: ((h * nb + b) // cb, (h * nb + b) % cb))
        raise ValueError(f"no slab tiling for {w.shape} over {n_steps} steps")

    slab_specs = [slab_spec(w) for w in weights]
    outs = pl.pallas_call(
        functools.partial(_attn_kernel, windows=_attn_windows(seq // GRID_W), nb=nb, n_cast=len(weights)),
        grid=(n_groups, nb),
        in_specs=in_specs + slab_specs,
        out_specs=[pl.BlockSpec((ATT_Q, width), lambda h, b: (b, h))] + slab_specs,
        out_shape=[jax.ShapeDtypeStruct((seq, NA_WIDTH), BF16)]
                  + [jax.ShapeDtypeStruct(w.shape, BF16) for w in weights],
        scratch_shapes=[pltpu.VMEM((hps, ATT_Q, ATT_K), BF16)],
        compiler_params=_params(("parallel", "arbitrary")),
        name="na_attention",
    )(*([qkv] * (1 + 2 * per_blk)), bias, *weights)
    return outs[0], outs[1:]


def _attention_bias(rpb, rows):
    cols = jnp.arange(GRID_W)
    col_start = jnp.clip(cols - NA_COLS // 2, 0, GRID_W - NA_COLS)
    col_ok = (cols[None, :] >= col_start[:, None]) & (cols[None, :] < col_start[:, None] + NA_COLS)
    col_idx = jnp.clip(cols[None, :] - cols[:, None], -(NA_COLS - 1), NA_COLS - 1) + (NA_COLS - 1)
    pick = (jnp.arange(2 * NA_COLS - 1)[:, None, None] == col_idx[None]).astype(F32)
    by_col = jnp.einsum("hrj,jqk->hqrk", rpb.astype(F32), pick, precision=lax.Precision.HIGHEST)
    by_col = jnp.where(col_ok[None, :, None, :], by_col, MASK_VALUE)
    kinds = []
    for geo in _attn_windows(rows):
        strips = []
        for _, phase, rfirst in geo:
            strip = by_col[:, :, rfirst:rfirst + NA_ROWS, :].reshape(NA_HEADS, GRID_W, NA_ROWS * GRID_W)
            strips.append(jnp.pad(strip, ((0, 0), (0, 0), (phase, ATT_WIN - NA_ROWS * GRID_W - phase)),
                                  constant_values=MASK_VALUE))
        kinds.append(jnp.stack(strips, axis=1))
    return jnp.stack(kinds)


def _s5_tables(a_re, a_im, log_dt, b_re, b_im, c_re, c_im, d):
    t_len = S5_CHUNK
    groups = a_re.shape[1]
    steps = jnp.arange(t_len, dtype=F32)
    flip = t_len - 1 - steps

    def cmul(xr, xi, yr, yi):
        return xr * yr - xi * yi, xr * yi + xi * yr

    def powers(zr, zi, n):
        mag = jnp.exp(zr * n)
        return mag * jnp.cos(zi * n), mag * jnp.sin(zi * n)

    ar, ai = a_re.astype(F32), a_im.astype(F32)
    dt = jnp.exp(log_dt.astype(F32))[..., None]
    zr, zi = ar * dt, ai * dt
    lr, li = powers(zr, zi, 1.0)
    lr = lr - 1.0
    den = ar * ar + ai * ai
    fr, fi = (lr * ar + li * ai) / den, (li * ar - lr * ai) / den
    swap = lambda v: jnp.swapaxes(v.astype(F32), 2, 3)
    bbr, bbi = cmul(fr[:, :, None, :], fi[:, :, None, :], swap(b_re), swap(b_im))
    crt, cit = swap(c_re), swap(c_im)
    n_w = jnp.stack([flip, steps])[:, None, :, None]
    wp_r, wp_i = powers(zr[:, :, None, :], zi[:, :, None, :], n_w)
    wr, wi = cmul(wp_r[:, :, :, None, :], wp_i[:, :, :, None, :], bbr[:, :, None], bbi[:, :, None])
    wr = wr.reshape(2, groups, S5_CHUNK_WIDTH, S5_STATE)
    wi = wi.reshape(2, groups, S5_CHUNK_WIDTH, S5_STATE)
    n_e = jnp.stack([steps + 1.0, flip + 1.0])[:, None, None, :]
    ep_r, ep_i = powers(zr[..., None], zi[..., None], n_e)
    er, ei = cmul(crt[:, :, :, None, :], cit[:, :, :, None, :], ep_r[..., None], ep_i[..., None])
    er = er.reshape(2, groups, S5_STATE, S5_CHUNK_WIDTH)
    ei = ei.reshape(2, groups, S5_STATE, S5_CHUNK_WIDTH)
    lp_r, lp_i = powers(zr[..., None], zi[..., None], steps)
    xr, xi = cmul(crt[:, :, :, None, :], cit[:, :, :, None, :], lp_r[..., None], lp_i[..., None])
    lhs = jnp.concatenate([bbr, -bbi], axis=-1)
    rhs = jnp.concatenate([xr, xi], axis=2).reshape(2, groups, 2 * S5_STATE, S5_CHUNK_WIDTH)
    lag_kernels = jnp.einsum("dgck,dgkn->dgcn", lhs, rhs, precision=lax.Precision.HIGHEST)
    step_r, step_i = powers(zr, zi, float(t_len))

    w_state = jnp.concatenate([wr[0], wr[1], wi[0], wi[1]], axis=-1)
    m_state = jnp.concatenate([er[0], er[1], -ei[0], -ei[1]], axis=1)
    kf, kb = lag_kernels[0], lag_kernels[1]
    kb_rev = kb[:, :, S5_GROUP:].reshape(groups, S5_GROUP, t_len - 1, S5_GROUP)[:, :, ::-1]
    lag0 = kf[:, :, :S5_GROUP] + kb[:, :, :S5_GROUP] + d.astype(F32)[:, None, :] * jnp.eye(S5_GROUP, dtype=F32)[None]
    lags = jnp.concatenate([kb_rev.reshape(groups, S5_GROUP, -1), lag0, kf[:, :, S5_GROUP:]], axis=-1)
    toe = jnp.stack([lags[:, :, S5_GROUP * (t_len - 1 - s):S5_GROUP * (2 * t_len - 1 - s)] for s in range(t_len)],
                    axis=1).reshape(groups, S5_CHUNK_WIDTH, S5_CHUNK_WIDTH)
    m_out = jnp.concatenate([toe, m_state], axis=1)
    return (w_state.astype(BF16), m_out.astype(BF16),
            jnp.concatenate([step_r[0], step_r[1]], axis=-1), jnp.concatenate([step_i[0], step_i[1]], axis=-1))


def _lane_group_ids():
    return lax.broadcasted_iota(jnp.int32, (RELAYOUT_ROWS, LANES), 1) // S5_GROUP


def _transpose_lane_blocks(pieces, grp):
    nblk = S5_LANE_GROUPS
    rolled = []
    for r in range(nblk):
        acc = None
        for b in range(nblk):
            acc = pieces[b] if acc is None else jnp.where(grp == (b - r) % nblk, pieces[b], acc)
        rolled.append(pltpu.roll(acc, S5_GROUP * r, 1) if r else acc)
    out = []
    for k in range(nblk):
        acc = None
        for b in range(nblk):
            z = rolled[(b - k) % nblk]
            acc = z if acc is None else jnp.where(grp == b, z, acc)
        out.append(acc)
    return out


def _s5_state_kernel(u_ref, w_ref, ar_ref, ai_ref, ug_ref, h_ref, s_ref):
    lg = S5_LANE_GROUPS
    nc = ug_ref.shape[1]
    grp = _lane_group_ids()

    def regroup(rb, carry):
        r0 = pl.multiple_of(rb * RELAYOUT_ROWS, RELAYOUT_ROWS)
        for half in range(S5_CHUNK_WIDTH // LANES):
            steps = [u_ref[pl.ds(r0 * S5_CHUNK + half * S5_LANE_GROUPS + tp, RELAYOUT_ROWS, stride=S5_CHUNK), :]
                     for tp in range(S5_LANE_GROUPS)]
            for g, v in enumerate(_transpose_lane_blocks(steps, grp)):
                ug_ref[g, pl.ds(r0, RELAYOUT_ROWS), half * LANES:(half + 1) * LANES] = v.astype(BF16)
        return carry

    lax.fori_loop(0, nc // RELAYOUT_ROWS, regroup, 0, unroll=2)

    half = S5_STATE_WIDTH // 2
    for g in range(lg):
        sg = jnp.dot(ug_ref[g], w_ref[g], preferred_element_type=F32)
        s_ref[0, pl.ds(g, nc, stride=lg), :] = sg[:, :half]
        s_ref[1, pl.ds(g, nc, stride=lg), :] = sg[:, half:]

    lane = lax.broadcasted_iota(jnp.int32, (lg, half), 1)
    fwd = lane < S5_STATE
    ar = ar_ref[...]
    ai = ai_ref[...]
    rows = lambda i: pl.ds(pl.multiple_of(i * lg, lg), lg)

    def advance(i, hr, hi):
        sr = jnp.where(fwd, s_ref[0, rows(i), :], s_ref[0, rows(nc - 1 - i), :])
        si = jnp.where(fwd, s_ref[1, rows(i), :], s_ref[1, rows(nc - 1 - i), :])
        return ar * hr - ai * hi + sr, ar * hi + ai * hr + si

    def first_touch(i, carry):
        hr, hi = carry
        for c in (i, nc - 1 - i):
            h_ref[0, rows(c), :] = hr
            h_ref[1, rows(c), :] = hi
        return advance(i, hr, hi)

    def second_touch(i, carry):
        hr, hi = carry
        j = nc - 1 - i
        h_ref[0, rows(i), :] = jnp.where(fwd, hr, h_ref[0, rows(i), :])
        h_ref[1, rows(i), :] = jnp.where(fwd, hi, h_ref[1, rows(i), :])
        h_ref[0, rows(j), :] = jnp.where(fwd, h_ref[0, rows(j), :], hr)
        h_ref[1, rows(j), :] = jnp.where(fwd, h_ref[1, rows(j), :], hi)
        return advance(i, hr, hi)

    zero = jnp.zeros((lg, half), F32)
    carry = lax.fori_loop(0, nc // 2, first_touch, (zero, zero))
    lax.fori_loop(nc // 2, nc, second_touch, carry)


def _s5_states(u, w_state, a_re, a_im):
    seq, width = u.shape
    g = width // S5_GROUP
    nc = seq // S5_CHUNK
    lg = S5_LANE_GROUPS
    half = S5_STATE_WIDTH // 2
    return pl.pallas_call(
        _s5_state_kernel,
        grid=(width // LANES,),
        in_specs=[pl.BlockSpec((seq, LANES), lambda j: (0, j)),
                  pl.BlockSpec((lg, S5_CHUNK_WIDTH, S5_STATE_WIDTH), lambda j: (j, 0, 0)),
                  pl.BlockSpec((lg, half), lambda j: (j, 0)),
                  pl.BlockSpec((lg, half), lambda j: (j, 0))],
        out_specs=[pl.BlockSpec((lg, nc, S5_CHUNK_WIDTH), lambda j: (j, 0, 0)),
                   pl.BlockSpec((None, 2, nc * lg, half), lambda j: (j, 0, 0, 0))],
        out_shape=[jax.ShapeDtypeStruct((g, nc, S5_CHUNK_WIDTH), BF16),
                   jax.ShapeDtypeStruct((g // lg, 2, nc * lg, half), F32)],
        scratch_shapes=[pltpu.VMEM((2, nc * lg, half), F32)],
        compiler_params=_params(("parallel",)),
        name="s5_states",
    )(u, w_state, a_re, a_im)


def _s5_out_kernel(ug_ref, h_ref, m_ref, o_ref, y_ref):
    cw = S5_CHUNK_WIDTH
    nch = ug_ref.shape[1]
    for g in range(S5_LANE_GROUPS):
        hg = jnp.concatenate([h_ref[part, pl.ds(g, nch, stride=S5_LANE_GROUPS), :] for part in range(2)],
                             axis=1).astype(BF16)
        y = jnp.dot(ug_ref[g], m_ref[g, :cw, :], preferred_element_type=F32)
        y = y + jnp.dot(hg, m_ref[g, cw:, :], preferred_element_type=F32)
        y_ref[g] = jax.nn.gelu(y, approximate=True)
    grp = _lane_group_ids()

    def regroup(rb, carry):
        r0 = pl.multiple_of(rb * RELAYOUT_ROWS, RELAYOUT_ROWS)
        for half in range(cw // LANES):
            groups = [y_ref[g, pl.ds(r0, RELAYOUT_ROWS), half * LANES:(half + 1) * LANES]
                      for g in range(S5_LANE_GROUPS)]
            for tp, v in enumerate(_transpose_lane_blocks(groups, grp)):
                t = half * S5_LANE_GROUPS + tp
                o_ref[pl.ds(r0 * S5_CHUNK + t, RELAYOUT_ROWS, stride=S5_CHUNK), :] = v
        return carry

    lax.fori_loop(0, nch // RELAYOUT_ROWS, regroup, 0, unroll=2)


def _s5_outputs(ug, h3, m_out, nsplit):
    g, nc, cw = ug.shape
    nch = nc // nsplit
    lg = S5_LANE_GROUPS
    return pl.pallas_call(
        _s5_out_kernel,
        grid=(g // lg, nsplit),
        in_specs=[pl.BlockSpec((lg, nch, cw), lambda j, h: (j, h, 0)),
                  pl.BlockSpec((None, 2, nch * lg, S5_STATE_WIDTH // 2), lambda j, h: (j, 0, h, 0)),
                  pl.BlockSpec((lg, cw + S5_STATE_WIDTH, cw), lambda j, h: (j, 0, 0))],
        out_specs=pl.BlockSpec((nch * S5_CHUNK, LANES), lambda j, h: (h, j)),
        out_shape=jax.ShapeDtypeStruct((nc * S5_CHUNK, g * S5_GROUP), F32),
        scratch_shapes=[pltpu.VMEM((lg, nch, cw), F32)],
        compiler_params=_params(("parallel", "parallel")),
        name="s5_outputs",
    )(ug, h3, m_out)


def _merge_kernel(a1_ref, w1_ref, ys_ref, wg_ref, bg_ref, w2_ref, g1_ref, g2_ref, o_ref, ysb_ref, z_ref, *, tg):
    @pl.when(pl.program_id(1) == 0)
    def _():
        ysb_ref[...] = ys_ref[...].astype(BF16)
        for c in range(z_ref.shape[1] // tg):
            cols = slice(c * tg, (c + 1) * tg)
            t = jnp.dot(ysb_ref[...], wg_ref[:, cols], preferred_element_type=F32) + bg_ref[:, cols]
            z_ref[:, cols] = (ys_ref[:, cols] * jax.nn.sigmoid(t)).astype(BF16)

    y1 = jnp.dot(a1_ref[...], w1_ref[...], preferred_element_type=F32)
    y2 = jnp.dot(z_ref[...], w2_ref[...], preferred_element_type=F32)
    o_ref[...] = (g1_ref[...].astype(F32) * y1 + g2_ref[...].astype(F32) * y2).astype(o_ref.dtype)


def _glu_merge(att, w_na, ys, w_glu, b_glu, w_s5, gates, tm, tn):
    m, k1 = att.shape
    k2 = ys.shape[1]
    n = w_na.shape[1]
    g1 = 0
    g2 = n // tn
    return pl.pallas_call(
        functools.partial(_merge_kernel, tg=tn),
        grid=(m // tm, n // tn),
        in_specs=[pl.BlockSpec((tm, k1), lambda i, j: (i, 0)),
                  pl.BlockSpec((k1, tn), lambda i, j: (0, j)),
                  pl.BlockSpec((tm, k2), lambda i, j: (i, 0)),
                  pl.BlockSpec((k2, k2), lambda i, j: (0, 0)),
                  pl.BlockSpec((1, k2), lambda i, j: (0, 0)),
                  pl.BlockSpec((k2, tn), lambda i, j: (0, j)),
                  pl.BlockSpec((tm, tn), lambda i, j: (i, g1 + j)),
                  pl.BlockSpec((tm, tn), lambda i, j: (i, g2 + j))],
        out_specs=pl.BlockSpec((tm, tn), lambda i, j: (i, j)),
        out_shape=jax.ShapeDtypeStruct((m, n), BF16),
        scratch_shapes=[pltpu.VMEM((tm, k2), BF16), pltpu.VMEM((tm, k2), BF16)],
        compiler_params=_params(("parallel", "arbitrary")),
        name="glu_gated_merge",
    )(att, w_na, ys, w_glu, b_glu, w_s5, gates, gates)


def _matmul_residual_ln(a_ref, w_ref, res_ref, g_ref, b_ref, o_ref, alpha, ob_ref=None):
    sub = a_ref.shape[0] // LN_ROW_SPLIT
    for r in range(LN_ROW_SPLIT):
        rows = slice(r * sub, (r + 1) * sub)
        y = jnp.dot(a_ref[rows, :], w_ref[...], preferred_element_type=F32)
        out = _layer_norm_rows(alpha * res_ref[rows, :] + y, g_ref[...], b_ref[...])
        o_ref[rows, :] = out
        if ob_ref is not None:
            ob_ref[rows, :] = out.astype(ob_ref.dtype)


def _outproj_ln_kernel(a_ref, w_ref, x_ref, g_ref, b_ref, h_ref, hb_ref, *, alpha):
    _matmul_residual_ln(a_ref, w_ref, x_ref, g_ref, b_ref, h_ref, alpha, hb_ref)


def _outproj_ln(a, w, x, g, b, alpha, tm):
    m, k = a.shape
    n = w.shape[1]
    return pl.pallas_call(
        functools.partial(_outproj_ln_kernel, alpha=alpha),
        grid=(m // tm,),
        in_specs=[pl.BlockSpec((tm, k), lambda i: (i, 0)),
                  pl.BlockSpec((k, n), lambda i: (0, 0)),
                  pl.BlockSpec((tm, n), lambda i: (i, 0)),
                  pl.BlockSpec((1, n), lambda i: (0, 0)),
                  pl.BlockSpec((1, n), lambda i: (0, 0))],
        out_specs=[pl.BlockSpec((tm, n), lambda i: (i, 0)),
                   pl.BlockSpec((tm, n), lambda i: (i, 0))],
        out_shape=[jax.ShapeDtypeStruct((m, n), F32), jax.ShapeDtypeStruct((m, n), BF16)],
        compiler_params=_params(("parallel",)),
        name="outproj_ln1",
    )(a, w, x, g, b)


def _ffn_act_kernel(hb_ref, wg_ref, wu_ref, o_ref):
    hb = hb_ref[...]
    gate = jnp.dot(hb, wg_ref[...], preferred_element_type=F32)
    up = jnp.dot(hb, wu_ref[...], preferred_element_type=F32)
    o_ref[...] = (jax.nn.silu(gate) * up).astype(o_ref.dtype)


def _ffn_act(hb, wg, wu, tm, tf):
    m, d = hb.shape
    f = wg.shape[1]
    return pl.pallas_call(
        _ffn_act_kernel,
        grid=(m // tm, f // tf),
        in_specs=[pl.BlockSpec((tm, d), lambda i, j: (i, 0)),
                  pl.BlockSpec((d, tf), lambda i, j: (0, j)),
                  pl.BlockSpec((d, tf), lambda i, j: (0, j))],
        out_specs=pl.BlockSpec((tm, tf), lambda i, j: (i, j)),
        out_shape=jax.ShapeDtypeStruct((m, f), BF16),
        compiler_params=_params(("parallel", "arbitrary")),
        name="swiglu_act",
    )(hb, wg, wu)


def _ffn_down_ln_kernel(a_ref, w_ref, h_ref, g_ref, b_ref, o_ref, *, alpha):
    _matmul_residual_ln(a_ref, w_ref, h_ref, g_ref, b_ref, o_ref, alpha)


def _ffn_down_ln(act, wd, h, g, b, alpha, tm):
    m, f = act.shape
    d = wd.shape[1]
    return pl.pallas_call(
        functools.partial(_ffn_down_ln_kernel, alpha=alpha),
        grid=(m // tm,),
        in_specs=[pl.BlockSpec((tm, f), lambda i: (i, 0)),
                  pl.BlockSpec((f, d), lambda i: (0, 0), pipeline_mode=pl.Buffered(1)),
                  pl.BlockSpec((tm, d), lambda i: (i, 0)),
                  pl.BlockSpec((1, d), lambda i: (0, 0)),
                  pl.BlockSpec((1, d), lambda i: (0, 0))],
        out_specs=pl.BlockSpec((tm, d), lambda i: (i, 0)),
        out_shape=jax.ShapeDtypeStruct((m, d), F32),
        compiler_params=_params(("parallel",)),
        name="swiglu_down_ln2",
    )(act, wd, h, g, b)


def _layer(x2, w_in, b_gate, na_rpb, w_na_out, s5_a_re, s5_a_im, s5_log_dt, s5_b_re, s5_b_im,
           s5_c_re, s5_c_im, s5_d, w_glu, b_glu, w_s5_out, w_out, ln1_g, ln1_b,
           w_ffn_gate, w_ffn_up, w_ffn_down, ln2_g, ln2_b, alpha):
    seq, d_model = x2.shape
    rows = seq // GRID_W
    nc = seq // S5_CHUNK
    gate_col0 = 3 * NA_WIDTH + S5_WIDTH
    in_cols = gate_col0 + 2 * d_model
    row = lambda v: v.astype(F32).reshape(1, -1)

    col_scale = jnp.where(jnp.arange(in_cols) < NA_WIDTH, NA_HEAD_DIM ** -0.5, 1.0).astype(F32).reshape(1, -1)
    col_bias = jnp.concatenate([jnp.zeros((gate_col0,), F32), b_gate.astype(F32)]).reshape(1, -1)
    qkv, u, gates = _inproj(x2, w_in.astype(BF16), col_scale, col_bias, 3 * NA_WIDTH, gate_col0, tm=1024, tn=1024)

    later_weights = [w.astype(F32) for w in (w_na_out, w_glu, w_s5_out, w_out, w_ffn_gate, w_ffn_up, w_ffn_down)]
    att, (w_na_b, w_glu_b, w_s5_b, w_out_b, w_gate_b, w_up_b, w_down_b) = _attention(
        qkv, _attention_bias(na_rpb, rows), seq, later_weights)

    w_state, m_out, a_re, a_im = _s5_tables(s5_a_re, s5_a_im, s5_log_dt, s5_b_re, s5_b_im,
                                            s5_c_re, s5_c_im, s5_d)
    ug, h_states = _s5_states(u, w_state, a_re, a_im)
    ys = _s5_outputs(ug, h_states, m_out, nsplit=2)

    merged = _glu_merge(att, w_na_b, ys, w_glu_b, row(b_glu), w_s5_b, gates, tm=1024, tn=512)
    h, hb = _outproj_ln(merged, w_out_b, x2, row(ln1_g), row(ln1_b), alpha, tm=512)
    act = _ffn_act(hb, w_gate_b, w_up_b, tm=1024, tf=512)
    return _ffn_down_ln(act, w_down_b, h, row(ln2_g), row(ln2_b), alpha, tm=512)


def kernel(x, w_in, b_gate, na_rpb, w_na_out, s5_a_re, s5_a_im, s5_log_dt, s5_b_re, s5_b_im, s5_c_re, s5_c_im, s5_d, w_glu, b_glu, w_s5_out, w_out, ln1_g, ln1_b, w_ffn_gate, w_ffn_up, w_ffn_down, ln2_g, ln2_b):
    bsz, seq, d_model = x.shape
    depth = w_in.shape[0]
    alpha = (2.0 * depth) ** 0.25
    outs = []
    for bi in range(bsz):
        xb = x[bi]
        for l in range(depth):
            xb = _layer(xb, w_in[l], b_gate[l], na_rpb[l], w_na_out[l], s5_a_re[l], s5_a_im[l], s5_log_dt[l],
                        s5_b_re[l], s5_b_im[l], s5_c_re[l], s5_c_im[l], s5_d[l], w_glu[l], b_glu[l],
                        w_s5_out[l], w_out[l], ln1_g[l], ln1_b[l], w_ffn_gate[l], w_ffn_up[l],
                        w_ffn_down[l], ln2_g[l], ln2_b[l], alpha)
        outs.append(xb)
    return jnp.stack(outs)
```

```python
import functools
import math

import jax
import jax.numpy as jnp
from jax import lax
from jax.experimental import pallas as pl
from jax.experimental.pallas import tpu as pltpu

F32 = jnp.float32
BF16 = jnp.bfloat16

GRID_W = 64
NA_HEADS = 8
NA_HEAD_DIM = 128
NA_WIDTH = NA_HEADS * NA_HEAD_DIM
NA_ROWS = 8
NA_COLS = 16
S5_GROUP = 16
S5_GROUPS = 64
S5_WIDTH = S5_GROUP * S5_GROUPS
S5_STATE = 64
LN_EPS = 1e-5
MASK_VALUE = -1e30

S5_CHUNK = 16
S5_CHUNK_WIDTH = S5_CHUNK * S5_GROUP
S5_STATE_WIDTH = 4 * S5_STATE
SCAN_GROUPS = 8

ATT_Q_ROWS = 8
ATT_K_ROWS = 16
ATT_Q = ATT_Q_ROWS * GRID_W
ATT_K = ATT_K_ROWS * GRID_W
ATT_KBLK = 256
LANES = 128
ATT_WIN = NA_ROWS * GRID_W + LANES
ATT_HEADS_PER_STEP = 4
ATT_ROW_SPLIT = 2
BF16_SUBLANES = 16
LN_ROW_SPLIT = 4
S5_LANE_GROUPS = LANES // S5_GROUP
RELAYOUT_ROWS = 16

VMEM_LIMIT = 56 * 1024 * 1024


def _params(sem):
    return pltpu.CompilerParams(dimension_semantics=sem, vmem_limit_bytes=VMEM_LIMIT)


def _layer_norm_rows(y, g, b):
    mu = jnp.mean(y, axis=-1, keepdims=True)
    d = y - mu
    var = jnp.mean(d * d, axis=-1, keepdims=True)
    return d * lax.rsqrt(var + LN_EPS) * g + b


def _inproj_kernel(x_ref, w_ref, s_ref, b_ref, qkv_ref, u_ref, g_ref, xb_ref, *, u_tile0, gate_tile0):
    j = pl.program_id(1)

    @pl.when(j == 0)
    def _():
        xb_ref[...] = x_ref[...].astype(BF16)

    @pl.when(j < u_tile0)
    def _():
        acc = (jnp.dot(xb_ref[...], w_ref[...], preferred_element_type=F32) * s_ref[...]).astype(qkv_ref.dtype)
        for h in range(qkv_ref.shape[0]):
            qkv_ref[h] = acc[:, h * NA_HEAD_DIM:(h + 1) * NA_HEAD_DIM]

    @pl.when((j >= u_tile0) & (j < gate_tile0))
    def _():
        u_ref[...] = jnp.dot(xb_ref[...], w_ref[...], preferred_element_type=F32)

    @pl.when(j >= gate_tile0)
    def _():
        acc = jnp.dot(xb_ref[...], w_ref[...], preferred_element_type=F32)
        g_ref[...] = (acc + b_ref[...]).astype(g_ref.dtype)


def _inproj(x, w, col_scale, col_bias, u_col0, gate_col0, tm, tn):
    m, k = x.shape
    n = w.shape[1]
    u_tile0, gate_tile0 = u_col0 // tn, gate_col0 // tn
    heads_per_tile = tn // NA_HEAD_DIM
    last_u = gate_tile0 - u_tile0 - 1
    last_g = n // tn - gate_tile0 - 1
    return pl.pallas_call(
        functools.partial(_inproj_kernel, u_tile0=u_tile0, gate_tile0=gate_tile0),
        grid=(m // tm, n // tn),
        in_specs=[
            pl.BlockSpec((tm, k), lambda i, j: (i, 0)),
            pl.BlockSpec((k, tn), lambda i, j: (0, j)),
            pl.BlockSpec((1, tn), lambda i, j: (0, j)),
            pl.BlockSpec((1, tn), lambda i, j: (0, j)),
        ],
        out_specs=[pl.BlockSpec((heads_per_tile, tm, NA_HEAD_DIM), lambda i, j: (jnp.minimum(j, u_tile0 - 1), i, 0)),
                   pl.BlockSpec((tm, tn), lambda i, j: (i, jnp.clip(j - u_tile0, 0, last_u))),
                   pl.BlockSpec((tm, tn), lambda i, j: (i, jnp.clip(j - gate_tile0, 0, last_g)))],
        out_shape=[jax.ShapeDtypeStruct((u_col0 // NA_HEAD_DIM, m, NA_HEAD_DIM), BF16),
                   jax.ShapeDtypeStruct((m, gate_col0 - u_col0), F32),
                   jax.ShapeDtypeStruct((m, n - gate_col0), BF16)],
        scratch_shapes=[pltpu.VMEM((tm, k), BF16)],
        compiler_params=_params(("parallel", "arbitrary")),
        name="inproj",
    )(x, w, col_scale, col_bias)


def _attn_windows(rows):
    nb = rows // ATT_Q_ROWS
    kinds = []
    for b in (0, 1, nb - 1):
        base = min(max(b * ATT_Q_ROWS - NA_ROWS // 2, 0), rows - ATT_K_ROWS)
        geo = []
        for ql in range(ATT_Q_ROWS):
            qr = b * ATT_Q_ROWS + ql
            r0 = min(max(qr - NA_ROWS // 2, 0), rows - NA_ROWS)
            koff = r0 - base
            lane0 = min(LANES * (koff * GRID_W // LANES), ATT_K - ATT_WIN)
            geo.append((lane0, koff * GRID_W - lane0, r0 - qr + NA_ROWS - 1))
        kinds.append(tuple(geo))
    return tuple(kinds)


def _attn_kernel(q_ref, k0, k1, k2, k3, v0, v1, v2, v3, bias_ref, *rest, windows, nb, n_cast):
    cast_src, o_ref, cast_dst, p_ref = rest[:n_cast], rest[n_cast], rest[n_cast + 1:2 * n_cast + 1], rest[-1]
    b = pl.program_id(1)
    k_refs, v_refs = (k0, k1, k2, k3), (v0, v1, v2, v3)

    for src, dst in zip(cast_src, cast_dst):
        dst[...] = src[...].astype(dst.dtype)

    def one_head(hh, geo):
        c = slice(hh * NA_HEAD_DIM, (hh + 1) * NA_HEAD_DIM)
        k = jnp.concatenate([r[hh] for r in k_refs], axis=0)
        v = jnp.concatenate([r[hh] for r in v_refs], axis=0)
        per = ATT_Q_ROWS // ATT_ROW_SPLIT
        for part in range(ATT_ROW_SPLIT):
            sub = geo[part * per:(part + 1) * per]
            lo = min(lane0 for lane0, _, _ in sub)
            hi = max(lane0 for lane0, _, _ in sub) + ATT_WIN
            rows = slice(part * per * GRID_W, (part + 1) * per * GRID_W)
            s = lax.dot_general(q_ref[hh, rows, :], k[lo:hi], (((1,), (1,)), ((), ())),
                                preferred_element_type=F32)
            inv = []
            for i, (lane0, _, _) in enumerate(sub):
                ql = part * per + i
                r = slice(ql * GRID_W, (ql + 1) * GRID_W)
                sw = s[i * GRID_W:(i + 1) * GRID_W, lane0 - lo:lane0 - lo + ATT_WIN] + bias_ref[hh, ql]
                m = jnp.max(sw, axis=-1, keepdims=True)
                p = jnp.exp(sw - m)
                inv.append(1.0 / jnp.sum(p, axis=-1, keepdims=True))
                if lane0 > lo:
                    p_ref[hh, r, lo:lane0] = jnp.zeros((GRID_W, lane0 - lo), BF16)
                p_ref[hh, r, lane0:lane0 + ATT_WIN] = p.astype(BF16)
                if lane0 + ATT_WIN < hi:
                    p_ref[hh, r, lane0 + ATT_WIN:hi] = jnp.zeros((GRID_W, hi - lane0 - ATT_WIN), BF16)
            o = jnp.dot(p_ref[hh, rows, lo:hi], v[lo:hi], preferred_element_type=F32)
            o_ref[rows, c] = (o * jnp.concatenate(inv, axis=0)).astype(o_ref.dtype)

    def block_kind(geo):
        for hh in range(ATT_HEADS_PER_STEP):
            one_head(hh, geo)

    @pl.when(b == 0)
    def _():
        block_kind(windows[0])

    @pl.when((b > 0) & (b < nb - 1))
    def _():
        block_kind(windows[1])

    @pl.when(b == nb - 1)
    def _():
        block_kind(windows[2])


def _attention(qkv, bias, seq, weights):
    nb = seq // ATT_Q
    n_kblk = seq // ATT_KBLK
    per_blk = ATT_K // ATT_KBLK

    def kstart(b):
        return jnp.clip(2 * b - 1, 0, n_kblk - per_blk)

    def kind(b):
        return jnp.where(b == 0, 0, jnp.where(b == nb - 1, 2, 1))

    hps = ATT_HEADS_PER_STEP
    width = hps * NA_HEAD_DIM
    n_groups = NA_HEADS // hps

    def kv_spec(group0, t):
        return pl.BlockSpec((hps, ATT_KBLK, NA_HEAD_DIM), lambda h, b: (group0 + h, kstart(b) + t, 0))

    in_specs = [pl.BlockSpec((hps, ATT_Q, NA_HEAD_DIM), lambda h, b: (h, b, 0))]
    in_specs += [kv_spec(n_groups, t) for t in range(per_blk)]
    in_specs += [kv_spec(2 * n_groups, t) for t in range(per_blk)]
    in_specs += [pl.BlockSpec((None, hps, ATT_Q_ROWS, GRID_W, ATT_WIN), lambda h, b: (kind(b), h, 0, 0, 0))]

    n_steps = n_groups * nb

    def slab_spec(w):
        rows_w, cols_w = w.shape
        for rb in range(n_steps, 0, -1):
            cb = n_steps // rb
            if (rb * cb == n_steps and rows_w % rb == 0 and cols_w % cb == 0
                    and (rows_w // rb) % BF16_SUBLANES == 0 and (cols_w // cb) % LANES == 0):
                return pl.BlockSpec((rows_w // rb, cols_w // cb),
                                    lambda h, b, cb=cb: ((h * nb + b) // cb, (h * nb + b) % cb))
        raise ValueError(f"no slab tiling for {w.shape} over {n_steps} steps")

    slab_specs = [slab_spec(w) for w in weights]
    outs = pl.pallas_call(
        functools.partial(_attn_kernel, windows=_attn_windows(seq // GRID_W), nb=nb, n_cast=len(weights)),
        grid=(n_groups, nb),
        in_specs=in_specs + slab_specs,
        out_specs=[pl.BlockSpec((ATT_Q, width), lambda h, b: (b, h))] + slab_specs,
        out_shape=[jax.ShapeDtypeStruct((seq, NA_WIDTH), BF16)]
                  + [jax.ShapeDtypeStruct(w.shape, BF16) for w in weights],
        scratch_shapes=[pltpu.VMEM((hps, ATT_Q, ATT_K), BF16)],
        compiler_params=_params(("parallel", "arbitrary")),
        name="na_attention",
    )(*([qkv] * (1 + 2 * per_blk)), bias, *weights)
    return outs[0], outs[1:]


def _attention_bias(rpb, rows):
    cols = jnp.arange(GRID_W)
    col_start = jnp.clip(cols - NA_COLS // 2, 0, GRID_W - NA_COLS)
    col_ok = (cols[None, :] >= col_start[:, None]) & (cols[None, :] < col_start[:, None] + NA_COLS)
    col_idx = jnp.clip(cols[None, :] - cols[:, None], -(NA_COLS - 1), NA_COLS - 1) + (NA_COLS - 1)
    pick = (jnp.arange(2 * NA_COLS - 1)[:, None, None] == col_idx[None]).astype(F32)
    by_col = jnp.einsum("hrj,jqk->hqrk", rpb.astype(F32), pick, precision=lax.Precision.HIGHEST)
    by_col = jnp.where(col_ok[None, :, None, :], by_col, MASK_VALUE)
    kinds, made = [], {}
    for geo in _attn_windows(rows):
        strips = []
        for _, phase, rfirst in geo:
            if (phase, rfirst) not in made:
                strip = by_col[:, :, rfirst:rfirst + NA_ROWS, :].reshape(NA_HEADS, GRID_W, NA_ROWS * GRID_W)
                made[phase, rfirst] = jnp.pad(strip, ((0, 0), (0, 0), (phase, ATT_WIN - NA_ROWS * GRID_W - phase)),
                                              constant_values=MASK_VALUE)
            strips.append(made[phase, rfirst])
        kinds.append(jnp.stack(strips, axis=1))
    return jnp.stack(kinds)


def _s5_tables(a_re, a_im, log_dt, b_re, b_im, c_re, c_im, d):
    t_len = S5_CHUNK
    groups = a_re.shape[1]
    steps = jnp.arange(t_len, dtype=F32)
    flip = t_len - 1 - steps

    def cmul(xr, xi, yr, yi):
        return xr * yr - xi * yi, xr * yi + xi * yr

    def powers(zr, zi, n):
        mag = jnp.exp(zr * n)
        return mag * jnp.cos(zi * n), mag * jnp.sin(zi * n)

    ar, ai = a_re.astype(F32), a_im.astype(F32)
    dt = jnp.exp(log_dt.astype(F32))[..., None]
    zr, zi = ar * dt, ai * dt
    lr, li = powers(zr, zi, 1.0)
    lr = lr - 1.0
    den = ar * ar + ai * ai
    fr, fi = (lr * ar + li * ai) / den, (li * ar - lr * ai) / den
    swap = lambda v: jnp.swapaxes(v.astype(F32), 2, 3)
    bbr, bbi = cmul(fr[:, :, None, :], fi[:, :, None, :], swap(b_re), swap(b_im))
    crt, cit = swap(c_re), swap(c_im)
    n_w = jnp.stack([flip, steps])[:, None, :, None]
    wp_r, wp_i = powers(zr[:, :, None, :], zi[:, :, None, :], n_w)
    wr, wi = cmul(wp_r[:, :, :, None, :], wp_i[:, :, :, None, :], bbr[:, :, None], bbi[:, :, None])
    wr = wr.reshape(2, groups, S5_CHUNK_WIDTH, S5_STATE)
    wi = wi.reshape(2, groups, S5_CHUNK_WIDTH, S5_STATE)
    n_e = jnp.stack([steps + 1.0, flip + 1.0])[:, None, None, :]
    ep_r, ep_i = powers(zr[..., None], zi[..., None], n_e)
    er, ei = cmul(crt[:, :, :, None, :], cit[:, :, :, None, :], ep_r[..., None], ep_i[..., None])
    er = er.reshape(2, groups, S5_STATE, S5_CHUNK_WIDTH)
    ei = ei.reshape(2, groups, S5_STATE, S5_CHUNK_WIDTH)
    lp_r, lp_i = powers(zr[..., None], zi[..., None], steps)
    xr, xi = cmul(crt[:, :, :, None, :], cit[:, :, :, None, :], lp_r[..., None], lp_i[..., None])
    lhs = jnp.concatenate([bbr, -bbi], axis=-1)
    rhs = jnp.concatenate([xr, xi], axis=2).reshape(2, groups, 2 * S5_STATE, S5_CHUNK_WIDTH)
    lag_kernels = jnp.einsum("dgck,dgkn->dgcn", lhs, rhs, precision=lax.Precision.HIGHEST)
    step_r, step_i = powers(zr, zi, float(t_len))

    w_state = jnp.concatenate([wr[0], wr[1], wi[0], wi[1]], axis=-1)
    m_state = jnp.concatenate([er[0], er[1], -ei[0], -ei[1]], axis=1)
    kf, kb = lag_kernels[0], lag_kernels[1]
    kb_rev = kb[:, :, S5_GROUP:].reshape(groups, S5_GROUP, t_len - 1, S5_GROUP)[:, :, ::-1]
    lag0 = kf[:, :, :S5_GROUP] + kb[:, :, :S5_GROUP] + d.astype(F32)[:, None, :] * jnp.eye(S5_GROUP, dtype=F32)[None]
    lags = jnp.concatenate([kb_rev.reshape(groups, S5_GROUP, -1), lag0, kf[:, :, S5_GROUP:]], axis=-1)
    return (w_state.astype(BF16), lags, m_state.astype(BF16),
            jnp.concatenate([step_r[0], step_r[1]], axis=-1), jnp.concatenate([step_i[0], step_i[1]], axis=-1))


def _lane_group_ids():
    return lax.broadcasted_iota(jnp.int32, (RELAYOUT_ROWS, LANES), 1) // S5_GROUP


def _transpose_lane_blocks(pieces, grp):
    nblk = S5_LANE_GROUPS
    rolled = []
    for r in range(nblk):
        acc = None
        for b in range(nblk):
            acc = pieces[b] if acc is None else jnp.where(grp == (b - r) % nblk, pieces[b], acc)
        rolled.append(pltpu.roll(acc, S5_GROUP * r, 1) if r else acc)
    out = []
    for k in range(nblk):
        acc = None
        for b in range(nblk):
            z = rolled[(b - k) % nblk]
            acc = z if acc is None else jnp.where(grp == b, z, acc)
        out.append(acc)
    return out


def _s5_state_kernel(u_ref, w_ref, ar_ref, ai_ref, ug_ref, h_ref, s_ref):
    lg = S5_LANE_GROUPS
    nc = ug_ref.shape[1]
    grp = _lane_group_ids()

    def regroup(rb, carry):
        r0 = pl.multiple_of(rb * RELAYOUT_ROWS, RELAYOUT_ROWS)
        for half in range(S5_CHUNK_WIDTH // LANES):
            steps = [u_ref[pl.ds(r0 * S5_CHUNK + half * S5_LANE_GROUPS + tp, RELAYOUT_ROWS, stride=S5_CHUNK), :]
                     for tp in range(S5_LANE_GROUPS)]
            for g, v in enumerate(_transpose_lane_blocks(steps, grp)):
                ug_ref[g, pl.ds(r0, RELAYOUT_ROWS), half * LANES:(half + 1) * LANES] = v.astype(BF16)
        return carry

    lax.fori_loop(0, nc // RELAYOUT_ROWS, regroup, 0, unroll=2)

    half = S5_STATE_WIDTH // 2
    for g in range(lg):
        sg = jnp.dot(ug_ref[g], w_ref[g], preferred_element_type=F32)
        s_ref[0, pl.ds(g, nc, stride=lg), :] = sg[:, :half]
        s_ref[1, pl.ds(g, nc, stride=lg), :] = sg[:, half:]

    lane = lax.broadcasted_iota(jnp.int32, (lg, half), 1)
    fwd = lane < S5_STATE
    ar = ar_ref[...]
    ai = ai_ref[...]
    rows = lambda i: pl.ds(pl.multiple_of(i * lg, lg), lg)

    def advance(i, hr, hi):
        sr = jnp.where(fwd, s_ref[0, rows(i), :], s_ref[0, rows(nc - 1 - i), :])
        si = jnp.where(fwd, s_ref[1, rows(i), :], s_ref[1, rows(nc - 1 - i), :])
        return ar * hr - ai * hi + sr, ar * hi + ai * hr + si

    def first_touch(i, carry):
        hr, hi = carry
        for c in (i, nc - 1 - i):
            h_ref[0, rows(c), :] = hr
            h_ref[1, rows(c), :] = hi
        return advance(i, hr, hi)

    def second_touch(i, carry):
        hr, hi = carry
        j = nc - 1 - i
        h_ref[0, rows(i), :] = jnp.where(fwd, hr, h_ref[0, rows(i), :])
        h_ref[1, rows(i), :] = jnp.where(fwd, hi, h_ref[1, rows(i), :])
        h_ref[0, rows(j), :] = jnp.where(fwd, h_ref[0, rows(j), :], hr)
        h_ref[1, rows(j), :] = jnp.where(fwd, h_ref[1, rows(j), :], hi)
        return advance(i, hr, hi)

    zero = jnp.zeros((lg, half), F32)
    carry = lax.fori_loop(0, nc // 2, first_touch, (zero, zero))
    lax.fori_loop(nc // 2, nc, second_touch, carry)


def _s5_states(u, w_state, a_re, a_im):
    seq, width = u.shape
    g = width // S5_GROUP
    nc = seq // S5_CHUNK
    lg = S5_LANE_GROUPS
    half = S5_STATE_WIDTH // 2
    return pl.pallas_call(
        _s5_state_kernel,
        grid=(width // LANES,),
        in_specs=[pl.BlockSpec((seq, LANES), lambda j: (0, j)),
                  pl.BlockSpec((lg, S5_CHUNK_WIDTH, S5_STATE_WIDTH), lambda j: (j, 0, 0)),
                  pl.BlockSpec((lg, half), lambda j: (j, 0)),
                  pl.BlockSpec((lg, half), lambda j: (j, 0))],
        out_specs=[pl.BlockSpec((lg, nc, S5_CHUNK_WIDTH), lambda j: (j, 0, 0)),
                   pl.BlockSpec((None, 2, nc * lg, half), lambda j: (j, 0, 0, 0))],
        out_shape=[jax.ShapeDtypeStruct((g, nc, S5_CHUNK_WIDTH), BF16),
                   jax.ShapeDtypeStruct((g // lg, 2, nc * lg, half), F32)],
        scratch_shapes=[pltpu.VMEM((2, nc * lg, half), F32)],
        compiler_params=_params(("parallel",)),
        name="s5_states",
    )(u, w_state, a_re, a_im)


def _s5_out_kernel(ug_ref, h_ref, lag_ref, m_ref, o_ref, y_ref, toe_ref):
    cw = S5_CHUNK_WIDTH
    nch = ug_ref.shape[1]
    for g in range(S5_LANE_GROUPS):
        lag = lag_ref[g]
        for s in range(S5_CHUNK):
            first = S5_GROUP * (S5_CHUNK - 1 - s)
            toe_ref[g, s * S5_GROUP:(s + 1) * S5_GROUP, :] = lag[:, first:first + cw].astype(BF16)
    for g in range(S5_LANE_GROUPS):
        hg = jnp.concatenate([h_ref[part, pl.ds(g, nch, stride=S5_LANE_GROUPS), :] for part in range(2)],
                             axis=1).astype(BF16)
        y = jnp.dot(ug_ref[g], toe_ref[g], preferred_element_type=F32)
        y = y + jnp.dot(hg, m_ref[g], preferred_element_type=F32)
        y_ref[g] = jax.nn.gelu(y, approximate=True)
    grp = _lane_group_ids()

    def regroup(rb, carry):
        r0 = pl.multiple_of(rb * RELAYOUT_ROWS, RELAYOUT_ROWS)
        for half in range(cw // LANES):
            groups = [y_ref[g, pl.ds(r0, RELAYOUT_ROWS), half * LANES:(half + 1) * LANES]
                      for g in range(S5_LANE_GROUPS)]
            for tp, v in enumerate(_transpose_lane_blocks(groups, grp)):
                t = half * S5_LANE_GROUPS + tp
                o_ref[pl.ds(r0 * S5_CHUNK + t, RELAYOUT_ROWS, stride=S5_CHUNK), :] = v
        return carry

    lax.fori_loop(0, nch // RELAYOUT_ROWS, regroup, 0, unroll=2)


def _s5_outputs(ug, h3, lags, m_state, nsplit):
    g, nc, cw = ug.shape
    nch = nc // nsplit
    lg = S5_LANE_GROUPS
    return pl.pallas_call(
        _s5_out_kernel,
        grid=(g // lg, nsplit),
        in_specs=[pl.BlockSpec((lg, nch, cw), lambda j, h: (j, h, 0)),
                  pl.BlockSpec((None, 2, nch * lg, S5_STATE_WIDTH // 2), lambda j, h: (j, 0, h, 0)),
                  pl.BlockSpec((lg,) + lags.shape[1:], lambda j, h: (j, 0, 0)),
                  pl.BlockSpec((lg, S5_STATE_WIDTH, cw), lambda j, h: (j, 0, 0))],
        out_specs=pl.BlockSpec((nch * S5_CHUNK, LANES), lambda j, h: (h, j)),
        out_shape=jax.ShapeDtypeStruct((nc * S5_CHUNK, g * S5_GROUP), F32),
        scratch_shapes=[pltpu.VMEM((lg, nch, cw), F32), pltpu.VMEM((lg, cw, cw), BF16)],
        compiler_params=_params(("parallel", "parallel")),
        name="s5_outputs",
    )(ug, h3, lags, m_state)


def _merge_kernel(a1_ref, w1_ref, ys_ref, wg_ref, bg_ref, w2_ref, g1_ref, g2_ref, o_ref, ysb_ref, z_ref, *, tg):
    @pl.when(pl.program_id(1) == 0)
    def _():
        ysb_ref[...] = ys_ref[...].astype(BF16)
        for c in range(z_ref.shape[1] // tg):
            cols = slice(c * tg, (c + 1) * tg)
            t = jnp.dot(ysb_ref[...], wg_ref[:, cols], preferred_element_type=F32) + bg_ref[:, cols]
            z_ref[:, cols] = (ys_ref[:, cols] * jax.nn.sigmoid(t)).astype(BF16)

    y1 = jnp.dot(a1_ref[...], w1_ref[...], preferred_element_type=F32)
    y2 = jnp.dot(z_ref[...], w2_ref[...], preferred_element_type=F32)
    g1 = jax.nn.sigmoid(g1_ref[...].astype(F32))
    g2 = jax.nn.sigmoid(g2_ref[...].astype(F32))
    o_ref[...] = (g1 * y1 + g2 * y2).astype(o_ref.dtype)


def _glu_merge(att, w_na, ys, w_glu, b_glu, w_s5, gates, tm, tn):
    m, k1 = att.shape
    k2 = ys.shape[1]
    n = w_na.shape[1]
    g1 = 0
    g2 = n // tn
    return pl.pallas_call(
        functools.partial(_merge_kernel, tg=tn),
        grid=(m // tm, n // tn),
        in_specs=[pl.BlockSpec((tm, k1), lambda i, j: (i, 0)),
                  pl.BlockSpec((k1, tn), lambda i, j: (0, j)),
                  pl.BlockSpec((tm, k2), lambda i, j: (i, 0)),
                  pl.BlockSpec((k2, k2), lambda i, j: (0, 0)),
                  pl.BlockSpec((1, k2), lambda i, j: (0, 0)),
                  pl.BlockSpec((k2, tn), lambda i, j: (0, j)),
                  pl.BlockSpec((tm, tn), lambda i, j: (i, g1 + j)),
                  pl.BlockSpec((tm, tn), lambda i, j: (i, g2 + j))],
        out_specs=pl.BlockSpec((tm, tn), lambda i, j: (i, j)),
        out_shape=jax.ShapeDtypeStruct((m, n), BF16),
        scratch_shapes=[pltpu.VMEM((tm, k2), BF16), pltpu.VMEM((tm, k2), BF16)],
        compiler_params=_params(("parallel", "arbitrary")),
        name="glu_gated_merge",
    )(att, w_na, ys, w_glu, b_glu, w_s5, gates, gates)


def _matmul_residual_ln(a_ref, w_ref, res_ref, g_ref, b_ref, o_ref, alpha, ob_ref=None):
    sub = a_ref.shape[0] // LN_ROW_SPLIT
    for r in range(LN_ROW_SPLIT):
        rows = slice(r * sub, (r + 1) * sub)
        y = jnp.dot(a_ref[rows, :], w_ref[...], preferred_element_type=F32)
        out = _layer_norm_rows(alpha * res_ref[rows, :] + y, g_ref[...], b_ref[...])
        o_ref[rows, :] = out
        if ob_ref is not None:
            ob_ref[rows, :] = out.astype(ob_ref.dtype)


def _outproj_ln_kernel(a_ref, w_ref, x_ref, g_ref, b_ref, h_ref, hb_ref, *, alpha):
    _matmul_residual_ln(a_ref, w_ref, x_ref, g_ref, b_ref, h_ref, alpha, hb_ref)


def _outproj_ln(a, w, x, g, b, alpha, tm):
    m, k = a.shape
    n = w.shape[1]
    return pl.pallas_call(
        functools.partial(_outproj_ln_kernel, alpha=alpha),
        grid=(m // tm,),
        in_specs=[pl.BlockSpec((tm, k), lambda i: (i, 0)),
                  pl.BlockSpec((k, n), lambda i: (0, 0)),
                  pl.BlockSpec((tm, n), lambda i: (i, 0)),
                  pl.BlockSpec((1, n), lambda i: (0, 0)),
                  pl.BlockSpec((1, n), lambda i: (0, 0))],
        out_specs=[pl.BlockSpec((tm, n), lambda i: (i, 0)),
                   pl.BlockSpec((tm, n), lambda i: (i, 0))],
        out_shape=[jax.ShapeDtypeStruct((m, n), F32), jax.ShapeDtypeStruct((m, n), BF16)],
        compiler_params=_params(("parallel",)),
        name="outproj_ln1",
    )(a, w, x, g, b)


def _ffn_act_kernel(hb_ref, wg_ref, wu_ref, o_ref):
    hb = hb_ref[...]
    gate = jnp.dot(hb, wg_ref[...], preferred_element_type=F32)
    up = jnp.dot(hb, wu_ref[...], preferred_element_type=F32)
    o_ref[...] = (jax.nn.silu(gate) * up).astype(o_ref.dtype)


def _ffn_act(hb, wg, wu, tm, tf):
    m, d = hb.shape
    f = wg.shape[1]
    return pl.pallas_call(
        _ffn_act_kernel,
        grid=(m // tm, f // tf),
        in_specs=[pl.BlockSpec((tm, d), lambda i, j: (i, 0)),
                  pl.BlockSpec((d, tf), lambda i, j: (0, j)),
                  pl.BlockSpec((d, tf), lambda i, j: (0, j))],
        out_specs=pl.BlockSpec((tm, tf), lambda i, j: (i, j)),
        out_shape=jax.ShapeDtypeStruct((m, f), BF16),
        compiler_params=_params(("parallel", "arbitrary")),
        name="swiglu_act",
    )(hb, wg, wu)


def _ffn_down_ln_kernel(a_ref, w_ref, h_ref, g_ref, b_ref, o_ref, *, alpha):
    _matmul_residual_ln(a_ref, w_ref, h_ref, g_ref, b_ref, o_ref, alpha)


def _ffn_down_ln(act, wd, h, g, b, alpha, tm):
    m, f = act.shape
    d = wd.shape[1]
    return pl.pallas_call(
        functools.partial(_ffn_down_ln_kernel, alpha=alpha),
        grid=(m // tm,),
        in_specs=[pl.BlockSpec((tm, f), lambda i: (i, 0)),
                  pl.BlockSpec((f, d), lambda i: (0, 0), pipeline_mode=pl.Buffered(1)),
                  pl.BlockSpec((tm, d), lambda i: (i, 0)),
                  pl.BlockSpec((1, d), lambda i: (0, 0)),
                  pl.BlockSpec((1, d), lambda i: (0, 0))],
        out_specs=pl.BlockSpec((tm, d), lambda i: (i, 0)),
        out_shape=jax.ShapeDtypeStruct((m, d), F32),
        compiler_params=_params(("parallel",)),
        name="swiglu_down_ln2",
    )(act, wd, h, g, b)


def _layer(x2, w_in, b_gate, na_rpb, w_na_out, s5_a_re, s5_a_im, s5_log_dt, s5_b_re, s5_b_im,
           s5_c_re, s5_c_im, s5_d, w_glu, b_glu, w_s5_out, w_out, ln1_g, ln1_b,
           w_ffn_gate, w_ffn_up, w_ffn_down, ln2_g, ln2_b, alpha):
    seq, d_model = x2.shape
    rows = seq // GRID_W
    nc = seq // S5_CHUNK
    gate_col0 = 3 * NA_WIDTH + S5_WIDTH
    in_cols = gate_col0 + 2 * d_model
    row = lambda v: v.astype(F32).reshape(1, -1)

    col_scale = jnp.where(jnp.arange(in_cols) < NA_WIDTH, NA_HEAD_DIM ** -0.5, 1.0).astype(F32).reshape(1, -1)
    col_bias = jnp.concatenate([jnp.zeros((gate_col0,), F32), b_gate.astype(F32)]).reshape(1, -1)
    qkv, u, gates = _inproj(x2, w_in.astype(BF16), col_scale, col_bias, 3 * NA_WIDTH, gate_col0, tm=1024, tn=1024)

    later_weights = [w.astype(F32) for w in (w_na_out, w_glu, w_s5_out, w_out, w_ffn_gate, w_ffn_up, w_ffn_down)]
    att, (w_na_b, w_glu_b, w_s5_b, w_out_b, w_gate_b, w_up_b, w_down_b) = _attention(
        qkv, _attention_bias(na_rpb, rows), seq, later_weights)

    w_state, lags, m_state, a_re, a_im = _s5_tables(s5_a_re, s5_a_im, s5_log_dt, s5_b_re, s5_b_im,
                                                    s5_c_re, s5_c_im, s5_d)
    ug, h_states = _s5_states(u, w_state, a_re, a_im)
    ys = _s5_outputs(ug, h_states, lags, m_state, nsplit=2)

    merged = _glu_merge(att, w_na_b, ys, w_glu_b, row(b_glu), w_s5_b, gates, tm=1024, tn=512)
    h, hb = _outproj_ln(merged, w_out_b, x2, row(ln1_g), row(ln1_b), alpha, tm=512)
    act = _ffn_act(hb, w_gate_b, w_up_b, tm=1024, tf=512)
    return _ffn_down_ln(act, w_down_b, h, row(ln2_g), row(ln2_b), alpha, tm=512)


def kernel(x, w_in, b_gate, na_rpb, w_na_out, s5_a_re, s5_a_im, s5_log_dt, s5_b_re, s5_b_im, s5_c_re, s5_c_im, s5_d, w_glu, b_glu, w_s5_out, w_out, ln1_g, ln1_b, w_ffn_gate, w_ffn_up, w_ffn_down, ln2_g, ln2_b):
    bsz, seq, d_model = x.shape
    depth = w_in.shape[0]
    alpha = (2.0 * depth) ** 0.25
    outs = []
    for bi in range(bsz):
        xb = x[bi]
        for l in range(depth):
            xb = _layer(xb, w_in[l], b_gate[l], na_rpb[l], w_na_out[l], s5_a_re[l], s5_a_im[l], s5_log_dt[l],
                        s5_b_re[l], s5_b_im[l], s5_c_re[l], s5_c_im[l], s5_d[l], w_glu[l], b_glu[l],
                        w_s5_out[l], w_out[l], ln1_g[l], ln1_b[l], w_ffn_gate[l], w_ffn_up[l],
                        w_ffn_down[l], ln2_g[l], ln2_b[l], alpha)
        outs.append(xb)
    return jnp.stack(outs)
```

```python
import functools
import math

import jax
import jax.numpy as jnp
from jax import lax
from jax.experimental import pallas as pl
from jax.experimental.pallas import tpu as pltpu

F32 = jnp.float32
BF16 = jnp.bfloat16

GRID_W = 64
NA_HEADS = 8
NA_HEAD_DIM = 128
NA_WIDTH = NA_HEADS * NA_HEAD_DIM
NA_ROWS = 8
NA_COLS = 16
S5_GROUP = 16
S5_GROUPS = 64
S5_WIDTH = S5_GROUP * S5_GROUPS
S5_STATE = 64
LN_EPS = 1e-5
MASK_VALUE = -1e30

S5_CHUNK = 16
S5_CHUNK_WIDTH = S5_CHUNK * S5_GROUP
S5_STATE_WIDTH = 4 * S5_STATE
SCAN_GROUPS = 8

ATT_Q_ROWS = 8
ATT_K_ROWS = 16
ATT_Q = ATT_Q_ROWS * GRID_W
ATT_K = ATT_K_ROWS * GRID_W
ATT_KBLK = 256
LANES = 128
ATT_WIN = NA_ROWS * GRID_W + LANES
ATT_HEADS_PER_STEP = 4
ATT_ROW_SPLIT = 2
BF16_SUBLANES = 16
LN_ROW_SPLIT = 4
S5_LANE_GROUPS = LANES // S5_GROUP
RELAYOUT_ROWS = 16

VMEM_LIMIT = 56 * 1024 * 1024


def _params(sem):
    return pltpu.CompilerParams(dimension_semantics=sem, vmem_limit_bytes=VMEM_LIMIT)


def _layer_norm_rows(y, g, b):
    mu = jnp.mean(y, axis=-1, keepdims=True)
    d = y - mu
    var = jnp.mean(d * d, axis=-1, keepdims=True)
    return d * lax.rsqrt(var + LN_EPS) * g + b


def _inproj_kernel(x_ref, w_ref, s_ref, b_ref, qkv_ref, u_ref, g_ref, xb_ref, *, n_blocks, u_tile0, gate_tile0):
    r = pl.program_id(0)
    j = pl.program_id(1)
    slab = x_ref.shape[0]

    @pl.when(r < n_blocks)
    def _():
        xb_ref[r % 2, pl.ds(pl.multiple_of(j * slab, slab), slab), :] = x_ref[...].astype(BF16)

    cur = 1 - r % 2
    busy = r > 0

    @pl.when(busy & (j < u_tile0))
    def _():
        acc = (jnp.dot(xb_ref[cur], w_ref[...], preferred_element_type=F32) * s_ref[...]).astype(qkv_ref.dtype)
        for h in range(qkv_ref.shape[0]):
            qkv_ref[h] = acc[:, h * NA_HEAD_DIM:(h + 1) * NA_HEAD_DIM]

    @pl.when(busy & (j >= u_tile0) & (j < gate_tile0))
    def _():
        u_ref[...] = jnp.dot(xb_ref[cur], w_ref[...], preferred_element_type=F32)

    @pl.when(busy & (j >= gate_tile0))
    def _():
        acc = jnp.dot(xb_ref[cur], w_ref[...], preferred_element_type=F32)
        g_ref[...] = (acc + b_ref[...]).astype(g_ref.dtype)


def _inproj(x, w, col_scale, col_bias, u_col0, gate_col0, tm, tn):
    m, k = x.shape
    n = w.shape[1]
    n_blocks, n_tiles = m // tm, n // tn
    u_tile0, gate_tile0 = u_col0 // tn, gate_col0 // tn
    heads_per_tile = tn // NA_HEAD_DIM
    last_u = gate_tile0 - u_tile0 - 1
    last_g = n_tiles - gate_tile0 - 1
    slab = tm // n_tiles
    blk = lambda r: jnp.maximum(r - 1, 0)
    live = lambda r, t: jnp.where(r == 0, 0, t)
    return pl.pallas_call(
        functools.partial(_inproj_kernel, n_blocks=n_blocks, u_tile0=u_tile0, gate_tile0=gate_tile0),
        grid=(n_blocks + 1, n_tiles),
        in_specs=[
            pl.BlockSpec((slab, k), lambda r, j: (jnp.minimum(r * n_tiles + j, n_blocks * n_tiles - 1), 0)),
            pl.BlockSpec((k, tn), lambda r, j: (0, live(r, j))),
            pl.BlockSpec((1, tn), lambda r, j: (0, j)),
            pl.BlockSpec((1, tn), lambda r, j: (0, j)),
        ],
        out_specs=[pl.BlockSpec((heads_per_tile, tm, NA_HEAD_DIM),
                                lambda r, j: (live(r, jnp.minimum(j, u_tile0 - 1)), blk(r), 0)),
                   pl.BlockSpec((tm, tn), lambda r, j: (blk(r), live(r, jnp.clip(j - u_tile0, 0, last_u)))),
                   pl.BlockSpec((tm, tn), lambda r, j: (blk(r), live(r, jnp.clip(j - gate_tile0, 0, last_g))))],
        out_shape=[jax.ShapeDtypeStruct((u_col0 // NA_HEAD_DIM, m, NA_HEAD_DIM), BF16),
                   jax.ShapeDtypeStruct((m, gate_col0 - u_col0), F32),
                   jax.ShapeDtypeStruct((m, n - gate_col0), BF16)],
        scratch_shapes=[pltpu.VMEM((2, tm, k), BF16)],
        compiler_params=_params(("arbitrary", "arbitrary")),
        name="inproj",
    )(x, w, col_scale, col_bias)


def _attn_windows(rows):
    nb = rows // ATT_Q_ROWS
    kinds = []
    for b in (0, 1, nb - 1):
        base = min(max(b * ATT_Q_ROWS - NA_ROWS // 2, 0), rows - ATT_K_ROWS)
        geo = []
        for ql in range(ATT_Q_ROWS):
            qr = b * ATT_Q_ROWS + ql
            r0 = min(max(qr - NA_ROWS // 2, 0), rows - NA_ROWS)
            koff = r0 - base
            lane0 = min(LANES * (koff * GRID_W // LANES), ATT_K - ATT_WIN)
            geo.append((lane0, koff * GRID_W - lane0, r0 - qr + NA_ROWS - 1))
        kinds.append(tuple(geo))
    return tuple(kinds)


def _attn_kernel(q_ref, k0, k1, k2, k3, v0, v1, v2, v3, bias_ref, *rest, windows, nb, n_cast):
    cast_src, o_ref, cast_dst, p_ref = rest[:n_cast], rest[n_cast], rest[n_cast + 1:2 * n_cast + 1], rest[-1]
    b = pl.program_id(1)
    k_refs, v_refs = (k0, k1, k2, k3), (v0, v1, v2, v3)


    def one_head(hh, geo):
        c = slice(hh * NA_HEAD_DIM, (hh + 1) * NA_HEAD_DIM)
        k = jnp.concatenate([r[hh] for r in k_refs], axis=0)
        v = jnp.concatenate([r[hh] for r in v_refs], axis=0)
        per = ATT_Q_ROWS // ATT_ROW_SPLIT
        for part in range(ATT_ROW_SPLIT):
            sub = geo[part * per:(part + 1) * per]
            lo = min(lane0 for lane0, _, _ in sub)
            hi = max(lane0 for lane0, _, _ in sub) + ATT_WIN
            rows = slice(part * per * GRID_W, (part + 1) * per * GRID_W)
            s = lax.dot_general(q_ref[hh, rows, :], k[lo:hi], (((1,), (1,)), ((), ())),
                                preferred_element_type=F32)
            inv = []
            for i, (lane0, _, _) in enumerate(sub):
                ql = part * per + i
                r = slice(ql * GRID_W, (ql + 1) * GRID_W)
                sw = s[i * GRID_W:(i + 1) * GRID_W, lane0 - lo:lane0 - lo + ATT_WIN] + bias_ref[hh, ql]
                m = jnp.max(sw, axis=-1, keepdims=True)
                p = jnp.exp(sw - m)
                inv.append(1.0 / jnp.sum(p, axis=-1, keepdims=True))
                if lane0 > lo:
                    p_ref[hh, r, lo:lane0] = jnp.zeros((GRID_W, lane0 - lo), BF16)
                p_ref[hh, r, lane0:lane0 + ATT_WIN] = p.astype(BF16)
                if lane0 + ATT_WIN < hi:
                    p_ref[hh, r, lane0 + ATT_WIN:hi] = jnp.zeros((GRID_W, hi - lane0 - ATT_WIN), BF16)
            o = jnp.dot(p_ref[hh, rows, lo:hi], v[lo:hi], preferred_element_type=F32)
            o_ref[rows, c] = (o * jnp.concatenate(inv, axis=0)).astype(o_ref.dtype)

    def block_kind(geo):
        for src, dst in zip(cast_src, cast_dst):
            dst[...] = src[...].astype(dst.dtype)
        for hh in range(ATT_HEADS_PER_STEP):
            one_head(hh, geo)

    @pl.when(b == 0)
    def _():
        block_kind(windows[0])

    @pl.when((b > 0) & (b < nb - 1))
    def _():
        block_kind(windows[1])

    @pl.when(b == nb - 1)
    def _():
        block_kind(windows[2])


def _attention(qkv, bias, seq, weights):
    nb = seq // ATT_Q
    n_kblk = seq // ATT_KBLK
    per_blk = ATT_K // ATT_KBLK

    def kstart(b):
        return jnp.clip(2 * b - 1, 0, n_kblk - per_blk)

    def kind(b):
        return jnp.where(b == 0, 0, jnp.where(b == nb - 1, 2, 1))

    hps = ATT_HEADS_PER_STEP
    width = hps * NA_HEAD_DIM
    n_groups = NA_HEADS // hps

    def kv_spec(group0, t):
        return pl.BlockSpec((hps, ATT_KBLK, NA_HEAD_DIM), lambda h, b: (group0 + h, kstart(b) + t, 0))

    in_specs = [pl.BlockSpec((hps, ATT_Q, NA_HEAD_DIM), lambda h, b: (h, b, 0))]
    in_specs += [kv_spec(n_groups, t) for t in range(per_blk)]
    in_specs += [kv_spec(2 * n_groups, t) for t in range(per_blk)]
    in_specs += [pl.BlockSpec((None, hps, ATT_Q_ROWS, GRID_W, ATT_WIN), lambda h, b: (kind(b), h, 0, 0, 0))]

    n_steps = n_groups * nb

    def slab_spec(w):
        rows_w, cols_w = w.shape
        for rb in range(n_steps, 0, -1):
            cb = n_steps // rb
            if (rb * cb == n_steps and rows_w % rb == 0 and cols_w % cb == 0
                    and (rows_w // rb) % BF16_SUBLANES == 0 and (cols_w // cb) % LANES == 0):
                return pl.BlockSpec((rows_w // rb, cols_w // cb),
                                    lambda h, b, cb=cb: ((h * nb + b) // cb, (h * nb + b) % cb))
        raise ValueError(f"no slab tiling for {w.shape} over {n_steps} steps")

    slab_specs = [slab_spec(w) for w in weights]
    outs = pl.pallas_call(
        functools.partial(_attn_kernel, windows=_attn_windows(seq // GRID_W), nb=nb, n_cast=len(weights)),
        grid=(n_groups, nb),
        in_specs=in_specs + slab_specs,
        out_specs=[pl.BlockSpec((ATT_Q, width), lambda h, b: (b, h))] + slab_specs,
        out_shape=[jax.ShapeDtypeStruct((seq, NA_WIDTH), BF16)]
                  + [jax.ShapeDtypeStruct(w.shape, BF16) for w in weights],
        scratch_shapes=[pltpu.VMEM((hps, ATT_Q, ATT_K), BF16)],
        compiler_params=_params(("parallel", "arbitrary")),
        name="na_attention",
    )(*([qkv] * (1 + 2 * per_blk)), bias, *weights)
    return outs[0], outs[1:]


def _attention_bias(rpb, rows):
    cols = jnp.arange(GRID_W)
    col_start = jnp.clip(cols - NA_COLS // 2, 0, GRID_W - NA_COLS)
    col_ok = (cols[None, :] >= col_start[:, None]) & (cols[None, :] < col_start[:, None] + NA_COLS)
    col_idx = jnp.clip(cols[None, :] - cols[:, None], -(NA_COLS - 1), NA_COLS - 1) + (NA_COLS - 1)
    pick = (jnp.arange(2 * NA_COLS - 1)[:, None, None] == col_idx[None]).astype(F32)
    by_col = jnp.einsum("hrj,jqk->hqrk", rpb.astype(F32), pick, precision=lax.Precision.HIGHEST)
    by_col = jnp.where(col_ok[None, :, None, :], by_col, MASK_VALUE)
    kinds, made = [], {}
    for geo in _attn_windows(rows):
        strips = []
        for _, phase, rfirst in geo:
            if (phase, rfirst) not in made:
                strip = by_col[:, :, rfirst:rfirst + NA_ROWS, :].reshape(NA_HEADS, GRID_W, NA_ROWS * GRID_W)
                made[phase, rfirst] = jnp.pad(strip, ((0, 0), (0, 0), (phase, ATT_WIN - NA_ROWS * GRID_W - phase)),
                                              constant_values=MASK_VALUE)
            strips.append(made[phase, rfirst])
        kinds.append(jnp.stack(strips, axis=1))
    return jnp.stack(kinds)


def _s5_tables(a_re, a_im, log_dt, b_re, b_im, c_re, c_im, d):
    t_len = S5_CHUNK
    groups = a_re.shape[1]
    steps = jnp.arange(t_len, dtype=F32)
    flip = t_len - 1 - steps

    def cmul(xr, xi, yr, yi):
        return xr * yr - xi * yi, xr * yi + xi * yr

    def powers(zr, zi, n):
        mag = jnp.exp(zr * n)
        return mag * jnp.cos(zi * n), mag * jnp.sin(zi * n)

    ar, ai = a_re.astype(F32), a_im.astype(F32)
    dt = jnp.exp(log_dt.astype(F32))[..., None]
    zr, zi = ar * dt, ai * dt
    lr, li = powers(zr, zi, 1.0)
    lr = lr - 1.0
    den = ar * ar + ai * ai
    fr, fi = (lr * ar + li * ai) / den, (li * ar - lr * ai) / den
    swap = lambda v: jnp.swapaxes(v.astype(F32), 2, 3)
    bbr, bbi = cmul(fr[:, :, None, :], fi[:, :, None, :], swap(b_re), swap(b_im))
    crt, cit = swap(c_re), swap(c_im)
    n_w = jnp.stack([flip, steps])[:, None, :, None]
    wp_r, wp_i = powers(zr[:, :, None, :], zi[:, :, None, :], n_w)
    wr, wi = cmul(wp_r[:, :, :, None, :], wp_i[:, :, :, None, :], bbr[:, :, None], bbi[:, :, None])
    wr = wr.reshape(2, groups, S5_CHUNK_WIDTH, S5_STATE)
    wi = wi.reshape(2, groups, S5_CHUNK_WIDTH, S5_STATE)
    n_e = jnp.stack([steps + 1.0, flip + 1.0])[:, None, None, :]
    ep_r, ep_i = powers(zr[..., None], zi[..., None], n_e)
    er, ei = cmul(crt[:, :, :, None, :], cit[:, :, :, None, :], ep_r[..., None], ep_i[..., None])
    er = er.reshape(2, groups, S5_STATE, S5_CHUNK_WIDTH)
    ei = ei.reshape(2, groups, S5_STATE, S5_CHUNK_WIDTH)
    lp_r, lp_i = powers(zr[..., None], zi[..., None], steps)
    xr, xi = cmul(crt[:, :, :, None, :], cit[:, :, :, None, :], lp_r[..., None], lp_i[..., None])
    lhs = jnp.concatenate([bbr, -bbi], axis=-1)
    rhs = jnp.concatenate([xr, xi], axis=2).reshape(2, groups, 2 * S5_STATE, S5_CHUNK_WIDTH)
    lag_kernels = jnp.einsum("dgck,dgkn->dgcn", lhs, rhs, precision=lax.Precision.HIGHEST)
    step_r, step_i = powers(zr, zi, float(t_len))

    w_state = jnp.concatenate([wr[0], wr[1], wi[0], wi[1]], axis=-1)
    m_state = jnp.concatenate([er[0], er[1], -ei[0], -ei[1]], axis=1)
    kf, kb = lag_kernels[0], lag_kernels[1]
    kb_rev = kb[:, :, S5_GROUP:].reshape(groups, S5_GROUP, t_len - 1, S5_GROUP)[:, :, ::-1]
    lag0 = kf[:, :, :S5_GROUP] + kb[:, :, :S5_GROUP] + d.astype(F32)[:, None, :] * jnp.eye(S5_GROUP, dtype=F32)[None]
    lags = jnp.concatenate([kb_rev.reshape(groups, S5_GROUP, -1), lag0, kf[:, :, S5_GROUP:]], axis=-1)
    return (w_state.astype(BF16), lags, m_state.astype(BF16),
            jnp.concatenate([step_r[0], step_r[1]], axis=-1), jnp.concatenate([step_i[0], step_i[1]], axis=-1))


def _lane_group_ids():
    return lax.broadcasted_iota(jnp.int32, (RELAYOUT_ROWS, LANES), 1) // S5_GROUP


def _transpose_lane_blocks(pieces, grp):
    nblk = S5_LANE_GROUPS
    rolled = []
    for r in range(nblk):
        acc = None
        for b in range(nblk):
            acc = pieces[b] if acc is None else jnp.where(grp == (b - r) % nblk, pieces[b], acc)
        rolled.append(pltpu.roll(acc, S5_GROUP * r, 1) if r else acc)
    out = []
    for k in range(nblk):
        acc = None
        for b in range(nblk):
            z = rolled[(b - k) % nblk]
            acc = z if acc is None else jnp.where(grp == b, z, acc)
        out.append(acc)
    return out


def _s5_state_kernel(u_ref, w_ref, ar_ref, ai_ref, ug_ref, h_ref, s_ref):
    lg = S5_LANE_GROUPS
    nc = ug_ref.shape[1]
    grp = _lane_group_ids()

    def regroup(rb, carry):
        r0 = pl.multiple_of(rb * RELAYOUT_ROWS, RELAYOUT_ROWS)
        for half in range(S5_CHUNK_WIDTH // LANES):
            steps = [u_ref[pl.ds(r0 * S5_CHUNK + half * S5_LANE_GROUPS + tp, RELAYOUT_ROWS, stride=S5_CHUNK), :]
                     for tp in range(S5_LANE_GROUPS)]
            for g, v in enumerate(_transpose_lane_blocks(steps, grp)):
                ug_ref[g, pl.ds(r0, RELAYOUT_ROWS), half * LANES:(half + 1) * LANES] = v.astype(BF16)
        return carry

    lax.fori_loop(0, nc // RELAYOUT_ROWS, regroup, 0, unroll=2)

    half = S5_STATE_WIDTH // 2
    for g in range(lg):
        sg = jnp.dot(ug_ref[g], w_ref[g], preferred_element_type=F32)
        s_ref[0, pl.ds(g, nc, stride=lg), :] = sg[:, :half]
        s_ref[1, pl.ds(g, nc, stride=lg), :] = sg[:, half:]

    lane = lax.broadcasted_iota(jnp.int32, (lg, half), 1)
    fwd = lane < S5_STATE
    ar = ar_ref[...]
    ai = ai_ref[...]
    rows = lambda i: pl.ds(pl.multiple_of(i * lg, lg), lg)

    def advance(i, hr, hi):
        sr = jnp.where(fwd, s_ref[0, rows(i), :], s_ref[0, rows(nc - 1 - i), :])
        si = jnp.where(fwd, s_ref[1, rows(i), :], s_ref[1, rows(nc - 1 - i), :])
        return ar * hr - ai * hi + sr, ar * hi + ai * hr + si

    def first_touch(i, carry):
        hr, hi = carry
        for c in (i, nc - 1 - i):
            h_ref[0, rows(c), :] = hr
            h_ref[1, rows(c), :] = hi
        return advance(i, hr, hi)

    def second_touch(i, carry):
        hr, hi = carry
        j = nc - 1 - i
        h_ref[0, rows(i), :] = jnp.where(fwd, hr, h_ref[0, rows(i), :])
        h_ref[1, rows(i), :] = jnp.where(fwd, hi, h_ref[1, rows(i), :])
        h_ref[0, rows(j), :] = jnp.where(fwd, h_ref[0, rows(j), :], hr)
        h_ref[1, rows(j), :] = jnp.where(fwd, h_ref[1, rows(j), :], hi)
        return advance(i, hr, hi)

    zero = jnp.zeros((lg, half), F32)
    carry = lax.fori_loop(0, nc // 2, first_touch, (zero, zero))
    lax.fori_loop(nc // 2, nc, second_touch, carry)


def _s5_states(u, w_state, a_re, a_im):
    seq, width = u.shape
    g = width // S5_GROUP
    nc = seq // S5_CHUNK
    lg = S5_LANE_GROUPS
    half = S5_STATE_WIDTH // 2
    return pl.pallas_call(
        _s5_state_kernel,
        grid=(width // LANES,),
        in_specs=[pl.BlockSpec((seq, LANES), lambda j: (0, j)),
                  pl.BlockSpec((lg, S5_CHUNK_WIDTH, S5_STATE_WIDTH), lambda j: (j, 0, 0)),
                  pl.BlockSpec((lg, half), lambda j: (j, 0)),
                  pl.BlockSpec((lg, half), lambda j: (j, 0))],
        out_specs=[pl.BlockSpec((lg, nc, S5_CHUNK_WIDTH), lambda j: (j, 0, 0)),
                   pl.BlockSpec((None, 2, nc * lg, half), lambda j: (j, 0, 0, 0))],
        out_shape=[jax.ShapeDtypeStruct((g, nc, S5_CHUNK_WIDTH), BF16),
                   jax.ShapeDtypeStruct((g // lg, 2, nc * lg, half), F32)],
        scratch_shapes=[pltpu.VMEM((2, nc * lg, half), F32)],
        compiler_params=_params(("parallel",)),
        name="s5_states",
    )(u, w_state, a_re, a_im)


def _s5_out_kernel(ug_ref, h_ref, lag_ref, m_ref, o_ref, y_ref, toe_ref):
    cw = S5_CHUNK_WIDTH
    nch = ug_ref.shape[1]
    for g in range(S5_LANE_GROUPS):
        lag = lag_ref[g]
        for s in range(S5_CHUNK):
            first = S5_GROUP * (S5_CHUNK - 1 - s)
            toe_ref[g, s * S5_GROUP:(s + 1) * S5_GROUP, :] = lag[:, first:first + cw].astype(BF16)
    for g in range(S5_LANE_GROUPS):
        hg = jnp.concatenate([h_ref[part, pl.ds(g, nch, stride=S5_LANE_GROUPS), :] for part in range(2)],
                             axis=1).astype(BF16)
        y = jnp.dot(ug_ref[g], toe_ref[g], preferred_element_type=F32)
        y = y + jnp.dot(hg, m_ref[g], preferred_element_type=F32)
        y_ref[g] = jax.nn.gelu(y, approximate=True)
    grp = _lane_group_ids()

    def regroup(rb, carry):
        r0 = pl.multiple_of(rb * RELAYOUT_ROWS, RELAYOUT_ROWS)
        for half in range(cw // LANES):
            groups = [y_ref[g, pl.ds(r0, RELAYOUT_ROWS), half * LANES:(half + 1) * LANES]
                      for g in range(S5_LANE_GROUPS)]
            for tp, v in enumerate(_transpose_lane_blocks(groups, grp)):
                t = half * S5_LANE_GROUPS + tp
                o_ref[pl.ds(r0 * S5_CHUNK + t, RELAYOUT_ROWS, stride=S5_CHUNK), :] = v
        return carry

    lax.fori_loop(0, nch // RELAYOUT_ROWS, regroup, 0, unroll=2)


def _s5_outputs(ug, h3, lags, m_state, nsplit):
    g, nc, cw = ug.shape
    nch = nc // nsplit
    lg = S5_LANE_GROUPS
    return pl.pallas_call(
        _s5_out_kernel,
        grid=(g // lg, nsplit),
        in_specs=[pl.BlockSpec((lg, nch, cw), lambda j, h: (j, h, 0)),
                  pl.BlockSpec((None, 2, nch * lg, S5_STATE_WIDTH // 2), lambda j, h: (j, 0, h, 0)),
                  pl.BlockSpec((lg,) + lags.shape[1:], lambda j, h: (j, 0, 0)),
                  pl.BlockSpec((lg, S5_STATE_WIDTH, cw), lambda j, h: (j, 0, 0))],
        out_specs=pl.BlockSpec((nch * S5_CHUNK, LANES), lambda j, h: (h, j)),
        out_shape=jax.ShapeDtypeStruct((nc * S5_CHUNK, g * S5_GROUP), F32),
        scratch_shapes=[pltpu.VMEM((lg, nch, cw), F32), pltpu.VMEM((lg, cw, cw), BF16)],
        compiler_params=_params(("parallel", "parallel")),
        name="s5_outputs",
    )(ug, h3, lags, m_state)


def _merge_kernel(a1_ref, w1_ref, ys_ref, wg_ref, bg_ref, w2_ref, g1_ref, g2_ref, o_ref, ysb_ref, z_ref, *, tg):
    @pl.when(pl.program_id(1) == 0)
    def _():
        ysb_ref[...] = ys_ref[...].astype(BF16)
        for c in range(z_ref.shape[1] // tg):
            cols = slice(c * tg, (c + 1) * tg)
            t = jnp.dot(ysb_ref[...], wg_ref[:, cols], preferred_element_type=F32) + bg_ref[:, cols]
            z_ref[:, cols] = (ys_ref[:, cols] * jax.nn.sigmoid(t)).astype(BF16)

    y1 = jnp.dot(a1_ref[...], w1_ref[...], preferred_element_type=F32)
    y2 = jnp.dot(z_ref[...], w2_ref[...], preferred_element_type=F32)
    g1 = jax.nn.sigmoid(g1_ref[...].astype(F32))
    g2 = jax.nn.sigmoid(g2_ref[...].astype(F32))
    o_ref[...] = (g1 * y1 + g2 * y2).astype(o_ref.dtype)


def _glu_merge(att, w_na, ys, w_glu, b_glu, w_s5, gates, tm, tn):
    m, k1 = att.shape
    k2 = ys.shape[1]
    n = w_na.shape[1]
    g1 = 0
    g2 = n // tn
    return pl.pallas_call(
        functools.partial(_merge_kernel, tg=tn),
        grid=(m // tm, n // tn),
        in_specs=[pl.BlockSpec((tm, k1), lambda i, j: (i, 0)),
                  pl.BlockSpec((k1, tn), lambda i, j: (0, j)),
                  pl.BlockSpec((tm, k2), lambda i, j: (i, 0)),
                  pl.BlockSpec((k2, k2), lambda i, j: (0, 0)),
                  pl.BlockSpec((1, k2), lambda i, j: (0, 0)),
                  pl.BlockSpec((k2, tn), lambda i, j: (0, j)),
                  pl.BlockSpec((tm, tn), lambda i, j: (i, g1 + j)),
                  pl.BlockSpec((tm, tn), lambda i, j: (i, g2 + j))],
        out_specs=pl.BlockSpec((tm, tn), lambda i, j: (i, j)),
        out_shape=jax.ShapeDtypeStruct((m, n), BF16),
        scratch_shapes=[pltpu.VMEM((tm, k2), BF16), pltpu.VMEM((tm, k2), BF16)],
        compiler_params=_params(("parallel", "arbitrary")),
        name="glu_gated_merge",
    )(att, w_na, ys, w_glu, b_glu, w_s5, gates, gates)


def _matmul_residual_ln(a_ref, w_ref, res_ref, g_ref, b_ref, o_ref, alpha, ob_ref=None):
    sub = a_ref.shape[0] // LN_ROW_SPLIT
    for r in range(LN_ROW_SPLIT):
        rows = slice(r * sub, (r + 1) * sub)
        y = jnp.dot(a_ref[rows, :], w_ref[...], preferred_element_type=F32)
        out = _layer_norm_rows(alpha * res_ref[rows, :] + y, g_ref[...], b_ref[...])
        o_ref[rows, :] = out
        if ob_ref is not None:
            ob_ref[rows, :] = out.astype(ob_ref.dtype)


def _outproj_ln_kernel(a_ref, w_ref, x_ref, g_ref, b_ref, h_ref, hb_ref, *, alpha):
    _matmul_residual_ln(a_ref, w_ref, x_ref, g_ref, b_ref, h_ref, alpha, hb_ref)


def _outproj_ln(a, w, x, g, b, alpha, tm):
    m, k = a.shape
    n = w.shape[1]
    return pl.pallas_call(
        functools.partial(_outproj_ln_kernel, alpha=alpha),
        grid=(m // tm,),
        in_specs=[pl.BlockSpec((tm, k), lambda i: (i, 0)),
                  pl.BlockSpec((k, n), lambda i: (0, 0)),
                  pl.BlockSpec((tm, n), lambda i: (i, 0)),
                  pl.BlockSpec((1, n), lambda i: (0, 0)),
                  pl.BlockSpec((1, n), lambda i: (0, 0))],
        out_specs=[pl.BlockSpec((tm, n), lambda i: (i, 0)),
                   pl.BlockSpec((tm, n), lambda i: (i, 0))],
        out_shape=[jax.ShapeDtypeStruct((m, n), F32), jax.ShapeDtypeStruct((m, n), BF16)],
        compiler_params=_params(("parallel",)),
        name="outproj_ln1",
    )(a, w, x, g, b)


def _ffn_act_kernel(hb_ref, wg_ref, wu_ref, o_ref):
    hb = hb_ref[...]
    gate = jnp.dot(hb, wg_ref[...], preferred_element_type=F32)
    up = jnp.dot(hb, wu_ref[...], preferred_element_type=F32)
    o_ref[...] = (jax.nn.silu(gate) * up).astype(o_ref.dtype)


def _ffn_act(hb, wg, wu, tm, tf):
    m, d = hb.shape
    f = wg.shape[1]
    return pl.pallas_call(
        _ffn_act_kernel,
        grid=(m // tm, f // tf),
        in_specs=[pl.BlockSpec((tm, d), lambda i, j: (i, 0)),
                  pl.BlockSpec((d, tf), lambda i, j: (0, j)),
                  pl.BlockSpec((d, tf), lambda i, j: (0, j))],
        out_specs=pl.BlockSpec((tm, tf), lambda i, j: (i, j)),
        out_shape=jax.ShapeDtypeStruct((m, f), BF16),
        compiler_params=_params(("parallel", "arbitrary")),
        name="swiglu_act",
    )(hb, wg, wu)


def _ffn_down_ln_kernel(a_ref, w_ref, h_ref, g_ref, b_ref, o_ref, *, alpha):
    _matmul_residual_ln(a_ref, w_ref, h_ref, g_ref, b_ref, o_ref, alpha)


def _ffn_down_ln(act, wd, h, g, b, alpha, tm):
    m, f = act.shape
    d = wd.shape[1]
    return pl.pallas_call(
        functools.partial(_ffn_down_ln_kernel, alpha=alpha),
        grid=(m // tm,),
        in_specs=[pl.BlockSpec((tm, f), lambda i: (i, 0)),
                  pl.BlockSpec((f, d), lambda i: (0, 0), pipeline_mode=pl.Buffered(1)),
                  pl.BlockSpec((tm, d), lambda i: (i, 0)),
                  pl.BlockSpec((1, d), lambda i: (0, 0)),
                  pl.BlockSpec((1, d), lambda i: (0, 0))],
        out_specs=pl.BlockSpec((tm, d), lambda i: (i, 0)),
        out_shape=jax.ShapeDtypeStruct((m, d), F32),
        compiler_params=_params(("parallel",)),
        name="swiglu_down_ln2",
    )(act, wd, h, g, b)


def _layer(x2, w_in, b_gate, na_rpb, w_na_out, s5_a_re, s5_a_im, s5_log_dt, s5_b_re, s5_b_im,
           s5_c_re, s5_c_im, s5_d, w_glu, b_glu, w_s5_out, w_out, ln1_g, ln1_b,
           w_ffn_gate, w_ffn_up, w_ffn_down, ln2_g, ln2_b, alpha):
    seq, d_model = x2.shape
    rows = seq // GRID_W
    nc = seq // S5_CHUNK
    gate_col0 = 3 * NA_WIDTH + S5_WIDTH
    in_cols = gate_col0 + 2 * d_model
    row = lambda v: v.astype(F32).reshape(1, -1)

    col_scale = jnp.where(jnp.arange(in_cols) < NA_WIDTH, NA_HEAD_DIM ** -0.5, 1.0).astype(F32).reshape(1, -1)
    col_bias = jnp.concatenate([jnp.zeros((gate_col0,), F32), b_gate.astype(F32)]).reshape(1, -1)
    qkv, u, gates = _inproj(x2, w_in.astype(BF16), col_scale, col_bias, 3 * NA_WIDTH, gate_col0, tm=1024, tn=1024)

    later_weights = [w.astype(F32) for w in (w_na_out, w_glu, w_s5_out, w_out, w_ffn_gate, w_ffn_up, w_ffn_down)]
    att, (w_na_b, w_glu_b, w_s5_b, w_out_b, w_gate_b, w_up_b, w_down_b) = _attention(
        qkv, _attention_bias(na_rpb, rows), seq, later_weights)

    w_state, lags, m_state, a_re, a_im = _s5_tables(s5_a_re, s5_a_im, s5_log_dt, s5_b_re, s5_b_im,
                                                    s5_c_re, s5_c_im, s5_d)
    ug, h_states = _s5_states(u, w_state, a_re, a_im)
    ys = _s5_outputs(ug, h_states, lags, m_state, nsplit=2)

    merged = _glu_merge(att, w_na_b, ys, w_glu_b, row(b_glu), w_s5_b, gates, tm=1024, tn=512)
    h, hb = _outproj_ln(merged, w_out_b, x2, row(ln1_g), row(ln1_b), alpha, tm=512)
    act = _ffn_act(hb, w_gate_b, w_up_b, tm=1024, tf=512)
    return _ffn_down_ln(act, w_down_b, h, row(ln2_g), row(ln2_b), alpha, tm=512)


def kernel(x, w_in, b_gate, na_rpb, w_na_out, s5_a_re, s5_a_im, s5_log_dt, s5_b_re, s5_b_im, s5_c_re, s5_c_im, s5_d, w_glu, b_glu, w_s5_out, w_out, ln1_g, ln1_b, w_ffn_gate, w_ffn_up, w_ffn_down, ln2_g, ln2_b):
    bsz, seq, d_model = x.shape
    depth = w_in.shape[0]
    alpha = (2.0 * depth) ** 0.25
    outs = []
    for bi in range(bsz):
        xb = x[bi]
        for l in range(depth):
            xb = _layer(xb, w_in[l], b_gate[l], na_rpb[l], w_na_out[l], s5_a_re[l], s5_a_im[l], s5_log_dt[l],
                        s5_b_re[l], s5_b_im[l], s5_c_re[l], s5_c_im[l], s5_d[l], w_glu[l], b_glu[l],
                        w_s5_out[l], w_out[l], ln1_g[l], ln1_b[l], w_ffn_gate[l], w_ffn_up[l],
                        w_ffn_down[l], ln2_g[l], ln2_b[l], alpha)
        outs.append(xb)
    return jnp.stack(outs)
```

```python
import functools

import jax
import jax.numpy as jnp
from jax import lax
from jax.experimental import pallas as pl
from jax.experimental.pallas import tpu as pltpu

F32 = jnp.float32
BF16 = jnp.bfloat16

GRID_W = 64
NA_HEADS = 8
NA_HEAD_DIM = 128
NA_WIDTH = NA_HEADS * NA_HEAD_DIM
NA_ROWS = 8
NA_COLS = 16
S5_GROUP = 16
S5_GROUPS = 64
S5_WIDTH = S5_GROUP * S5_GROUPS
S5_STATE = 64
LN_EPS = 1e-5
MASK_VALUE = -1e30

S5_CHUNK = 16
S5_CHUNK_WIDTH = S5_CHUNK * S5_GROUP
S5_STATE_WIDTH = 4 * S5_STATE

ATT_Q_ROWS = 8
ATT_K_ROWS = 16
ATT_Q = ATT_Q_ROWS * GRID_W
ATT_K = ATT_K_ROWS * GRID_W
ATT_KBLK = 256
LANES = 128
ATT_WIN = NA_ROWS * GRID_W + LANES
ATT_HEADS_PER_STEP = 4
ATT_ROW_SPLIT = 2
BF16_SUBLANES = 16
LN_ROW_SPLIT = 4
S5_LANE_GROUPS = LANES // S5_GROUP
RELAYOUT_ROWS = 16

VMEM_LIMIT = 56 * 1024 * 1024


def _params(sem):
    return pltpu.CompilerParams(dimension_semantics=sem, vmem_limit_bytes=VMEM_LIMIT)


def _layer_norm_rows(y, g, b):
    mu = jnp.mean(y, axis=-1, keepdims=True)
    d = y - mu
    var = jnp.mean(d * d, axis=-1, keepdims=True)
    return d * lax.rsqrt(var + LN_EPS) * g + b


def _inproj_kernel(x_ref, w_ref, s_ref, b_ref, qkv_ref, u_ref, g_ref, xb_ref, *, n_blocks, u_tile0, gate_tile0):
    r = pl.program_id(0)
    j = pl.program_id(1)
    slab = x_ref.shape[0]

    @pl.when(r < n_blocks)
    def _():
        xb_ref[r % 2, pl.ds(pl.multiple_of(j * slab, slab), slab), :] = x_ref[...].astype(BF16)

    cur = 1 - r % 2
    busy = r > 0

    @pl.when(busy & (j < u_tile0))
    def _():
        acc = (jnp.dot(xb_ref[cur], w_ref[...], preferred_element_type=F32) * s_ref[...]).astype(qkv_ref.dtype)
        for h in range(qkv_ref.shape[0]):
            qkv_ref[h] = acc[:, h * NA_HEAD_DIM:(h + 1) * NA_HEAD_DIM]

    @pl.when(busy & (j >= u_tile0) & (j < gate_tile0))
    def _():
        u_ref[...] = jnp.dot(xb_ref[cur], w_ref[...], preferred_element_type=F32)

    @pl.when(busy & (j >= gate_tile0))
    def _():
        acc = jnp.dot(xb_ref[cur], w_ref[...], preferred_element_type=F32)
        g_ref[...] = (acc + b_ref[...]).astype(g_ref.dtype)


def _inproj(x, w, col_scale, col_bias, u_col0, gate_col0, tm, tn):
    m, k = x.shape
    n = w.shape[1]
    n_blocks, n_tiles = m // tm, n // tn
    u_tile0, gate_tile0 = u_col0 // tn, gate_col0 // tn
    heads_per_tile = tn // NA_HEAD_DIM
    last_u = gate_tile0 - u_tile0 - 1
    last_g = n_tiles - gate_tile0 - 1
    slab = tm // n_tiles
    blk = lambda r: jnp.maximum(r - 1, 0)
    live = lambda r, t: jnp.where(r == 0, 0, t)
    return pl.pallas_call(
        functools.partial(_inproj_kernel, n_blocks=n_blocks, u_tile0=u_tile0, gate_tile0=gate_tile0),
        grid=(n_blocks + 1, n_tiles),
        in_specs=[
            pl.BlockSpec((slab, k), lambda r, j: (jnp.minimum(r * n_tiles + j, n_blocks * n_tiles - 1), 0)),
            pl.BlockSpec((k, tn), lambda r, j: (0, live(r, j))),
            pl.BlockSpec((1, tn), lambda r, j: (0, j)),
            pl.BlockSpec((1, tn), lambda r, j: (0, j)),
        ],
        out_specs=[pl.BlockSpec((heads_per_tile, tm, NA_HEAD_DIM),
                                lambda r, j: (live(r, jnp.minimum(j, u_tile0 - 1)), blk(r), 0)),
                   pl.BlockSpec((tm, tn), lambda r, j: (blk(r), live(r, jnp.clip(j - u_tile0, 0, last_u)))),
                   pl.BlockSpec((tm, tn), lambda r, j: (blk(r), live(r, jnp.clip(j - gate_tile0, 0, last_g))))],
        out_shape=[jax.ShapeDtypeStruct((u_col0 // NA_HEAD_DIM, m, NA_HEAD_DIM), BF16),
                   jax.ShapeDtypeStruct((m, gate_col0 - u_col0), F32),
                   jax.ShapeDtypeStruct((m, n - gate_col0), BF16)],
        scratch_shapes=[pltpu.VMEM((2, tm, k), BF16)],
        compiler_params=_params(("arbitrary", "arbitrary")),
        name="inproj",
    )(x, w, col_scale, col_bias)


def _attn_windows(rows):
    nb = rows // ATT_Q_ROWS
    kinds = []
    for b in (0, 1, nb - 1):
        base = min(max(b * ATT_Q_ROWS - NA_ROWS // 2, 0), rows - ATT_K_ROWS)
        geo = []
        for ql in range(ATT_Q_ROWS):
            qr = b * ATT_Q_ROWS + ql
            r0 = min(max(qr - NA_ROWS // 2, 0), rows - NA_ROWS)
            koff = r0 - base
            lane0 = min(LANES * (koff * GRID_W // LANES), ATT_K - ATT_WIN)
            geo.append((lane0, koff * GRID_W - lane0, r0 - qr + NA_ROWS - 1))
        kinds.append(tuple(geo))
    return tuple(kinds)


def _attn_kernel(q_ref, k0, k1, k2, k3, v0, v1, v2, v3, bias_ref, *rest, windows, nb, n_cast):
    cast_src, o_ref, cast_dst, p_ref = rest[:n_cast], rest[n_cast], rest[n_cast + 1:2 * n_cast + 1], rest[-1]
    b = pl.program_id(1)
    k_refs, v_refs = (k0, k1, k2, k3), (v0, v1, v2, v3)


    def one_head(hh, geo):
        c = slice(hh * NA_HEAD_DIM, (hh + 1) * NA_HEAD_DIM)
        k = jnp.concatenate([r[hh] for r in k_refs], axis=0)
        v = jnp.concatenate([r[hh] for r in v_refs], axis=0)
        per = ATT_Q_ROWS // ATT_ROW_SPLIT
        for part in range(ATT_ROW_SPLIT):
            sub = geo[part * per:(part + 1) * per]
            lo = min(lane0 for lane0, _, _ in sub)
            hi = max(lane0 for lane0, _, _ in sub) + ATT_WIN
            rows = slice(part * per * GRID_W, (part + 1) * per * GRID_W)
            s = lax.dot_general(q_ref[hh, rows, :], k[lo:hi], (((1,), (1,)), ((), ())),
                                preferred_element_type=F32)
            inv = []
            for i, (lane0, _, _) in enumerate(sub):
                ql = part * per + i
                r = slice(ql * GRID_W, (ql + 1) * GRID_W)
                sw = s[i * GRID_W:(i + 1) * GRID_W, lane0 - lo:lane0 - lo + ATT_WIN] + bias_ref[hh, ql]
                m = jnp.max(sw, axis=-1, keepdims=True)
                p = jnp.exp(sw - m)
                inv.append(1.0 / jnp.sum(p, axis=-1, keepdims=True))
                if lane0 > lo:
                    p_ref[hh, r, lo:lane0] = jnp.zeros((GRID_W, lane0 - lo), BF16)
                p_ref[hh, r, lane0:lane0 + ATT_WIN] = p.astype(BF16)
                if lane0 + ATT_WIN < hi:
                    p_ref[hh, r, lane0 + ATT_WIN:hi] = jnp.zeros((GRID_W, hi - lane0 - ATT_WIN), BF16)
            o = jnp.dot(p_ref[hh, rows, lo:hi], v[lo:hi], preferred_element_type=F32)
            o_ref[rows, c] = (o * jnp.concatenate(inv, axis=0)).astype(o_ref.dtype)

    def block_kind(geo):
        for src, dst in zip(cast_src, cast_dst):
            dst[...] = src[...].astype(dst.dtype)
        for hh in range(ATT_HEADS_PER_STEP):
            one_head(hh, geo)

    @pl.when(b == 0)
    def _():
        block_kind(windows[0])

    @pl.when((b > 0) & (b < nb - 1))
    def _():
        block_kind(windows[1])

    @pl.when(b == nb - 1)
    def _():
        block_kind(windows[2])


def _attention(qkv, bias, seq, weights):
    nb = seq // ATT_Q
    n_kblk = seq // ATT_KBLK
    per_blk = ATT_K // ATT_KBLK

    def kstart(b):
        return jnp.clip(2 * b - 1, 0, n_kblk - per_blk)

    def kind(b):
        return jnp.where(b == 0, 0, jnp.where(b == nb - 1, 2, 1))

    hps = ATT_HEADS_PER_STEP
    width = hps * NA_HEAD_DIM
    n_groups = NA_HEADS // hps

    def kv_spec(group0, t):
        return pl.BlockSpec((hps, ATT_KBLK, NA_HEAD_DIM), lambda h, b: (group0 + h, kstart(b) + t, 0))

    in_specs = [pl.BlockSpec((hps, ATT_Q, NA_HEAD_DIM), lambda h, b: (h, b, 0))]
    in_specs += [kv_spec(n_groups, t) for t in range(per_blk)]
    in_specs += [kv_spec(2 * n_groups, t) for t in range(per_blk)]
    in_specs += [pl.BlockSpec((None, hps, ATT_Q_ROWS, GRID_W, ATT_WIN), lambda h, b: (kind(b), h, 0, 0, 0))]

    n_steps = n_groups * nb

    def slab_spec(w):
        rows_w, cols_w = w.shape
        for rb in range(n_steps, 0, -1):
            cb = n_steps // rb
            if (rb * cb == n_steps and rows_w % rb == 0 and cols_w % cb == 0
                    and (rows_w // rb) % BF16_SUBLANES == 0 and (cols_w // cb) % LANES == 0):
                return pl.BlockSpec((rows_w // rb, cols_w // cb),
                                    lambda h, b, cb=cb: ((h * nb + b) // cb, (h * nb + b) % cb))
        raise ValueError(f"no slab tiling for {w.shape} over {n_steps} steps")

    slab_specs = [slab_spec(w) for w in weights]
    outs = pl.pallas_call(
        functools.partial(_attn_kernel, windows=_attn_windows(seq // GRID_W), nb=nb, n_cast=len(weights)),
        grid=(n_groups, nb),
        in_specs=in_specs + slab_specs,
        out_specs=[pl.BlockSpec((ATT_Q, width), lambda h, b: (b, h))] + slab_specs,
        out_shape=[jax.ShapeDtypeStruct((seq, NA_WIDTH), BF16)]
                  + [jax.ShapeDtypeStruct(w.shape, BF16) for w in weights],
        scratch_shapes=[pltpu.VMEM((hps, ATT_Q, ATT_K), BF16)],
        compiler_params=_params(("parallel", "arbitrary")),
        name="na_attention",
    )(*([qkv] * (1 + 2 * per_blk)), bias, *weights)
    return outs[0], outs[1:]


def _attention_bias(rpb, rows):
    cols = jnp.arange(GRID_W)
    col_start = jnp.clip(cols - NA_COLS // 2, 0, GRID_W - NA_COLS)
    col_ok = (cols[None, :] >= col_start[:, None]) & (cols[None, :] < col_start[:, None] + NA_COLS)
    col_idx = jnp.clip(cols[None, :] - cols[:, None], -(NA_COLS - 1), NA_COLS - 1) + (NA_COLS - 1)
    pick = (jnp.arange(2 * NA_COLS - 1)[:, None, None] == col_idx[None]).astype(F32)
    by_col = jnp.einsum("hrj,jqk->hqrk", rpb.astype(F32), pick, precision=lax.Precision.HIGHEST)
    by_col = jnp.where(col_ok[None, :, None, :], by_col, MASK_VALUE)
    by_col = by_col.reshape(NA_HEADS, GRID_W, -1)
    kinds, made = [], {}
    span = NA_ROWS * GRID_W
    for geo in _attn_windows(rows):
        strips = []
        for _, phase, rfirst in geo:
            if (phase, rfirst) not in made:
                strip = by_col[:, :, rfirst * GRID_W:rfirst * GRID_W + span]
                made[phase, rfirst] = jnp.pad(strip, ((0, 0), (0, 0), (phase, ATT_WIN - span - phase)),
                                              constant_values=MASK_VALUE)
            strips.append(made[phase, rfirst])
        kinds.append(jnp.stack(strips, axis=1))
    return jnp.stack(kinds)


def _s5_tables(a_re, a_im, log_dt, b_re, b_im, c_re, c_im, d):
    t_len = S5_CHUNK
    groups = a_re.shape[1]
    steps = jnp.arange(t_len, dtype=F32)
    flip = t_len - 1 - steps

    def cmul(xr, xi, yr, yi):
        return xr * yr - xi * yi, xr * yi + xi * yr

    def powers(zr, zi, n):
        mag = jnp.exp(zr * n)
        return mag * jnp.cos(zi * n), mag * jnp.sin(zi * n)

    lead = lambda v: jnp.swapaxes(v.astype(F32), 0, 1)
    ar, ai = lead(a_re), lead(a_im)
    dt = jnp.exp(lead(log_dt))[..., None]
    zr, zi = ar * dt, ai * dt
    lr, li = powers(zr, zi, 1.0)
    lr = lr - 1.0
    den = ar * ar + ai * ai
    fr, fi = (lr * ar + li * ai) / den, (li * ar - lr * ai) / den
    b_t = lambda v: jnp.transpose(v.astype(F32), (1, 3, 0, 2))
    c_t = lambda v: jnp.transpose(v.astype(F32), (1, 0, 3, 2))
    bbr, bbi = cmul(fr[:, None], fi[:, None], b_t(b_re), b_t(b_im))
    crt, cit = c_t(c_re), c_t(c_im)
    n_w = jnp.stack([flip, steps], axis=-1)[None, :, None, :, None]
    wp_r, wp_i = powers(zr[:, None, None], zi[:, None, None], n_w)
    wr, wi = cmul(wp_r, wp_i, bbr[:, None], bbi[:, None])
    w_state = jnp.concatenate([wr, wi], axis=3).reshape(groups, S5_CHUNK_WIDTH, S5_STATE_WIDTH)
    n_e = jnp.stack([steps + 1.0, flip + 1.0])[None, :, None, :]
    ep_r, ep_i = powers(zr[..., None], zi[..., None], n_e)
    er, ei = cmul(crt[:, :, :, None, :], cit[:, :, :, None, :], ep_r[..., None], ep_i[..., None])
    m_state = jnp.concatenate([er, -ei], axis=1).reshape(groups, S5_STATE_WIDTH, S5_CHUNK_WIDTH)
    lp_r, lp_i = powers(zr[..., None], zi[..., None], steps)
    xr, xi = cmul(crt[:, :, :, None, :], cit[:, :, :, None, :], lp_r[..., None], lp_i[..., None])
    lhs = jnp.concatenate([bbr, -bbi], axis=-1)
    rhs = jnp.concatenate([xr, xi], axis=2).reshape(groups, 2, 2 * S5_STATE, S5_CHUNK_WIDTH)
    lag_kernels = jnp.einsum("gcdk,gdkn->gdcn", lhs, rhs, precision=lax.Precision.HIGHEST)
    step_r, step_i = powers(zr, zi, float(t_len))

    kf, kb = lag_kernels[:, 0], lag_kernels[:, 1]
    kb_rev = kb[:, :, S5_GROUP:].reshape(groups, S5_GROUP, t_len - 1, S5_GROUP)[:, :, ::-1]
    lag0 = kf[:, :, :S5_GROUP] + kb[:, :, :S5_GROUP] + d.astype(F32)[:, None, :] * jnp.eye(S5_GROUP, dtype=F32)[None]
    lags = jnp.concatenate([kb_rev.reshape(groups, S5_GROUP, -1), lag0, kf[:, :, S5_GROUP:]], axis=-1)
    return (w_state.astype(BF16), lags, m_state.astype(BF16),
            step_r.reshape(groups, 2 * S5_STATE), step_i.reshape(groups, 2 * S5_STATE))


def _lane_group_ids():
    return lax.broadcasted_iota(jnp.int32, (RELAYOUT_ROWS, LANES), 1) // S5_GROUP


def _transpose_lane_blocks(pieces, grp):
    nblk = S5_LANE_GROUPS
    rolled = []
    for r in range(nblk):
        acc = None
        for b in range(nblk):
            acc = pieces[b] if acc is None else jnp.where(grp == (b - r) % nblk, pieces[b], acc)
        rolled.append(pltpu.roll(acc, S5_GROUP * r, 1) if r else acc)
    out = []
    for k in range(nblk):
        acc = None
        for b in range(nblk):
            z = rolled[(b - k) % nblk]
            acc = z if acc is None else jnp.where(grp == b, z, acc)
        out.append(acc)
    return out


def _s5_state_kernel(u_ref, w_ref, ar_ref, ai_ref, ug_ref, h_ref, s_ref):
    lg = S5_LANE_GROUPS
    nc = ug_ref.shape[1]
    grp = _lane_group_ids()

    def regroup(rb, carry):
        r0 = pl.multiple_of(rb * RELAYOUT_ROWS, RELAYOUT_ROWS)
        for half in range(S5_CHUNK_WIDTH // LANES):
            steps = [u_ref[pl.ds(r0 * S5_CHUNK + half * S5_LANE_GROUPS + tp, RELAYOUT_ROWS, stride=S5_CHUNK), :]
                     for tp in range(S5_LANE_GROUPS)]
            for g, v in enumerate(_transpose_lane_blocks(steps, grp)):
                ug_ref[g, pl.ds(r0, RELAYOUT_ROWS), half * LANES:(half + 1) * LANES] = v.astype(BF16)
        return carry

    lax.fori_loop(0, nc // RELAYOUT_ROWS, regroup, 0, unroll=2)

    half = S5_STATE_WIDTH // 2
    for g in range(lg):
        sg = jnp.dot(ug_ref[g], w_ref[g], preferred_element_type=F32)
        s_ref[0, pl.ds(g, nc, stride=lg), :] = sg[:, :half]
        s_ref[1, pl.ds(g, nc, stride=lg), :] = sg[:, half:]

    lane = lax.broadcasted_iota(jnp.int32, (lg, half), 1)
    fwd = lane < S5_STATE
    ar = ar_ref[...]
    ai = ai_ref[...]
    rows = lambda i: pl.ds(pl.multiple_of(i * lg, lg), lg)

    def advance(i, hr, hi):
        sr = jnp.where(fwd, s_ref[0, rows(i), :], s_ref[0, rows(nc - 1 - i), :])
        si = jnp.where(fwd, s_ref[1, rows(i), :], s_ref[1, rows(nc - 1 - i), :])
        return ar * hr - ai * hi + sr, ar * hi + ai * hr + si

    def first_touch(i, carry):
        hr, hi = carry
        for c in (i, nc - 1 - i):
            h_ref[0, rows(c), :] = hr
            h_ref[1, rows(c), :] = hi
        return advance(i, hr, hi)

    def second_touch(i, carry):
        hr, hi = carry
        j = nc - 1 - i
        h_ref[0, rows(i), :] = jnp.where(fwd, hr, h_ref[0, rows(i), :])
        h_ref[1, rows(i), :] = jnp.where(fwd, hi, h_ref[1, rows(i), :])
        h_ref[0, rows(j), :] = jnp.where(fwd, h_ref[0, rows(j), :], hr)
        h_ref[1, rows(j), :] = jnp.where(fwd, h_ref[1, rows(j), :], hi)
        return advance(i, hr, hi)

    zero = jnp.zeros((lg, half), F32)
    carry = lax.fori_loop(0, nc // 2, first_touch, (zero, zero))
    lax.fori_loop(nc // 2, nc, second_touch, carry)


def _s5_states(u, w_state, a_re, a_im):
    seq, width = u.shape
    g = width // S5_GROUP
    nc = seq // S5_CHUNK
    lg = S5_LANE_GROUPS
    half = S5_STATE_WIDTH // 2
    return pl.pallas_call(
        _s5_state_kernel,
        grid=(width // LANES,),
        in_specs=[pl.BlockSpec((seq, LANES), lambda j: (0, j)),
                  pl.BlockSpec((lg, S5_CHUNK_WIDTH, S5_STATE_WIDTH), lambda j: (j, 0, 0)),
                  pl.BlockSpec((lg, half), lambda j: (j, 0)),
                  pl.BlockSpec((lg, half), lambda j: (j, 0))],
        out_specs=[pl.BlockSpec((lg, nc, S5_CHUNK_WIDTH), lambda j: (j, 0, 0)),
                   pl.BlockSpec((None, 2, nc * lg, half), lambda j: (j, 0, 0, 0))],
        out_shape=[jax.ShapeDtypeStruct((g, nc, S5_CHUNK_WIDTH), BF16),
                   jax.ShapeDtypeStruct((g // lg, 2, nc * lg, half), F32)],
        scratch_shapes=[pltpu.VMEM((2, nc * lg, half), F32)],
        compiler_params=_params(("parallel",)),
        name="s5_states",
    )(u, w_state, a_re, a_im)


def _s5_out_kernel(ug_ref, h_ref, lag_ref, m_ref, o_ref, y_ref, toe_ref):
    cw = S5_CHUNK_WIDTH
    nch = ug_ref.shape[1]
    for g in range(S5_LANE_GROUPS):
        lag = lag_ref[g]
        for s in range(S5_CHUNK):
            first = S5_GROUP * (S5_CHUNK - 1 - s)
            toe_ref[g, s * S5_GROUP:(s + 1) * S5_GROUP, :] = lag[:, first:first + cw].astype(BF16)
    for g in range(S5_LANE_GROUPS):
        hg = jnp.concatenate([h_ref[part, pl.ds(g, nch, stride=S5_LANE_GROUPS), :] for part in range(2)],
                             axis=1).astype(BF16)
        y = jnp.dot(ug_ref[g], toe_ref[g], preferred_element_type=F32)
        y = y + jnp.dot(hg, m_ref[g], preferred_element_type=F32)
        y_ref[g] = jax.nn.gelu(y, approximate=True)
    grp = _lane_group_ids()

    def regroup(rb, carry):
        r0 = pl.multiple_of(rb * RELAYOUT_ROWS, RELAYOUT_ROWS)
        for half in range(cw // LANES):
            groups = [y_ref[g, pl.ds(r0, RELAYOUT_ROWS), half * LANES:(half + 1) * LANES]
                      for g in range(S5_LANE_GROUPS)]
            for tp, v in enumerate(_transpose_lane_blocks(groups, grp)):
                t = half * S5_LANE_GROUPS + tp
                o_ref[pl.ds(r0 * S5_CHUNK + t, RELAYOUT_ROWS, stride=S5_CHUNK), :] = v
        return carry

    lax.fori_loop(0, nch // RELAYOUT_ROWS, regroup, 0, unroll=2)


def _s5_outputs(ug, h3, lags, m_state, nsplit):
    g, nc, cw = ug.shape
    nch = nc // nsplit
    lg = S5_LANE_GROUPS
    return pl.pallas_call(
        _s5_out_kernel,
        grid=(g // lg, nsplit),
        in_specs=[pl.BlockSpec((lg, nch, cw), lambda j, h: (j, h, 0)),
                  pl.BlockSpec((None, 2, nch * lg, S5_STATE_WIDTH // 2), lambda j, h: (j, 0, h, 0)),
                  pl.BlockSpec((lg,) + lags.shape[1:], lambda j, h: (j, 0, 0)),
                  pl.BlockSpec((lg, S5_STATE_WIDTH, cw), lambda j, h: (j, 0, 0))],
        out_specs=pl.BlockSpec((nch * S5_CHUNK, LANES), lambda j, h: (h, j)),
        out_shape=jax.ShapeDtypeStruct((nc * S5_CHUNK, g * S5_GROUP), F32),
        scratch_shapes=[pltpu.VMEM((lg, nch, cw), F32), pltpu.VMEM((lg, cw, cw), BF16)],
        compiler_params=_params(("parallel", "parallel")),
        name="s5_outputs",
    )(ug, h3, lags, m_state)


def _merge_kernel(a1_ref, w1_ref, ys_ref, wg_ref, bg_ref, w2_ref, g1_ref, g2_ref, o_ref, ysb_ref, z_ref, *, tg):
    @pl.when(pl.program_id(1) == 0)
    def _():
        ysb_ref[...] = ys_ref[...].astype(BF16)
        for c in range(z_ref.shape[1] // tg):
            cols = slice(c * tg, (c + 1) * tg)
            t = jnp.dot(ysb_ref[...], wg_ref[:, cols], preferred_element_type=F32) + bg_ref[:, cols]
            z_ref[:, cols] = (ys_ref[:, cols] * jax.nn.sigmoid(t)).astype(BF16)

    y1 = jnp.dot(a1_ref[...], w1_ref[...], preferred_element_type=F32)
    y2 = jnp.dot(z_ref[...], w2_ref[...], preferred_element_type=F32)
    g1 = jax.nn.sigmoid(g1_ref[...].astype(F32))
    g2 = jax.nn.sigmoid(g2_ref[...].astype(F32))
    o_ref[...] = (g1 * y1 + g2 * y2).astype(o_ref.dtype)


def _glu_merge(att, w_na, ys, w_glu, b_glu, w_s5, gates, tm, tn):
    m, k1 = att.shape
    k2 = ys.shape[1]
    n = w_na.shape[1]
    g1 = 0
    g2 = n // tn
    return pl.pallas_call(
        functools.partial(_merge_kernel, tg=tn),
        grid=(m // tm, n // tn),
        in_specs=[pl.BlockSpec((tm, k1), lambda i, j: (i, 0)),
                  pl.BlockSpec((k1, tn), lambda i, j: (0, j)),
                  pl.BlockSpec((tm, k2), lambda i, j: (i, 0)),
                  pl.BlockSpec((k2, k2), lambda i, j: (0, 0)),
                  pl.BlockSpec((1, k2), lambda i, j: (0, 0)),
                  pl.BlockSpec((k2, tn), lambda i, j: (0, j)),
                  pl.BlockSpec((tm, tn), lambda i, j: (i, g1 + j)),
                  pl.BlockSpec((tm, tn), lambda i, j: (i, g2 + j))],
        out_specs=pl.BlockSpec((tm, tn), lambda i, j: (i, j)),
        out_shape=jax.ShapeDtypeStruct((m, n), BF16),
        scratch_shapes=[pltpu.VMEM((tm, k2), BF16), pltpu.VMEM((tm, k2), BF16)],
        compiler_params=_params(("parallel", "arbitrary")),
        name="glu_gated_merge",
    )(att, w_na, ys, w_glu, b_glu, w_s5, gates, gates)


def _matmul_residual_ln(a_ref, w_ref, res_ref, g_ref, b_ref, o_ref, alpha, ob_ref=None):
    sub = a_ref.shape[0] // LN_ROW_SPLIT
    for r in range(LN_ROW_SPLIT):
        rows = slice(r * sub, (r + 1) * sub)
        y = jnp.dot(a_ref[rows, :], w_ref[...], preferred_element_type=F32)
        out = _layer_norm_rows(alpha * res_ref[rows, :] + y, g_ref[...], b_ref[...])
        o_ref[rows, :] = out
        if ob_ref is not None:
            ob_ref[rows, :] = out.astype(ob_ref.dtype)


def _outproj_ln_kernel(a_ref, w_ref, x_ref, g_ref, b_ref, h_ref, hb_ref, *, alpha):
    _matmul_residual_ln(a_ref, w_ref, x_ref, g_ref, b_ref, h_ref, alpha, hb_ref)


def _outproj_ln(a, w, x, g, b, alpha, tm):
    m, k = a.shape
    n = w.shape[1]
    return pl.pallas_call(
        functools.partial(_outproj_ln_kernel, alpha=alpha),
        grid=(m // tm,),
        in_specs=[pl.BlockSpec((tm, k), lambda i: (i, 0)),
                  pl.BlockSpec((k, n), lambda i: (0, 0)),
                  pl.BlockSpec((tm, n), lambda i: (i, 0)),
                  pl.BlockSpec((1, n), lambda i: (0, 0)),
                  pl.BlockSpec((1, n), lambda i: (0, 0))],
        out_specs=[pl.BlockSpec((tm, n), lambda i: (i, 0)),
                   pl.BlockSpec((tm, n), lambda i: (i, 0))],
        out_shape=[jax.ShapeDtypeStruct((m, n), F32), jax.ShapeDtypeStruct((m, n), BF16)],
        compiler_params=_params(("parallel",)),
        name="outproj_ln1",
    )(a, w, x, g, b)


def _ffn_act_kernel(hb_ref, wg_ref, wu_ref, o_ref):
    hb = hb_ref[...]
    gate = jnp.dot(hb, wg_ref[...], preferred_element_type=F32)
    up = jnp.dot(hb, wu_ref[...], preferred_element_type=F32)
    o_ref[...] = (jax.nn.silu(gate) * up).astype(o_ref.dtype)


def _ffn_act(hb, wg, wu, tm, tf):
    m, d = hb.shape
    f = wg.shape[1]
    return pl.pallas_call(
        _ffn_act_kernel,
        grid=(m // tm, f // tf),
        in_specs=[pl.BlockSpec((tm, d), lambda i, j: (i, 0)),
                  pl.BlockSpec((d, tf), lambda i, j: (0, j)),
                  pl.BlockSpec((d, tf), lambda i, j: (0, j))],
        out_specs=pl.BlockSpec((tm, tf), lambda i, j: (i, j)),
        out_shape=jax.ShapeDtypeStruct((m, f), BF16),
        compiler_params=_params(("parallel", "arbitrary")),
        name="swiglu_act",
    )(hb, wg, wu)


def _ffn_down_ln_kernel(a_ref, w_ref, h_ref, g_ref, b_ref, o_ref, *, alpha):
    _matmul_residual_ln(a_ref, w_ref, h_ref, g_ref, b_ref, o_ref, alpha)


def _ffn_down_ln(act, wd, h, g, b, alpha, tm):
    m, f = act.shape
    d = wd.shape[1]
    return pl.pallas_call(
        functools.partial(_ffn_down_ln_kernel, alpha=alpha),
        grid=(m // tm,),
        in_specs=[pl.BlockSpec((tm, f), lambda i: (i, 0)),
                  pl.BlockSpec((f, d), lambda i: (0, 0), pipeline_mode=pl.Buffered(1)),
                  pl.BlockSpec((tm, d), lambda i: (i, 0)),
                  pl.BlockSpec((1, d), lambda i: (0, 0)),
                  pl.BlockSpec((1, d), lambda i: (0, 0))],
        out_specs=pl.BlockSpec((tm, d), lambda i: (i, 0)),
        out_shape=jax.ShapeDtypeStruct((m, d), F32),
        compiler_params=_params(("parallel",)),
        name="swiglu_down_ln2",
    )(act, wd, h, g, b)


def _layer(x2, w_in, b_gate, na_rpb, w_na_out, s5_a_re, s5_a_im, s5_log_dt, s5_b_re, s5_b_im,
           s5_c_re, s5_c_im, s5_d, w_glu, b_glu, w_s5_out, w_out, ln1_g, ln1_b,
           w_ffn_gate, w_ffn_up, w_ffn_down, ln2_g, ln2_b, alpha):
    seq, d_model = x2.shape
    rows = seq // GRID_W
    nc = seq // S5_CHUNK
    gate_col0 = 3 * NA_WIDTH + S5_WIDTH
    in_cols = gate_col0 + 2 * d_model
    row = lambda v: v.astype(F32).reshape(1, -1)

    col_scale = jnp.where(jnp.arange(in_cols) < NA_WIDTH, NA_HEAD_DIM ** -0.5, 1.0).astype(F32).reshape(1, -1)
    col_bias = jnp.concatenate([jnp.zeros((gate_col0,), F32), b_gate.astype(F32)]).reshape(1, -1)
    qkv, u, gates = _inproj(x2, w_in.astype(BF16), col_scale, col_bias, 3 * NA_WIDTH, gate_col0, tm=1024, tn=1024)

    later_weights = [w.astype(F32) for w in (w_na_out, w_glu, w_s5_out, w_out, w_ffn_gate, w_ffn_up, w_ffn_down)]
    att, (w_na_b, w_glu_b, w_s5_b, w_out_b, w_gate_b, w_up_b, w_down_b) = _attention(
        qkv, _attention_bias(na_rpb, rows), seq, later_weights)

    w_state, lags, m_state, a_re, a_im = _s5_tables(s5_a_re, s5_a_im, s5_log_dt, s5_b_re, s5_b_im,
                                                    s5_c_re, s5_c_im, s5_d)
    ug, h_states = _s5_states(u, w_state, a_re, a_im)
    ys = _s5_outputs(ug, h_states, lags, m_state, nsplit=2)

    merged = _glu_merge(att, w_na_b, ys, w_glu_b, row(b_glu), w_s5_b, gates, tm=1024, tn=512)
    h, hb = _outproj_ln(merged, w_out_b, x2, row(ln1_g), row(ln1_b), alpha, tm=512)
    act = _ffn_act(hb, w_gate_b, w_up_b, tm=1024, tf=512)
    return _ffn_down_ln(act, w_down_b, h, row(ln2_g), row(ln2_b), alpha, tm=512)


def kernel(x, w_in, b_gate, na_rpb, w_na_out, s5_a_re, s5_a_im, s5_log_dt, s5_b_re, s5_b_im, s5_c_re, s5_c_im, s5_d, w_glu, b_glu, w_s5_out, w_out, ln1_g, ln1_b, w_ffn_gate, w_ffn_up, w_ffn_down, ln2_g, ln2_b):
    bsz, seq, d_model = x.shape
    depth = w_in.shape[0]
    alpha = (2.0 * depth) ** 0.25
    outs = []
    for bi in range(bsz):
        xb = x[bi]
        for l in range(depth):
            xb = _layer(xb, w_in[l], b_gate[l], na_rpb[l], w_na_out[l], s5_a_re[l], s5_a_im[l], s5_log_dt[l],
                        s5_b_re[l], s5_b_im[l], s5_c_re[l], s5_c_im[l], s5_d[l], w_glu[l], b_glu[l],
                        w_s5_out[l], w_out[l], ln1_g[l], ln1_b[l], w_ffn_gate[l], w_ffn_up[l],
                        w_ffn_down[l], ln2_g[l], ln2_b[l], alpha)
        outs.append(xb)
    return jnp.stack(outs)
```

```python
import functools

import jax
import jax.numpy as jnp
from jax import lax
from jax.experimental import pallas as pl
from jax.experimental.pallas import tpu as pltpu

F32 = jnp.float32
BF16 = jnp.bfloat16

GRID_W = 64
NA_HEADS = 8
NA_HEAD_DIM = 128
NA_WIDTH = NA_HEADS * NA_HEAD_DIM
NA_ROWS = 8
NA_COLS = 16
S5_GROUP = 16
S5_GROUPS = 64
S5_WIDTH = S5_GROUP * S5_GROUPS
S5_STATE = 64
LN_EPS = 1e-5
MASK_VALUE = -1e30

S5_CHUNK = 16
S5_CHUNK_WIDTH = S5_CHUNK * S5_GROUP
S5_STATE_WIDTH = 4 * S5_STATE

ATT_Q_ROWS = 8
ATT_K_ROWS = 16
ATT_Q = ATT_Q_ROWS * GRID_W
ATT_K = ATT_K_ROWS * GRID_W
ATT_KBLK = 256
LANES = 128
ATT_WIN = NA_ROWS * GRID_W + LANES
ATT_HEADS_PER_STEP = 4
ATT_ROW_SPLIT = 2
BF16_SUBLANES = 16
LN_ROW_SPLIT = 4
S5_LANE_GROUPS = LANES // S5_GROUP
RELAYOUT_ROWS = 16

VMEM_LIMIT = 56 * 1024 * 1024


def _params(sem):
    return pltpu.CompilerParams(dimension_semantics=sem, vmem_limit_bytes=VMEM_LIMIT)


def _layer_norm_rows(y, g, b):
    mu = jnp.mean(y, axis=-1, keepdims=True)
    d = y - mu
    var = jnp.mean(d * d, axis=-1, keepdims=True)
    return d * lax.rsqrt(var + LN_EPS) * g + b


def _inproj_kernel(x_ref, w_ref, s_ref, b_ref, qkv_ref, u_ref, g_ref, xb_ref, *, n_blocks, u_tile0, gate_tile0):
    r = pl.program_id(0)
    j = pl.program_id(1)
    slab = x_ref.shape[0]

    @pl.when(r < n_blocks)
    def _():
        xb_ref[r % 2, pl.ds(pl.multiple_of(j * slab, slab), slab), :] = x_ref[...].astype(BF16)

    cur = 1 - r % 2
    busy = r > 0

    @pl.when(busy & (j < u_tile0))
    def _():
        acc = (jnp.dot(xb_ref[cur], w_ref[...], preferred_element_type=F32) * s_ref[...]).astype(qkv_ref.dtype)
        for h in range(qkv_ref.shape[0]):
            qkv_ref[h] = acc[:, h * NA_HEAD_DIM:(h + 1) * NA_HEAD_DIM]

    @pl.when(busy & (j >= u_tile0) & (j < gate_tile0))
    def _():
        u_ref[...] = jnp.dot(xb_ref[cur], w_ref[...], preferred_element_type=F32)

    @pl.when(busy & (j >= gate_tile0))
    def _():
        acc = jnp.dot(xb_ref[cur], w_ref[...], preferred_element_type=F32)
        g_ref[...] = (acc + b_ref[...]).astype(g_ref.dtype)


def _inproj(x, w, col_scale, col_bias, u_col0, gate_col0, tm, tn):
    m, k = x.shape
    n = w.shape[1]
    n_blocks, n_tiles = m // tm, n // tn
    u_tile0, gate_tile0 = u_col0 // tn, gate_col0 // tn
    heads_per_tile = tn // NA_HEAD_DIM
    last_u = gate_tile0 - u_tile0 - 1
    last_g = n_tiles - gate_tile0 - 1
    slab = tm // n_tiles
    blk = lambda r: jnp.maximum(r - 1, 0)
    live = lambda r, t: jnp.where(r == 0, 0, t)
    return pl.pallas_call(
        functools.partial(_inproj_kernel, n_blocks=n_blocks, u_tile0=u_tile0, gate_tile0=gate_tile0),
        grid=(n_blocks + 1, n_tiles),
        in_specs=[
            pl.BlockSpec((slab, k), lambda r, j: (jnp.minimum(r * n_tiles + j, n_blocks * n_tiles - 1), 0)),
            pl.BlockSpec((k, tn), lambda r, j: (0, live(r, j))),
            pl.BlockSpec((1, tn), lambda r, j: (0, j)),
            pl.BlockSpec((1, tn), lambda r, j: (0, j)),
        ],
        out_specs=[pl.BlockSpec((heads_per_tile, tm, NA_HEAD_DIM),
                                lambda r, j: (live(r, jnp.minimum(j, u_tile0 - 1)), blk(r), 0)),
                   pl.BlockSpec((tm, tn), lambda r, j: (blk(r), live(r, jnp.clip(j - u_tile0, 0, last_u)))),
                   pl.BlockSpec((tm, tn), lambda r, j: (blk(r), live(r, jnp.clip(j - gate_tile0, 0, last_g))))],
        out_shape=[jax.ShapeDtypeStruct((u_col0 // NA_HEAD_DIM, m, NA_HEAD_DIM), BF16),
                   jax.ShapeDtypeStruct((m, gate_col0 - u_col0), F32),
                   jax.ShapeDtypeStruct((m, n - gate_col0), BF16)],
        scratch_shapes=[pltpu.VMEM((2, tm, k), BF16)],
        compiler_params=_params(("arbitrary", "arbitrary")),
        name="inproj",
    )(x, w, col_scale, col_bias)


def _attn_windows(rows):
    nb = rows // ATT_Q_ROWS
    kinds = []
    for b in (0, 1, nb - 1):
        base = min(max(b * ATT_Q_ROWS - NA_ROWS // 2, 0), rows - ATT_K_ROWS)
        geo = []
        for ql in range(ATT_Q_ROWS):
            qr = b * ATT_Q_ROWS + ql
            r0 = min(max(qr - NA_ROWS // 2, 0), rows - NA_ROWS)
            koff = r0 - base
            lane0 = min(LANES * (koff * GRID_W // LANES), ATT_K - ATT_WIN)
            geo.append((lane0, koff * GRID_W - lane0, r0 - qr + NA_ROWS - 1))
        kinds.append(tuple(geo))
    return tuple(kinds)


def _attn_kernel(q_ref, k0, k1, k2, k3, v0, v1, v2, v3, bias_ref, *rest, windows, nb, n_cast):
    cast_src, o_ref, cast_dst, p_ref = rest[:n_cast], rest[n_cast], rest[n_cast + 1:2 * n_cast + 1], rest[-1]
    b = pl.program_id(1)
    k_refs, v_refs = (k0, k1, k2, k3), (v0, v1, v2, v3)

    def one_head(hh, geo):
        c = slice(hh * NA_HEAD_DIM, (hh + 1) * NA_HEAD_DIM)
        k = jnp.concatenate([r[hh] for r in k_refs], axis=0)
        v = jnp.concatenate([r[hh] for r in v_refs], axis=0)
        per = ATT_Q_ROWS // ATT_ROW_SPLIT
        for part in range(ATT_ROW_SPLIT):
            sub = geo[part * per:(part + 1) * per]
            lo = min(lane0 for lane0, _, _ in sub)
            hi = max(lane0 for lane0, _, _ in sub) + ATT_WIN
            rows = slice(part * per * GRID_W, (part + 1) * per * GRID_W)
            s = lax.dot_general(q_ref[hh, rows, :], k[lo:hi], (((1,), (1,)), ((), ())),
                                preferred_element_type=F32)
            inv = []
            for i, (lane0, _, _) in enumerate(sub):
                ql = part * per + i
                r = slice(ql * GRID_W, (ql + 1) * GRID_W)
                sw = s[i * GRID_W:(i + 1) * GRID_W, lane0 - lo:lane0 - lo + ATT_WIN] + bias_ref[hh, ql]
                m = jnp.max(sw, axis=-1, keepdims=True)
                p = jnp.exp(sw - m)
                inv.append(1.0 / jnp.sum(p, axis=-1, keepdims=True))
                if lane0 > lo:
                    p_ref[hh, r, lo:lane0] = jnp.zeros((GRID_W, lane0 - lo), BF16)
                p_ref[hh, r, lane0:lane0 + ATT_WIN] = p.astype(BF16)
                if lane0 + ATT_WIN < hi:
                    p_ref[hh, r, lane0 + ATT_WIN:hi] = jnp.zeros((GRID_W, hi - lane0 - ATT_WIN), BF16)
            o = jnp.dot(p_ref[hh, rows, lo:hi], v[lo:hi], preferred_element_type=F32)
            o_ref[rows, c] = (o * jnp.concatenate(inv, axis=0)).astype(o_ref.dtype)

    def block_kind(geo):
        for src, dst in zip(cast_src, cast_dst):
            dst[...] = src[...].astype(dst.dtype)
        for hh in range(ATT_HEADS_PER_STEP):
            one_head(hh, geo)

    @pl.when(b == 0)
    def _():
        block_kind(windows[0])

    @pl.when((b > 0) & (b < nb - 1))
    def _():
        block_kind(windows[1])

    @pl.when(b == nb - 1)
    def _():
        block_kind(windows[2])


def _attention(qkv, bias, seq, weights):
    nb = seq // ATT_Q
    n_kblk = seq // ATT_KBLK
    per_blk = ATT_K // ATT_KBLK

    def kstart(b):
        return jnp.clip(2 * b - 1, 0, n_kblk - per_blk)

    def kind(b):
        return jnp.where(b == 0, 0, jnp.where(b == nb - 1, 2, 1))

    hps = ATT_HEADS_PER_STEP
    width = hps * NA_HEAD_DIM
    n_groups = NA_HEADS // hps

    def kv_spec(group0, t):
        return pl.BlockSpec((hps, ATT_KBLK, NA_HEAD_DIM), lambda h, b: (group0 + h, kstart(b) + t, 0))

    in_specs = [pl.BlockSpec((hps, ATT_Q, NA_HEAD_DIM), lambda h, b: (h, b, 0))]
    in_specs += [kv_spec(n_groups, t) for t in range(per_blk)]
    in_specs += [kv_spec(2 * n_groups, t) for t in range(per_blk)]
    in_specs += [pl.BlockSpec((None, hps, ATT_Q_ROWS, GRID_W, ATT_WIN), lambda h, b: (kind(b), h, 0, 0, 0))]

    n_steps = n_groups * nb

    def slab_spec(w):
        rows_w, cols_w = w.shape
        for rb in range(n_steps, 0, -1):
            cb = n_steps // rb
            if (rb * cb == n_steps and rows_w % rb == 0 and cols_w % cb == 0
                    and (rows_w // rb) % BF16_SUBLANES == 0 and (cols_w // cb) % LANES == 0):
                return pl.BlockSpec((rows_w // rb, cols_w // cb),
                                    lambda h, b, cb=cb: ((h * nb + b) // cb, (h * nb + b) % cb))
        raise ValueError(f"no slab tiling for {w.shape} over {n_steps} steps")

    slab_specs = [slab_spec(w) for w in weights]
    outs = pl.pallas_call(
        functools.partial(_attn_kernel, windows=_attn_windows(seq // GRID_W), nb=nb, n_cast=len(weights)),
        grid=(n_groups, nb),
        in_specs=in_specs + slab_specs,
        out_specs=[pl.BlockSpec((ATT_Q, width), lambda h, b: (b, h))] + slab_specs,
        out_shape=[jax.ShapeDtypeStruct((seq, NA_WIDTH), BF16)]
                  + [jax.ShapeDtypeStruct(w.shape, BF16) for w in weights],
        scratch_shapes=[pltpu.VMEM((hps, ATT_Q, ATT_K), BF16)],
        compiler_params=_params(("parallel", "arbitrary")),
        name="na_attention",
    )(*([qkv] * (1 + 2 * per_blk)), bias, *weights)
    return outs[0], outs[1:]


def _attention_bias(rpb, rows):
    cols = jnp.arange(GRID_W)
    col_start = jnp.clip(cols - NA_COLS // 2, 0, GRID_W - NA_COLS)
    col_ok = (cols[None, :] >= col_start[:, None]) & (cols[None, :] < col_start[:, None] + NA_COLS)
    col_idx = jnp.clip(cols[None, :] - cols[:, None], -(NA_COLS - 1), NA_COLS - 1) + (NA_COLS - 1)
    pick = (jnp.arange(2 * NA_COLS - 1)[:, None, None] == col_idx[None]).astype(F32)
    by_col = jnp.einsum("hrj,jqk->hqrk", rpb.astype(F32), pick, precision=lax.Precision.HIGHEST)
    by_col = jnp.where(col_ok[None, :, None, :], by_col, MASK_VALUE)
    kinds, made = [], {}
    for geo in _attn_windows(rows):
        strips = []
        for _, phase, rfirst in geo:
            if (phase, rfirst) not in made:
                strip = by_col[:, :, rfirst:rfirst + NA_ROWS, :].reshape(NA_HEADS, GRID_W, NA_ROWS * GRID_W)
                made[phase, rfirst] = jnp.pad(strip, ((0, 0), (0, 0), (phase, ATT_WIN - NA_ROWS * GRID_W - phase)),
                                              constant_values=MASK_VALUE)
            strips.append(made[phase, rfirst])
        kinds.append(jnp.stack(strips, axis=1))
    return jnp.stack(kinds)


def _s5_tables(a_re, a_im, log_dt, b_re, b_im, c_re, c_im, d):
    t_len = S5_CHUNK
    groups = a_re.shape[1]
    steps = jnp.arange(t_len, dtype=F32)
    flip = t_len - 1 - steps

    def cmul(xr, xi, yr, yi):
        return xr * yr - xi * yi, xr * yi + xi * yr

    def powers(zr, zi, n):
        mag = jnp.exp(zr * n)
        return mag * jnp.cos(zi * n), mag * jnp.sin(zi * n)

    ar, ai = a_re.astype(F32), a_im.astype(F32)
    dt = jnp.exp(log_dt.astype(F32))[..., None]
    zr, zi = ar * dt, ai * dt
    lr, li = powers(zr, zi, 1.0)
    lr = lr - 1.0
    den = ar * ar + ai * ai
    fr, fi = (lr * ar + li * ai) / den, (li * ar - lr * ai) / den
    swap = lambda v: jnp.swapaxes(v.astype(F32), 2, 3)
    bbr, bbi = cmul(fr[:, :, None, :], fi[:, :, None, :], swap(b_re), swap(b_im))
    crt, cit = swap(c_re), swap(c_im)
    n_w = jnp.stack([flip, steps])[:, None, :, None]
    wp_r, wp_i = powers(zr[:, :, None, :], zi[:, :, None, :], n_w)
    wr, wi = cmul(wp_r[:, :, :, None, :], wp_i[:, :, :, None, :], bbr[:, :, None], bbi[:, :, None])
    wr = wr.reshape(2, groups, S5_CHUNK_WIDTH, S5_STATE)
    wi = wi.reshape(2, groups, S5_CHUNK_WIDTH, S5_STATE)
    n_e = jnp.stack([steps + 1.0, flip + 1.0])[:, None, None, :]
    ep_r, ep_i = powers(zr[..., None], zi[..., None], n_e)
    er, ei = cmul(crt[:, :, :, None, :], cit[:, :, :, None, :], ep_r[..., None], ep_i[..., None])
    er = er.reshape(2, groups, S5_STATE, S5_CHUNK_WIDTH)
    ei = ei.reshape(2, groups, S5_STATE, S5_CHUNK_WIDTH)
    lp_r, lp_i = powers(zr[..., None], zi[..., None], steps)
    xr, xi = cmul(crt[:, :, :, None, :], cit[:, :, :, None, :], lp_r[..., None], lp_i[..., None])
    lhs = jnp.concatenate([bbr, -bbi], axis=-1)
    rhs = jnp.concatenate([xr, xi], axis=2).reshape(2, groups, 2 * S5_STATE, S5_CHUNK_WIDTH)
    lag_kernels = jnp.einsum("dgck,dgkn->dgcn", lhs, rhs, precision=lax.Precision.HIGHEST)
    step_r, step_i = powers(zr, zi, float(t_len))

    w_state = jnp.concatenate([wr[0], wr[1], wi[0], wi[1]], axis=-1)
    m_state = jnp.concatenate([er[0], er[1], -ei[0], -ei[1]], axis=1)
    kf, kb = lag_kernels[0], lag_kernels[1]
    kb_rev = kb[:, :, S5_GROUP:].reshape(groups, S5_GROUP, t_len - 1, S5_GROUP)[:, :, ::-1]
    lag0 = kf[:, :, :S5_GROUP] + kb[:, :, :S5_GROUP] + d.astype(F32)[:, None, :] * jnp.eye(S5_GROUP, dtype=F32)[None]
    lags = jnp.concatenate([kb_rev.reshape(groups, S5_GROUP, -1), lag0, kf[:, :, S5_GROUP:]], axis=-1)
    return (w_state.astype(BF16), lags, m_state.astype(BF16),
            jnp.concatenate([step_r[0], step_r[1]], axis=-1), jnp.concatenate([step_i[0], step_i[1]], axis=-1))


def _lane_group_ids():
    return lax.broadcasted_iota(jnp.int32, (RELAYOUT_ROWS, LANES), 1) // S5_GROUP


def _transpose_lane_blocks(pieces, grp):
    nblk = S5_LANE_GROUPS
    rolled = []
    for r in range(nblk):
        acc = None
        for b in range(nblk):
            acc = pieces[b] if acc is None else jnp.where(grp == (b - r) % nblk, pieces[b], acc)
        rolled.append(pltpu.roll(acc, S5_GROUP * r, 1) if r else acc)
    out = []
    for k in range(nblk):
        acc = None
        for b in range(nblk):
            z = rolled[(b - k) % nblk]
            acc = z if acc is None else jnp.where(grp == b, z, acc)
        out.append(acc)
    return out


def _s5_state_kernel(u_ref, w_ref, ar_ref, ai_ref, ug_ref, h_ref, s_ref):
    lg = S5_LANE_GROUPS
    nc = ug_ref.shape[1]
    grp = _lane_group_ids()

    def regroup(rb, carry):
        r0 = pl.multiple_of(rb * RELAYOUT_ROWS, RELAYOUT_ROWS)
        for half in range(S5_CHUNK_WIDTH // LANES):
            steps = [u_ref[pl.ds(r0 * S5_CHUNK + half * S5_LANE_GROUPS + tp, RELAYOUT_ROWS, stride=S5_CHUNK), :]
                     for tp in range(S5_LANE_GROUPS)]
            for g, v in enumerate(_transpose_lane_blocks(steps, grp)):
                ug_ref[g, pl.ds(r0, RELAYOUT_ROWS), half * LANES:(half + 1) * LANES] = v.astype(BF16)
        return carry

    lax.fori_loop(0, nc // RELAYOUT_ROWS, regroup, 0, unroll=2)

    half = S5_STATE_WIDTH // 2
    for g in range(lg):
        sg = jnp.dot(ug_ref[g], w_ref[g], preferred_element_type=F32)
        s_ref[0, pl.ds(g, nc, stride=lg), :] = sg[:, :half]
        s_ref[1, pl.ds(g, nc, stride=lg), :] = sg[:, half:]

    lane = lax.broadcasted_iota(jnp.int32, (lg, half), 1)
    fwd = lane < S5_STATE
    ar = ar_ref[...]
    ai = ai_ref[...]
    rows = lambda i: pl.ds(pl.multiple_of(i * lg, lg), lg)

    def advance(i, hr, hi):
        sr = jnp.where(fwd, s_ref[0, rows(i), :], s_ref[0, rows(nc - 1 - i), :])
        si = jnp.where(fwd, s_ref[1, rows(i), :], s_ref[1, rows(nc - 1 - i), :])
        return ar * hr - ai * hi + sr, ar * hi + ai * hr + si

    def first_touch(i, carry):
        hr, hi = carry
        for c in (i, nc - 1 - i):
            h_ref[0, rows(c), :] = hr
            h_ref[1, rows(c), :] = hi
        return advance(i, hr, hi)

    def second_touch(i, carry):
        hr, hi = carry
        j = nc - 1 - i
        h_ref[0, rows(i), :] = jnp.where(fwd, hr, h_ref[0, rows(i), :])
        h_ref[1, rows(i), :] = jnp.where(fwd, hi, h_ref[1, rows(i), :])
        h_ref[0, rows(j), :] = jnp.where(fwd, h_ref[0, rows(j), :], hr)
        h_ref[1, rows(j), :] = jnp.where(fwd, h_ref[1, rows(j), :], hi)
        return advance(i, hr, hi)

    zero = jnp.zeros((lg, half), F32)
    carry = lax.fori_loop(0, nc // 2, first_touch, (zero, zero))
    lax.fori_loop(nc // 2, nc, second_touch, carry)


def _s5_states(u, w_state, a_re, a_im):
    seq, width = u.shape
    g = width // S5_GROUP
    nc = seq // S5_CHUNK
    lg = S5_LANE_GROUPS
    half = S5_STATE_WIDTH // 2
    return pl.pallas_call(
        _s5_state_kernel,
        grid=(width // LANES,),
        in_specs=[pl.BlockSpec((seq, LANES), lambda j: (0, j)),
                  pl.BlockSpec((lg, S5_CHUNK_WIDTH, S5_STATE_WIDTH), lambda j: (j, 0, 0)),
                  pl.BlockSpec((lg, half), lambda j: (j, 0)),
                  pl.BlockSpec((lg, half), lambda j: (j, 0))],
        out_specs=[pl.BlockSpec((lg, nc, S5_CHUNK_WIDTH), lambda j: (j, 0, 0)),
                   pl.BlockSpec((None, 2, nc * lg, half), lambda j: (j, 0, 0, 0))],
        out_shape=[jax.ShapeDtypeStruct((g, nc, S5_CHUNK_WIDTH), BF16),
                   jax.ShapeDtypeStruct((g // lg, 2, nc * lg, half), F32)],
        scratch_shapes=[pltpu.VMEM((2, nc * lg, half), F32)],
        compiler_params=_params(("parallel",)),
        name="s5_states",
    )(u, w_state, a_re, a_im)


def _s5_out_kernel(ug_ref, h_ref, lag_ref, m_ref, o_ref, y_ref, toe_ref):
    cw = S5_CHUNK_WIDTH
    nch = ug_ref.shape[1]
    for g in range(S5_LANE_GROUPS):
        lag = lag_ref[g]
        for s in range(S5_CHUNK):
            first = S5_GROUP * (S5_CHUNK - 1 - s)
            toe_ref[g, s * S5_GROUP:(s + 1) * S5_GROUP, :] = lag[:, first:first + cw].astype(BF16)
    for g in range(S5_LANE_GROUPS):
        hg = jnp.concatenate([h_ref[part, pl.ds(g, nch, stride=S5_LANE_GROUPS), :] for part in range(2)],
                             axis=1).astype(BF16)
        y = jnp.dot(ug_ref[g], toe_ref[g], preferred_element_type=F32)
        y = y + jnp.dot(hg, m_ref[g], preferred_element_type=F32)
        y_ref[g] = jax.nn.gelu(y, approximate=True)
    grp = _lane_group_ids()

    def regroup(rb, carry):
        r0 = pl.multiple_of(rb * RELAYOUT_ROWS, RELAYOUT_ROWS)
        for half in range(cw // LANES):
            groups = [y_ref[g, pl.ds(r0, RELAYOUT_ROWS), half * LANES:(half + 1) * LANES]
                      for g in range(S5_LANE_GROUPS)]
            for tp, v in enumerate(_transpose_lane_blocks(groups, grp)):
                t = half * S5_LANE_GROUPS + tp
                o_ref[pl.ds(r0 * S5_CHUNK + t, RELAYOUT_ROWS, stride=S5_CHUNK), :] = v
        return carry

    lax.fori_loop(0, nch // RELAYOUT_ROWS, regroup, 0, unroll=2)


def _s5_outputs(ug, h3, lags, m_state, nsplit):
    g, nc, cw = ug.shape
    nch = nc // nsplit
    lg = S5_LANE_GROUPS
    return pl.pallas_call(
        _s5_out_kernel,
        grid=(g // lg, nsplit),
        in_specs=[pl.BlockSpec((lg, nch, cw), lambda j, h: (j, h, 0)),
                  pl.BlockSpec((None, 2, nch * lg, S5_STATE_WIDTH // 2), lambda j, h: (j, 0, h, 0)),
                  pl.BlockSpec((lg,) + lags.shape[1:], lambda j, h: (j, 0, 0)),
                  pl.BlockSpec((lg, S5_STATE_WIDTH, cw), lambda j, h: (j, 0, 0))],
        out_specs=pl.BlockSpec((nch * S5_CHUNK, LANES), lambda j, h: (h, j)),
        out_shape=jax.ShapeDtypeStruct((nc * S5_CHUNK, g * S5_GROUP), F32),
        scratch_shapes=[pltpu.VMEM((lg, nch, cw), F32), pltpu.VMEM((lg, cw, cw), BF16)],
        compiler_params=_params(("parallel", "parallel")),
        name="s5_outputs",
    )(ug, h3, lags, m_state)


def _merge_kernel(a1_ref, w1_ref, ys_ref, wg_ref, bg_ref, w2_ref, g1_ref, g2_ref, o_ref, ysb_ref, z_ref, *, tg):
    @pl.when(pl.program_id(1) == 0)
    def _():
        ysb_ref[...] = ys_ref[...].astype(BF16)
        for c in range(z_ref.shape[1] // tg):
            cols = slice(c * tg, (c + 1) * tg)
            t = jnp.dot(ysb_ref[...], wg_ref[:, cols], preferred_element_type=F32) + bg_ref[:, cols]
            z_ref[:, cols] = (ys_ref[:, cols] * jax.nn.sigmoid(t)).astype(BF16)

    y1 = jnp.dot(a1_ref[...], w1_ref[...], preferred_element_type=F32)
    y2 = jnp.dot(z_ref[...], w2_ref[...], preferred_element_type=F32)
    g1 = jax.nn.sigmoid(g1_ref[...].astype(F32))
    g2 = jax.nn.sigmoid(g2_ref[...].astype(F32))
    o_ref[...] = (g1 * y1 + g2 * y2).astype(o_ref.dtype)


def _glu_merge(att, w_na, ys, w_glu, b_glu, w_s5, gates, tm, tn):
    m, k1 = att.shape
    k2 = ys.shape[1]
    n = w_na.shape[1]
    g1 = 0
    g2 = n // tn
    return pl.pallas_call(
        functools.partial(_merge_kernel, tg=tn),
        grid=(m // tm, n // tn),
        in_specs=[pl.BlockSpec((tm, k1), lambda i, j: (i, 0)),
                  pl.BlockSpec((k1, tn), lambda i, j: (0, j)),
                  pl.BlockSpec((tm, k2), lambda i, j: (i, 0)),
                  pl.BlockSpec((k2, k2), lambda i, j: (0, 0)),
                  pl.BlockSpec((1, k2), lambda i, j: (0, 0)),
                  pl.BlockSpec((k2, tn), lambda i, j: (0, j)),
                  pl.BlockSpec((tm, tn), lambda i, j: (i, g1 + j)),
                  pl.BlockSpec((tm, tn), lambda i, j: (i, g2 + j))],
        out_specs=pl.BlockSpec((tm, tn), lambda i, j: (i, j)),
        out_shape=jax.ShapeDtypeStruct((m, n), BF16),
        scratch_shapes=[pltpu.VMEM((tm, k2), BF16), pltpu.VMEM((tm, k2), BF16)],
        compiler_params=_params(("parallel", "arbitrary")),
        name="glu_gated_merge",
    )(att, w_na, ys, w_glu, b_glu, w_s5, gates, gates)


def _matmul_residual_ln(a_ref, w_ref, res_ref, g_ref, b_ref, o_ref, alpha, ob_ref=None):
    sub = a_ref.shape[0] // LN_ROW_SPLIT
    for r in range(LN_ROW_SPLIT):
        rows = slice(r * sub, (r + 1) * sub)
        y = jnp.dot(a_ref[rows, :], w_ref[...], preferred_element_type=F32)
        out = _layer_norm_rows(alpha * res_ref[rows, :] + y, g_ref[...], b_ref[...])
        o_ref[rows, :] = out
        if ob_ref is not None:
            ob_ref[rows, :] = out.astype(ob_ref.dtype)


def _outproj_ln_kernel(a_ref, w_ref, x_ref, g_ref, b_ref, h_ref, hb_ref, *, alpha):
    _matmul_residual_ln(a_ref, w_ref, x_ref, g_ref, b_ref, h_ref, alpha, hb_ref)


def _outproj_ln(a, w, x, g, b, alpha, tm):
    m, k = a.shape
    n = w.shape[1]
    return pl.pallas_call(
        functools.partial(_outproj_ln_kernel, alpha=alpha),
        grid=(m // tm,),
        in_specs=[pl.BlockSpec((tm, k), lambda i: (i, 0)),
                  pl.BlockSpec((k, n), lambda i: (0, 0)),
                  pl.BlockSpec((tm, n), lambda i: (i, 0)),
                  pl.BlockSpec((1, n), lambda i: (0, 0)),
                  pl.BlockSpec((1, n), lambda i: (0, 0))],
        out_specs=[pl.BlockSpec((tm, n), lambda i: (i, 0)),
                   pl.BlockSpec((tm, n), lambda i: (i, 0))],
        out_shape=[jax.ShapeDtypeStruct((m, n), F32), jax.ShapeDtypeStruct((m, n), BF16)],
        compiler_params=_params(("parallel",)),
        name="outproj_ln1",
    )(a, w, x, g, b)


def _ffn_act_kernel(hb_ref, wg_ref, wu_ref, o_ref):
    hb = hb_ref[...]
    gate = jnp.dot(hb, wg_ref[...], preferred_element_type=F32)
    up = jnp.dot(hb, wu_ref[...], preferred_element_type=F32)
    o_ref[...] = (jax.nn.silu(gate) * up).astype(o_ref.dtype)


def _ffn_act(hb, wg, wu, tm, tf):
    m, d = hb.shape
    f = wg.shape[1]
    return pl.pallas_call(
        _ffn_act_kernel,
        grid=(m // tm, f // tf),
        in_specs=[pl.BlockSpec((tm, d), lambda i, j: (i, 0)),
                  pl.BlockSpec((d, tf), lambda i, j: (0, j)),
                  pl.BlockSpec((d, tf), lambda i, j: (0, j))],
        out_specs=pl.BlockSpec((tm, tf), lambda i, j: (i, j)),
        out_shape=jax.ShapeDtypeStruct((m, f), BF16),
        compiler_params=_params(("parallel", "arbitrary")),
        name="swiglu_act",
    )(hb, wg, wu)


def _ffn_down_ln_kernel(a_ref, w_ref, h_ref, g_ref, b_ref, o_ref, *, alpha):
    _matmul_residual_ln(a_ref, w_ref, h_ref, g_ref, b_ref, o_ref, alpha)


def _ffn_down_ln(act, wd, h, g, b, alpha, tm):
    m, f = act.shape
    d = wd.shape[1]
    return pl.pallas_call(
        functools.partial(_ffn_down_ln_kernel, alpha=alpha),
        grid=(m // tm,),
        in_specs=[pl.BlockSpec((tm, f), lambda i: (i, 0)),
                  pl.BlockSpec((f, d), lambda i: (0, 0), pipeline_mode=pl.Buffered(1)),
                  pl.BlockSpec((tm, d), lambda i: (i, 0)),
                  pl.BlockSpec((1, d), lambda i: (0, 0)),
                  pl.BlockSpec((1, d), lambda i: (0, 0))],
        out_specs=pl.BlockSpec((tm, d), lambda i: (i, 0)),
        out_shape=jax.ShapeDtypeStruct((m, d), F32),
        compiler_params=_params(("parallel",)),
        name="swiglu_down_ln2",
    )(act, wd, h, g, b)


def _layer(x2, w_in, b_gate, na_rpb, w_na_out, s5_a_re, s5_a_im, s5_log_dt, s5_b_re, s5_b_im,
           s5_c_re, s5_c_im, s5_d, w_glu, b_glu, w_s5_out, w_out, ln1_g, ln1_b,
           w_ffn_gate, w_ffn_up, w_ffn_down, ln2_g, ln2_b, alpha):
    seq, d_model = x2.shape
    rows = seq // GRID_W
    nc = seq // S5_CHUNK
    gate_col0 = 3 * NA_WIDTH + S5_WIDTH
    in_cols = gate_col0 + 2 * d_model
    row = lambda v: v.astype(F32).reshape(1, -1)

    col_scale = jnp.where(jnp.arange(in_cols) < NA_WIDTH, NA_HEAD_DIM ** -0.5, 1.0).astype(F32).reshape(1, -1)
    col_bias = jnp.concatenate([jnp.zeros((gate_col0,), F32), b_gate.astype(F32)]).reshape(1, -1)
    qkv, u, gates = _inproj(x2, w_in.astype(BF16), col_scale, col_bias, 3 * NA_WIDTH, gate_col0, tm=1024, tn=1024)

    later_weights = [w.astype(F32) for w in (w_na_out, w_glu, w_s5_out, w_out, w_ffn_gate, w_ffn_up, w_ffn_down)]
    att, (w_na_b, w_glu_b, w_s5_b, w_out_b, w_gate_b, w_up_b, w_down_b) = _attention(
        qkv, _attention_bias(na_rpb, rows), seq, later_weights)

    w_state, lags, m_state, a_re, a_im = _s5_tables(s5_a_re, s5_a_im, s5_log_dt, s5_b_re, s5_b_im,
                                                    s5_c_re, s5_c_im, s5_d)
    ug, h_states = _s5_states(u, w_state, a_re, a_im)
    ys = _s5_outputs(ug, h_states, lags, m_state, nsplit=2)

    merged = _glu_merge(att, w_na_b, ys, w_glu_b, row(b_glu), w_s5_b, gates, tm=1024, tn=1024)
    h, hb = _outproj_ln(merged, w_out_b, x2, row(ln1_g), row(ln1_b), alpha, tm=512)
    act = _ffn_act(hb, w_gate_b, w_up_b, tm=1024, tf=512)
    return _ffn_down_ln(act, w_down_b, h, row(ln2_g), row(ln2_b), alpha, tm=512)


def kernel(x, w_in, b_gate, na_rpb, w_na_out, s5_a_re, s5_a_im, s5_log_dt, s5_b_re, s5_b_im, s5_c_re, s5_c_im, s5_d, w_glu, b_glu, w_s5_out, w_out, ln1_g, ln1_b, w_ffn_gate, w_ffn_up, w_ffn_down, ln2_g, ln2_b):
    bsz, seq, d_model = x.shape
    depth = w_in.shape[0]
    alpha = (2.0 * depth) ** 0.25
    outs = []
    for bi in range(bsz):
        xb = x[bi]
        for l in range(depth):
            xb = _layer(xb, w_in[l], b_gate[l], na_rpb[l], w_na_out[l], s5_a_re[l], s5_a_im[l], s5_log_dt[l],
                        s5_b_re[l], s5_b_im[l], s5_c_re[l], s5_c_im[l], s5_d[l], w_glu[l], b_glu[l],
                        w_s5_out[l], w_out[l], ln1_g[l], ln1_b[l], w_ffn_gate[l], w_ffn_up[l],
                        w_ffn_down[l], ln2_g[l], ln2_b[l], alpha)
        outs.append(xb)
    return jnp.stack(outs)
```

```python
import functools

import jax
import jax.numpy as jnp
from jax import lax
from jax.experimental import pallas as pl
from jax.experimental.pallas import tpu as pltpu

F32 = jnp.float32
BF16 = jnp.bfloat16

GRID_W = 64
NA_HEADS = 8
NA_HEAD_DIM = 128
NA_WIDTH = NA_HEADS * NA_HEAD_DIM
NA_ROWS = 8
NA_COLS = 16
S5_GROUP = 16
S5_GROUPS = 64
S5_WIDTH = S5_GROUP * S5_GROUPS
S5_STATE = 64
LN_EPS = 1e-5
MASK_VALUE = -1e30

S5_CHUNK = 16
S5_CHUNK_WIDTH = S5_CHUNK * S5_GROUP
S5_STATE_WIDTH = 4 * S5_STATE

ATT_Q_ROWS = 8
ATT_K_ROWS = 16
ATT_Q = ATT_Q_ROWS * GRID_W
ATT_K = ATT_K_ROWS * GRID_W
ATT_KBLK = 256
LANES = 128
ATT_WIN = NA_ROWS * GRID_W + LANES
ATT_HEADS_PER_STEP = 4
ATT_ROW_SPLIT = 2
BF16_SUBLANES = 16
LN_ROW_SPLIT = 4
S5_LANE_GROUPS = LANES // S5_GROUP
RELAYOUT_ROWS = 16

VMEM_LIMIT = 56 * 1024 * 1024


def _params(sem):
    return pltpu.CompilerParams(dimension_semantics=sem, vmem_limit_bytes=VMEM_LIMIT)


def _layer_norm_rows(y, g, b):
    mu = jnp.mean(y, axis=-1, keepdims=True)
    d = y - mu
    var = jnp.mean(d * d, axis=-1, keepdims=True)
    return d * lax.rsqrt(var + LN_EPS) * g + b


def _inproj_kernel(x_ref, w_ref, s_ref, b_ref, qkv_ref, u_ref, g_ref, xb_ref, *, n_blocks, u_tile0, gate_tile0):
    r = pl.program_id(0)
    j = pl.program_id(1)
    slab = x_ref.shape[0]

    @pl.when(r < n_blocks)
    def _():
        xb_ref[r % 2, pl.ds(pl.multiple_of(j * slab, slab), slab), :] = x_ref[...].astype(BF16)

    cur = 1 - r % 2
    busy = r > 0

    @pl.when(busy & (j < u_tile0))
    def _():
        acc = (jnp.dot(xb_ref[cur], w_ref[...], preferred_element_type=F32) * s_ref[...]).astype(qkv_ref.dtype)
        for h in range(qkv_ref.shape[0]):
            qkv_ref[h] = acc[:, h * NA_HEAD_DIM:(h + 1) * NA_HEAD_DIM]

    @pl.when(busy & (j >= u_tile0) & (j < gate_tile0))
    def _():
        u_ref[...] = jnp.dot(xb_ref[cur], w_ref[...], preferred_element_type=F32)

    @pl.when(busy & (j >= gate_tile0))
    def _():
        acc = jnp.dot(xb_ref[cur], w_ref[...], preferred_element_type=F32)
        g_ref[...] = (acc + b_ref[...]).astype(g_ref.dtype)


def _inproj(x, w, col_scale, col_bias, u_col0, gate_col0, tm, tn):
    m, k = x.shape
    n = w.shape[1]
    n_blocks, n_tiles = m // tm, n // tn
    u_tile0, gate_tile0 = u_col0 // tn, gate_col0 // tn
    heads_per_tile = tn // NA_HEAD_DIM
    last_u = gate_tile0 - u_tile0 - 1
    last_g = n_tiles - gate_tile0 - 1
    slab = tm // n_tiles
    blk = lambda r: jnp.maximum(r - 1, 0)
    live = lambda r, t: jnp.where(r == 0, 0, t)
    return pl.pallas_call(
        functools.partial(_inproj_kernel, n_blocks=n_blocks, u_tile0=u_tile0, gate_tile0=gate_tile0),
        grid=(n_blocks + 1, n_tiles),
        in_specs=[
            pl.BlockSpec((slab, k), lambda r, j: (jnp.minimum(r * n_tiles + j, n_blocks * n_tiles - 1), 0)),
            pl.BlockSpec((k, tn), lambda r, j: (0, live(r, j))),
            pl.BlockSpec((1, tn), lambda r, j: (0, j)),
            pl.BlockSpec((1, tn), lambda r, j: (0, j)),
        ],
        out_specs=[pl.BlockSpec((heads_per_tile, tm, NA_HEAD_DIM),
                                lambda r, j: (live(r, jnp.minimum(j, u_tile0 - 1)), blk(r), 0)),
                   pl.BlockSpec((tm, tn), lambda r, j: (blk(r), live(r, jnp.clip(j - u_tile0, 0, last_u)))),
                   pl.BlockSpec((tm, tn), lambda r, j: (blk(r), live(r, jnp.clip(j - gate_tile0, 0, last_g))))],
        out_shape=[jax.ShapeDtypeStruct((u_col0 // NA_HEAD_DIM, m, NA_HEAD_DIM), BF16),
                   jax.ShapeDtypeStruct((m, gate_col0 - u_col0), F32),
                   jax.ShapeDtypeStruct((m, n - gate_col0), BF16)],
        scratch_shapes=[pltpu.VMEM((2, tm, k), BF16)],
        compiler_params=_params(("arbitrary", "arbitrary")),
        name="inproj",
    )(x, w, col_scale, col_bias)


def _attn_windows(rows):
    nb = rows // ATT_Q_ROWS
    kinds = []
    for b in (0, 1, nb - 1):
        base = min(max(b * ATT_Q_ROWS - NA_ROWS // 2, 0), rows - ATT_K_ROWS)
        geo = []
        for ql in range(ATT_Q_ROWS):
            qr = b * ATT_Q_ROWS + ql
            r0 = min(max(qr - NA_ROWS // 2, 0), rows - NA_ROWS)
            koff = r0 - base
            lane0 = min(LANES * (koff * GRID_W // LANES), ATT_K - ATT_WIN)
            geo.append((lane0, koff * GRID_W - lane0, r0 - qr + NA_ROWS - 1))
        kinds.append(tuple(geo))
    return tuple(kinds)


def _attn_kernel(q_ref, k0, k1, k2, k3, v0, v1, v2, v3, bias_ref, *rest, windows, nb, n_cast):
    cast_src, o_ref, cast_dst, p_ref = rest[:n_cast], rest[n_cast], rest[n_cast + 1:2 * n_cast + 1], rest[-1]
    b = pl.program_id(1)
    k_refs, v_refs = (k0, k1, k2, k3), (v0, v1, v2, v3)

    def one_head(hh, geo):
        c = slice(hh * NA_HEAD_DIM, (hh + 1) * NA_HEAD_DIM)
        k = jnp.concatenate([r[hh] for r in k_refs], axis=0)
        v = jnp.concatenate([r[hh] for r in v_refs], axis=0)
        per = ATT_Q_ROWS // ATT_ROW_SPLIT
        for part in range(ATT_ROW_SPLIT):
            sub = geo[part * per:(part + 1) * per]
            cover = [(lane0 + phase - phase % LANES, phase - phase % LANES,
                      NA_ROWS * GRID_W if phase % LANES == 0 else ATT_WIN) for lane0, phase, _ in sub]
            lo = min(start for start, _, _ in cover)
            hi = max(start + width for start, _, width in cover)
            rows = slice(part * per * GRID_W, (part + 1) * per * GRID_W)
            s = lax.dot_general(q_ref[hh, rows, :], k[lo:hi], (((1,), (1,)), ((), ())),
                                preferred_element_type=F32)
            inv = []
            for i, (start, skip, width) in enumerate(cover):
                ql = part * per + i
                r = slice(ql * GRID_W, (ql + 1) * GRID_W)
                sw = s[i * GRID_W:(i + 1) * GRID_W, start - lo:start - lo + width] + bias_ref[hh, ql, :, skip:skip + width]
                m = jnp.max(sw, axis=-1, keepdims=True)
                p = jnp.exp(sw - m)
                inv.append(1.0 / jnp.sum(p, axis=-1, keepdims=True))
                if start > lo:
                    p_ref[hh, r, lo:start] = jnp.zeros((GRID_W, start - lo), BF16)
                p_ref[hh, r, start:start + width] = p.astype(BF16)
                if start + width < hi:
                    p_ref[hh, r, start + width:hi] = jnp.zeros((GRID_W, hi - start - width), BF16)
            o = jnp.dot(p_ref[hh, rows, lo:hi], v[lo:hi], preferred_element_type=F32)
            o_ref[rows, c] = (o * jnp.concatenate(inv, axis=0)).astype(o_ref.dtype)

    def block_kind(geo):
        for src, dst in zip(cast_src, cast_dst):
            dst[...] = src[...].astype(dst.dtype)
        for hh in range(ATT_HEADS_PER_STEP):
            one_head(hh, geo)

    @pl.when(b == 0)
    def _():
        block_kind(windows[0])

    @pl.when((b > 0) & (b < nb - 1))
    def _():
        block_kind(windows[1])

    @pl.when(b == nb - 1)
    def _():
        block_kind(windows[2])


def _attention(qkv, bias, seq, weights):
    nb = seq // ATT_Q
    n_kblk = seq // ATT_KBLK
    per_blk = ATT_K // ATT_KBLK

    def kstart(b):
        return jnp.clip(2 * b - 1, 0, n_kblk - per_blk)

    def kind(b):
        return jnp.where(b == 0, 0, jnp.where(b == nb - 1, 2, 1))

    hps = ATT_HEADS_PER_STEP
    width = hps * NA_HEAD_DIM
    n_groups = NA_HEADS // hps

    def kv_spec(group0, t):
        return pl.BlockSpec((hps, ATT_KBLK, NA_HEAD_DIM), lambda h, b: (group0 + h, kstart(b) + t, 0))

    in_specs = [pl.BlockSpec((hps, ATT_Q, NA_HEAD_DIM), lambda h, b: (h, b, 0))]
    in_specs += [kv_spec(n_groups, t) for t in range(per_blk)]
    in_specs += [kv_spec(2 * n_groups, t) for t in range(per_blk)]
    in_specs += [pl.BlockSpec((None, hps, ATT_Q_ROWS, GRID_W, ATT_WIN), lambda h, b: (kind(b), h, 0, 0, 0))]

    n_steps = n_groups * nb

    def slab_spec(w):
        rows_w, cols_w = w.shape
        for rb in range(n_steps, 0, -1):
            cb = n_steps // rb
            if (rb * cb == n_steps and rows_w % rb == 0 and cols_w % cb == 0
                    and (rows_w // rb) % BF16_SUBLANES == 0 and (cols_w // cb) % LANES == 0):
                return pl.BlockSpec((rows_w // rb, cols_w // cb),
                                    lambda h, b, cb=cb: ((h * nb + b) // cb, (h * nb + b) % cb))
        raise ValueError(f"no slab tiling for {w.shape} over {n_steps} steps")

    slab_specs = [slab_spec(w) for w in weights]
    outs = pl.pallas_call(
        functools.partial(_attn_kernel, windows=_attn_windows(seq // GRID_W), nb=nb, n_cast=len(weights)),
        grid=(n_groups, nb),
        in_specs=in_specs + slab_specs,
        out_specs=[pl.BlockSpec((ATT_Q, width), lambda h, b: (b, h))] + slab_specs,
        out_shape=[jax.ShapeDtypeStruct((seq, NA_WIDTH), BF16)]
                  + [jax.ShapeDtypeStruct(w.shape, BF16) for w in weights],
        scratch_shapes=[pltpu.VMEM((hps, ATT_Q, ATT_K), BF16)],
        compiler_params=_params(("parallel", "arbitrary")),
        name="na_attention",
    )(*([qkv] * (1 + 2 * per_blk)), bias, *weights)
    return outs[0], outs[1:]


def _attention_bias(rpb, rows):
    cols = jnp.arange(GRID_W)
    col_start = jnp.clip(cols - NA_COLS // 2, 0, GRID_W - NA_COLS)
    col_ok = (cols[None, :] >= col_start[:, None]) & (cols[None, :] < col_start[:, None] + NA_COLS)
    col_idx = jnp.clip(cols[None, :] - cols[:, None], -(NA_COLS - 1), NA_COLS - 1) + (NA_COLS - 1)
    pick = (jnp.arange(2 * NA_COLS - 1)[:, None, None] == col_idx[None]).astype(F32)
    by_col = jnp.einsum("hrj,jqk->hqrk", rpb.astype(F32), pick, precision=lax.Precision.HIGHEST)
    by_col = jnp.where(col_ok[None, :, None, :], by_col, MASK_VALUE)
    kinds, made = [], {}
    for geo in _attn_windows(rows):
        strips = []
        for _, phase, rfirst in geo:
            if (phase, rfirst) not in made:
                strip = by_col[:, :, rfirst:rfirst + NA_ROWS, :].reshape(NA_HEADS, GRID_W, NA_ROWS * GRID_W)
                made[phase, rfirst] = jnp.pad(strip, ((0, 0), (0, 0), (phase, ATT_WIN - NA_ROWS * GRID_W - phase)),
                                              constant_values=MASK_VALUE)
            strips.append(made[phase, rfirst])
        kinds.append(jnp.stack(strips, axis=1))
    return jnp.stack(kinds)


def _s5_tables(a_re, a_im, log_dt, b_re, b_im, c_re, c_im, d):
    t_len = S5_CHUNK
    groups = a_re.shape[1]
    steps = jnp.arange(t_len, dtype=F32)
    flip = t_len - 1 - steps

    def cmul(xr, xi, yr, yi):
        return xr * yr - xi * yi, xr * yi + xi * yr

    def powers(zr, zi, n):
        mag = jnp.exp(zr * n)
        return mag * jnp.cos(zi * n), mag * jnp.sin(zi * n)

    ar, ai = a_re.astype(F32), a_im.astype(F32)
    dt = jnp.exp(log_dt.astype(F32))[..., None]
    zr, zi = ar * dt, ai * dt
    lr, li = powers(zr, zi, 1.0)
    lr = lr - 1.0
    den = ar * ar + ai * ai
    fr, fi = (lr * ar + li * ai) / den, (li * ar - lr * ai) / den
    swap = lambda v: jnp.swapaxes(v.astype(F32), 2, 3)
    bbr, bbi = cmul(fr[:, :, None, :], fi[:, :, None, :], swap(b_re), swap(b_im))
    crt, cit = swap(c_re), swap(c_im)
    n_w = jnp.stack([flip, steps])[:, None, :, None]
    wp_r, wp_i = powers(zr[:, :, None, :], zi[:, :, None, :], n_w)
    wr, wi = cmul(wp_r[:, :, :, None, :], wp_i[:, :, :, None, :], bbr[:, :, None], bbi[:, :, None])
    wr = wr.reshape(2, groups, S5_CHUNK_WIDTH, S5_STATE)
    wi = wi.reshape(2, groups, S5_CHUNK_WIDTH, S5_STATE)
    n_e = jnp.stack([steps + 1.0, flip + 1.0])[:, None, None, :]
    ep_r, ep_i = powers(zr[..., None], zi[..., None], n_e)
    er, ei = cmul(crt[:, :, :, None, :], cit[:, :, :, None, :], ep_r[..., None], ep_i[..., None])
    er = er.reshape(2, groups, S5_STATE, S5_CHUNK_WIDTH)
    ei = ei.reshape(2, groups, S5_STATE, S5_CHUNK_WIDTH)
    lp_r, lp_i = powers(zr[..., None], zi[..., None], steps)
    xr, xi = cmul(crt[:, :, :, None, :], cit[:, :, :, None, :], lp_r[..., None], lp_i[..., None])
    lhs = jnp.concatenate([bbr, -bbi], axis=-1)
    rhs = jnp.concatenate([xr, xi], axis=2).reshape(2, groups, 2 * S5_STATE, S5_CHUNK_WIDTH)
    lag_kernels = jnp.einsum("dgck,dgkn->dgcn", lhs, rhs, precision=lax.Precision.HIGHEST)
    step_r, step_i = powers(zr, zi, float(t_len))

    w_state = jnp.concatenate([wr[0], wr[1], wi[0], wi[1]], axis=-1)
    m_state = jnp.concatenate([er[0], er[1], -ei[0], -ei[1]], axis=1)
    kf, kb = lag_kernels[0], lag_kernels[1]
    kb_rev = kb[:, :, S5_GROUP:].reshape(groups, S5_GROUP, t_len - 1, S5_GROUP)[:, :, ::-1]
    lag0 = kf[:, :, :S5_GROUP] + kb[:, :, :S5_GROUP] + d.astype(F32)[:, None, :] * jnp.eye(S5_GROUP, dtype=F32)[None]
    lags = jnp.concatenate([kb_rev.reshape(groups, S5_GROUP, -1), lag0, kf[:, :, S5_GROUP:]], axis=-1)
    return (w_state.astype(BF16), lags, m_state.astype(BF16),
            jnp.concatenate([step_r[0], step_r[1]], axis=-1), jnp.concatenate([step_i[0], step_i[1]], axis=-1))


def _lane_group_ids():
    return lax.broadcasted_iota(jnp.int32, (RELAYOUT_ROWS, LANES), 1) // S5_GROUP


def _transpose_lane_blocks(pieces, grp):
    nblk = S5_LANE_GROUPS
    rolled = []
    for r in range(nblk):
        acc = None
        for b in range(nblk):
            acc = pieces[b] if acc is None else jnp.where(grp == (b - r) % nblk, pieces[b], acc)
        rolled.append(pltpu.roll(acc, S5_GROUP * r, 1) if r else acc)
    out = []
    for k in range(nblk):
        acc = None
        for b in range(nblk):
            z = rolled[(b - k) % nblk]
            acc = z if acc is None else jnp.where(grp == b, z, acc)
        out.append(acc)
    return out


def _s5_state_kernel(u_ref, w_ref, ar_ref, ai_ref, ug_ref, h_ref, s_ref):
    lg = S5_LANE_GROUPS
    nc = ug_ref.shape[1]
    grp = _lane_group_ids()

    def regroup(rb, carry):
        r0 = pl.multiple_of(rb * RELAYOUT_ROWS, RELAYOUT_ROWS)
        for half in range(S5_CHUNK_WIDTH // LANES):
            steps = [u_ref[pl.ds(r0 * S5_CHUNK + half * S5_LANE_GROUPS + tp, RELAYOUT_ROWS, stride=S5_CHUNK), :]
                     for tp in range(S5_LANE_GROUPS)]
            for g, v in enumerate(_transpose_lane_blocks(steps, grp)):
                ug_ref[g, pl.ds(r0, RELAYOUT_ROWS), half * LANES:(half + 1) * LANES] = v.astype(BF16)
        return carry

    lax.fori_loop(0, nc // RELAYOUT_ROWS, regroup, 0, unroll=2)

    half = S5_STATE_WIDTH // 2
    for g in range(lg):
        sg = jnp.dot(ug_ref[g], w_ref[g], preferred_element_type=F32)
        s_ref[0, pl.ds(g, nc, stride=lg), :] = sg[:, :half]
        s_ref[1, pl.ds(g, nc, stride=lg), :] = sg[:, half:]

    lane = lax.broadcasted_iota(jnp.int32, (lg, half), 1)
    fwd = lane < S5_STATE
    ar = ar_ref[...]
    ai = ai_ref[...]
    rows = lambda i: pl.ds(pl.multiple_of(i * lg, lg), lg)

    def advance(i, hr, hi):
        sr = jnp.where(fwd, s_ref[0, rows(i), :], s_ref[0, rows(nc - 1 - i), :])
        si = jnp.where(fwd, s_ref[1, rows(i), :], s_ref[1, rows(nc - 1 - i), :])
        return ar * hr - ai * hi + sr, ar * hi + ai * hr + si

    def first_touch(i, carry):
        hr, hi = carry
        for c in (i, nc - 1 - i):
            h_ref[0, rows(c), :] = hr
            h_ref[1, rows(c), :] = hi
        return advance(i, hr, hi)

    def second_touch(i, carry):
        hr, hi = carry
        j = nc - 1 - i
        h_ref[0, rows(i), :] = jnp.where(fwd, hr, h_ref[0, rows(i), :])
        h_ref[1, rows(i), :] = jnp.where(fwd, hi, h_ref[1, rows(i), :])
        h_ref[0, rows(j), :] = jnp.where(fwd, h_ref[0, rows(j), :], hr)
        h_ref[1, rows(j), :] = jnp.where(fwd, h_ref[1, rows(j), :], hi)
        return advance(i, hr, hi)

    zero = jnp.zeros((lg, half), F32)
    carry = lax.fori_loop(0, nc // 2, first_touch, (zero, zero))
    lax.fori_loop(nc // 2, nc, second_touch, carry)


def _s5_states(u, w_state, a_re, a_im):
    seq, width = u.shape
    g = width // S5_GROUP
    nc = seq // S5_CHUNK
    lg = S5_LANE_GROUPS
    half = S5_STATE_WIDTH // 2
    return pl.pallas_call(
        _s5_state_kernel,
        grid=(width // LANES,),
        in_specs=[pl.BlockSpec((seq, LANES), lambda j: (0, j)),
                  pl.BlockSpec((lg, S5_CHUNK_WIDTH, S5_STATE_WIDTH), lambda j: (j, 0, 0)),
                  pl.BlockSpec((lg, half), lambda j: (j, 0)),
                  pl.BlockSpec((lg, half), lambda j: (j, 0))],
        out_specs=[pl.BlockSpec((lg, nc, S5_CHUNK_WIDTH), lambda j: (j, 0, 0)),
                   pl.BlockSpec((None, 2, nc * lg, half), lambda j: (j, 0, 0, 0))],
        out_shape=[jax.ShapeDtypeStruct((g, nc, S5_CHUNK_WIDTH), BF16),
                   jax.ShapeDtypeStruct((g // lg, 2, nc * lg, half), F32)],
        scratch_shapes=[pltpu.VMEM((2, nc * lg, half), F32)],
        compiler_params=_params(("parallel",)),
        name="s5_states",
    )(u, w_state, a_re, a_im)


def _s5_out_kernel(ug_ref, h_ref, lag_ref, m_ref, o_ref, y_ref, toe_ref):
    cw = S5_CHUNK_WIDTH
    nch = ug_ref.shape[1]
    for g in range(S5_LANE_GROUPS):
        lag = lag_ref[g]
        for s in range(S5_CHUNK):
            first = S5_GROUP * (S5_CHUNK - 1 - s)
            toe_ref[g, s * S5_GROUP:(s + 1) * S5_GROUP, :] = lag[:, first:first + cw].astype(BF16)
    for g in range(S5_LANE_GROUPS):
        hg = jnp.concatenate([h_ref[part, pl.ds(g, nch, stride=S5_LANE_GROUPS), :] for part in range(2)],
                             axis=1).astype(BF16)
        y = jnp.dot(ug_ref[g], toe_ref[g], preferred_element_type=F32)
        y = y + jnp.dot(hg, m_ref[g], preferred_element_type=F32)
        y_ref[g] = jax.nn.gelu(y, approximate=True)
    grp = _lane_group_ids()

    def regroup(rb, carry):
        r0 = pl.multiple_of(rb * RELAYOUT_ROWS, RELAYOUT_ROWS)
        for half in range(cw // LANES):
            groups = [y_ref[g, pl.ds(r0, RELAYOUT_ROWS), half * LANES:(half + 1) * LANES]
                      for g in range(S5_LANE_GROUPS)]
            for tp, v in enumerate(_transpose_lane_blocks(groups, grp)):
                t = half * S5_LANE_GROUPS + tp
                o_ref[pl.ds(r0 * S5_CHUNK + t, RELAYOUT_ROWS, stride=S5_CHUNK), :] = v
        return carry

    lax.fori_loop(0, nch // RELAYOUT_ROWS, regroup, 0, unroll=2)


def _s5_outputs(ug, h3, lags, m_state, nsplit):
    g, nc, cw = ug.shape
    nch = nc // nsplit
    lg = S5_LANE_GROUPS
    return pl.pallas_call(
        _s5_out_kernel,
        grid=(g // lg, nsplit),
        in_specs=[pl.BlockSpec((lg, nch, cw), lambda j, h: (j, h, 0)),
                  pl.BlockSpec((None, 2, nch * lg, S5_STATE_WIDTH // 2), lambda j, h: (j, 0, h, 0)),
                  pl.BlockSpec((lg,) + lags.shape[1:], lambda j, h: (j, 0, 0)),
                  pl.BlockSpec((lg, S5_STATE_WIDTH, cw), lambda j, h: (j, 0, 0))],
        out_specs=pl.BlockSpec((nch * S5_CHUNK, LANES), lambda j, h: (h, j)),
        out_shape=jax.ShapeDtypeStruct((nc * S5_CHUNK, g * S5_GROUP), F32),
        scratch_shapes=[pltpu.VMEM((lg, nch, cw), F32), pltpu.VMEM((lg, cw, cw), BF16)],
        compiler_params=_params(("parallel", "parallel")),
        name="s5_outputs",
    )(ug, h3, lags, m_state)


def _merge_kernel(a1_ref, w1_ref, ys_ref, wg_ref, bg_ref, w2_ref, g1_ref, g2_ref, o_ref, ysb_ref, z_ref, *, tg):
    @pl.when(pl.program_id(1) == 0)
    def _():
        ysb_ref[...] = ys_ref[...].astype(BF16)
        for c in range(z_ref.shape[1] // tg):
            cols = slice(c * tg, (c + 1) * tg)
            t = jnp.dot(ysb_ref[...], wg_ref[:, cols], preferred_element_type=F32) + bg_ref[:, cols]
            z_ref[:, cols] = (ys_ref[:, cols] * jax.nn.sigmoid(t)).astype(BF16)

    y1 = jnp.dot(a1_ref[...], w1_ref[...], preferred_element_type=F32)
    y2 = jnp.dot(z_ref[...], w2_ref[...], preferred_element_type=F32)
    g1 = jax.nn.sigmoid(g1_ref[...].astype(F32))
    g2 = jax.nn.sigmoid(g2_ref[...].astype(F32))
    o_ref[...] = (g1 * y1 + g2 * y2).astype(o_ref.dtype)


def _glu_merge(att, w_na, ys, w_glu, b_glu, w_s5, gates, tm, tn):
    m, k1 = att.shape
    k2 = ys.shape[1]
    n = w_na.shape[1]
    g1 = 0
    g2 = n // tn
    return pl.pallas_call(
        functools.partial(_merge_kernel, tg=tn),
        grid=(m // tm, n // tn),
        in_specs=[pl.BlockSpec((tm, k1), lambda i, j: (i, 0)),
                  pl.BlockSpec((k1, tn), lambda i, j: (0, j)),
                  pl.BlockSpec((tm, k2), lambda i, j: (i, 0)),
                  pl.BlockSpec((k2, k2), lambda i, j: (0, 0)),
                  pl.BlockSpec((1, k2), lambda i, j: (0, 0)),
                  pl.BlockSpec((k2, tn), lambda i, j: (0, j)),
                  pl.BlockSpec((tm, tn), lambda i, j: (i, g1 + j)),
                  pl.BlockSpec((tm, tn), lambda i, j: (i, g2 + j))],
        out_specs=pl.BlockSpec((tm, tn), lambda i, j: (i, j)),
        out_shape=jax.ShapeDtypeStruct((m, n), BF16),
        scratch_shapes=[pltpu.VMEM((tm, k2), BF16), pltpu.VMEM((tm, k2), BF16)],
        compiler_params=_params(("parallel", "arbitrary")),
        name="glu_gated_merge",
    )(att, w_na, ys, w_glu, b_glu, w_s5, gates, gates)


def _matmul_residual_ln(a_ref, w_ref, res_ref, g_ref, b_ref, o_ref, alpha, ob_ref=None):
    sub = a_ref.shape[0] // LN_ROW_SPLIT
    for r in range(LN_ROW_SPLIT):
        rows = slice(r * sub, (r + 1) * sub)
        y = jnp.dot(a_ref[rows, :], w_ref[...], preferred_element_type=F32)
        out = _layer_norm_rows(alpha * res_ref[rows, :] + y, g_ref[...], b_ref[...])
        o_ref[rows, :] = out
        if ob_ref is not None:
            ob_ref[rows, :] = out.astype(ob_ref.dtype)


def _outproj_ln_kernel(a_ref, w_ref, x_ref, g_ref, b_ref, h_ref, hb_ref, *, alpha):
    _matmul_residual_ln(a_ref, w_ref, x_ref, g_ref, b_ref, h_ref, alpha, hb_ref)


def _outproj_ln(a, w, x, g, b, alpha, tm):
    m, k = a.shape
    n = w.shape[1]
    return pl.pallas_call(
        functools.partial(_outproj_ln_kernel, alpha=alpha),
        grid=(m // tm,),
        in_specs=[pl.BlockSpec((tm, k), lambda i: (i, 0)),
                  pl.BlockSpec((k, n), lambda i: (0, 0)),
                  pl.BlockSpec((tm, n), lambda i: (i, 0)),
                  pl.BlockSpec((1, n), lambda i: (0, 0)),
                  pl.BlockSpec((1, n), lambda i: (0, 0))],
        out_specs=[pl.BlockSpec((tm, n), lambda i: (i, 0)),
                   pl.BlockSpec((tm, n), lambda i: (i, 0))],
        out_shape=[jax.ShapeDtypeStruct((m, n), F32), jax.ShapeDtypeStruct((m, n), BF16)],
        compiler_params=_params(("parallel",)),
        name="outproj_ln1",
    )(a, w, x, g, b)


def _ffn_act_kernel(hb_ref, wg_ref, wu_ref, o_ref):
    hb = hb_ref[...]
    gate = jnp.dot(hb, wg_ref[...], preferred_element_type=F32)
    up = jnp.dot(hb, wu_ref[...], preferred_element_type=F32)
    o_ref[...] = (jax.nn.silu(gate) * up).astype(o_ref.dtype)


def _ffn_act(hb, wg, wu, tm, tf):
    m, d = hb.shape
    f = wg.shape[1]
    return pl.pallas_call(
        _ffn_act_kernel,
        grid=(m // tm, f // tf),
        in_specs=[pl.BlockSpec((tm, d), lambda i, j: (i, 0)),
                  pl.BlockSpec((d, tf), lambda i, j: (0, j)),
                  pl.BlockSpec((d, tf), lambda i, j: (0, j))],
        out_specs=pl.BlockSpec((tm, tf), lambda i, j: (i, j)),
        out_shape=jax.ShapeDtypeStruct((m, f), BF16),
        compiler_params=_params(("parallel", "arbitrary")),
        name="swiglu_act",
    )(hb, wg, wu)


def _ffn_down_ln_kernel(a_ref, w_ref, h_ref, g_ref, b_ref, o_ref, *, alpha):
    _matmul_residual_ln(a_ref, w_ref, h_ref, g_ref, b_ref, o_ref, alpha)


def _ffn_down_ln(act, wd, h, g, b, alpha, tm):
    m, f = act.shape
    d = wd.shape[1]
    return pl.pallas_call(
        functools.partial(_ffn_down_ln_kernel, alpha=alpha),
        grid=(m // tm,),
        in_specs=[pl.BlockSpec((tm, f), lambda i: (i, 0)),
                  pl.BlockSpec((f, d), lambda i: (0, 0), pipeline_mode=pl.Buffered(1)),
                  pl.BlockSpec((tm, d), lambda i: (i, 0)),
                  pl.BlockSpec((1, d), lambda i: (0, 0)),
                  pl.BlockSpec((1, d), lambda i: (0, 0))],
        out_specs=pl.BlockSpec((tm, d), lambda i: (i, 0)),
        out_shape=jax.ShapeDtypeStruct((m, d), F32),
        compiler_params=_params(("parallel",)),
        name="swiglu_down_ln2",
    )(act, wd, h, g, b)


def _layer(x2, w_in, b_gate, na_rpb, w_na_out, s5_a_re, s5_a_im, s5_log_dt, s5_b_re, s5_b_im,
           s5_c_re, s5_c_im, s5_d, w_glu, b_glu, w_s5_out, w_out, ln1_g, ln1_b,
           w_ffn_gate, w_ffn_up, w_ffn_down, ln2_g, ln2_b, alpha):
    seq, d_model = x2.shape
    rows = seq // GRID_W
    nc = seq // S5_CHUNK
    gate_col0 = 3 * NA_WIDTH + S5_WIDTH
    in_cols = gate_col0 + 2 * d_model
    row = lambda v: v.astype(F32).reshape(1, -1)

    col_scale = jnp.where(jnp.arange(in_cols) < NA_WIDTH, NA_HEAD_DIM ** -0.5, 1.0).astype(F32).reshape(1, -1)
    col_bias = jnp.concatenate([jnp.zeros((gate_col0,), F32), b_gate.astype(F32)]).reshape(1, -1)
    qkv, u, gates = _inproj(x2, w_in.astype(BF16), col_scale, col_bias, 3 * NA_WIDTH, gate_col0, tm=1024, tn=1024)

    later_weights = [w.astype(F32) for w in (w_na_out, w_glu, w_s5_out, w_out, w_ffn_gate, w_ffn_up, w_ffn_down)]
    att, (w_na_b, w_glu_b, w_s5_b, w_out_b, w_gate_b, w_up_b, w_down_b) = _attention(
        qkv, _attention_bias(na_rpb, rows), seq, later_weights)

    w_state, lags, m_state, a_re, a_im = _s5_tables(s5_a_re, s5_a_im, s5_log_dt, s5_b_re, s5_b_im,
                                                    s5_c_re, s5_c_im, s5_d)
    ug, h_states = _s5_states(u, w_state, a_re, a_im)
    ys = _s5_outputs(ug, h_states, lags, m_state, nsplit=2)

    merged = _glu_merge(att, w_na_b, ys, w_glu_b, row(b_glu), w_s5_b, gates, tm=1024, tn=1024)
    h, hb = _outproj_ln(merged, w_out_b, x2, row(ln1_g), row(ln1_b), alpha, tm=512)
    act = _ffn_act(hb, w_gate_b, w_up_b, tm=1024, tf=512)
    return _ffn_down_ln(act, w_down_b, h, row(ln2_g), row(ln2_b), alpha, tm=512)


def kernel(x, w_in, b_gate, na_rpb, w_na_out, s5_a_re, s5_a_im, s5_log_dt, s5_b_re, s5_b_im, s5_c_re, s5_c_im, s5_d, w_glu, b_glu, w_s5_out, w_out, ln1_g, ln1_b, w_ffn_gate, w_ffn_up, w_ffn_down, ln2_g, ln2_b):
    bsz, seq, d_model = x.shape
    depth = w_in.shape[0]
    alpha = (2.0 * depth) ** 0.25
    outs = []
    for bi in range(bsz):
        xb = x[bi]
        for l in range(depth):
            xb = _layer(xb, w_in[l], b_gate[l], na_rpb[l], w_na_out[l], s5_a_re[l], s5_a_im[l], s5_log_dt[l],
                        s5_b_re[l], s5_b_im[l], s5_c_re[l], s5_c_im[l], s5_d[l], w_glu[l], b_glu[l],
                        w_s5_out[l], w_out[l], ln1_g[l], ln1_b[l], w_ffn_gate[l], w_ffn_up[l],
                        w_ffn_down[l], ln2_g[l], ln2_b[l], alpha)
        outs.append(xb)
    return jnp.stack(outs)
```

```python
import functools

import jax
import jax.numpy as jnp
from jax import lax
from jax.experimental import pallas as pl
from jax.experimental.pallas import tpu as pltpu

F32 = jnp.float32
BF16 = jnp.bfloat16

GRID_W = 64
NA_HEADS = 8
NA_HEAD_DIM = 128
NA_WIDTH = NA_HEADS * NA_HEAD_DIM
NA_ROWS = 8
NA_COLS = 16
S5_GROUP = 16
S5_GROUPS = 64
S5_WIDTH = S5_GROUP * S5_GROUPS
S5_STATE = 64
LN_EPS = 1e-5
MASK_VALUE = -1e30

S5_CHUNK = 16
S5_CHUNK_WIDTH = S5_CHUNK * S5_GROUP
S5_STATE_WIDTH = 4 * S5_STATE

ATT_Q_ROWS = 8
ATT_K_ROWS = 16
ATT_Q = ATT_Q_ROWS * GRID_W
ATT_K = ATT_K_ROWS * GRID_W
ATT_KBLK = 256
LANES = 128
ATT_WIN = NA_ROWS * GRID_W + LANES
ATT_HEADS_PER_STEP = 4
ATT_ROW_SPLIT = 2
BF16_SUBLANES = 16
LN_ROW_SPLIT = 4
S5_LANE_GROUPS = LANES // S5_GROUP
RELAYOUT_ROWS = 16

VMEM_LIMIT = 56 * 1024 * 1024


def _params(sem):
    return pltpu.CompilerParams(dimension_semantics=sem, vmem_limit_bytes=VMEM_LIMIT)


def _layer_norm_rows(y, g, b):
    mu = jnp.mean(y, axis=-1, keepdims=True)
    d = y - mu
    var = jnp.mean(d * d, axis=-1, keepdims=True)
    return d * lax.rsqrt(var + LN_EPS) * g + b


def _inproj_kernel(x_ref, w_ref, s_ref, b_ref, qkv_ref, u_ref, g_ref, xb_ref, *, n_blocks, u_tile0, gate_tile0):
    r = pl.program_id(0)
    j = pl.program_id(1)
    slab = x_ref.shape[0]

    @pl.when(r < n_blocks)
    def _():
        xb_ref[r % 2, pl.ds(pl.multiple_of(j * slab, slab), slab), :] = x_ref[...].astype(BF16)

    cur = 1 - r % 2
    busy = r > 0

    @pl.when(busy & (j < u_tile0))
    def _():
        acc = (jnp.dot(xb_ref[cur], w_ref[...], preferred_element_type=F32) * s_ref[...]).astype(qkv_ref.dtype)
        for h in range(qkv_ref.shape[0]):
            qkv_ref[h] = acc[:, h * NA_HEAD_DIM:(h + 1) * NA_HEAD_DIM]

    @pl.when(busy & (j >= u_tile0) & (j < gate_tile0))
    def _():
        u_ref[...] = jnp.dot(xb_ref[cur], w_ref[...], preferred_element_type=F32)

    @pl.when(busy & (j >= gate_tile0))
    def _():
        acc = jnp.dot(xb_ref[cur], w_ref[...], preferred_element_type=F32)
        g_ref[...] = (acc + b_ref[...]).astype(g_ref.dtype)


def _inproj(x, w, col_scale, col_bias, u_col0, gate_col0, tm, tn):
    m, k = x.shape
    n = w.shape[1]
    n_blocks, n_tiles = m // tm, n // tn
    u_tile0, gate_tile0 = u_col0 // tn, gate_col0 // tn
    heads_per_tile = tn // NA_HEAD_DIM
    last_u = gate_tile0 - u_tile0 - 1
    last_g = n_tiles - gate_tile0 - 1
    slab = tm // n_tiles
    blk = lambda r: jnp.maximum(r - 1, 0)
    live = lambda r, t: jnp.where(r == 0, 0, t)
    return pl.pallas_call(
        functools.partial(_inproj_kernel, n_blocks=n_blocks, u_tile0=u_tile0, gate_tile0=gate_tile0),
        grid=(n_blocks + 1, n_tiles),
        in_specs=[
            pl.BlockSpec((slab, k), lambda r, j: (jnp.minimum(r * n_tiles + j, n_blocks * n_tiles - 1), 0)),
            pl.BlockSpec((k, tn), lambda r, j: (0, live(r, j))),
            pl.BlockSpec((1, tn), lambda r, j: (0, j)),
            pl.BlockSpec((1, tn), lambda r, j: (0, j)),
        ],
        out_specs=[pl.BlockSpec((heads_per_tile, tm, NA_HEAD_DIM),
                                lambda r, j: (live(r, jnp.minimum(j, u_tile0 - 1)), blk(r), 0)),
                   pl.BlockSpec((tm, tn), lambda r, j: (blk(r), live(r, jnp.clip(j - u_tile0, 0, last_u)))),
                   pl.BlockSpec((tm, tn), lambda r, j: (blk(r), live(r, jnp.clip(j - gate_tile0, 0, last_g))))],
        out_shape=[jax.ShapeDtypeStruct((u_col0 // NA_HEAD_DIM, m, NA_HEAD_DIM), BF16),
                   jax.ShapeDtypeStruct((m, gate_col0 - u_col0), F32),
                   jax.ShapeDtypeStruct((m, n - gate_col0), BF16)],
        scratch_shapes=[pltpu.VMEM((2, tm, k), BF16)],
        compiler_params=_params(("arbitrary", "arbitrary")),
        name="inproj",
    )(x, w, col_scale, col_bias)


def _attn_windows(rows):
    nb = rows // ATT_Q_ROWS
    kinds = []
    for b in (0, 1, nb - 1):
        base = min(max(b * ATT_Q_ROWS - NA_ROWS // 2, 0), rows - ATT_K_ROWS)
        geo = []
        for ql in range(ATT_Q_ROWS):
            qr = b * ATT_Q_ROWS + ql
            r0 = min(max(qr - NA_ROWS // 2, 0), rows - NA_ROWS)
            koff = r0 - base
            lane0 = min(LANES * (koff * GRID_W // LANES), ATT_K - ATT_WIN)
            geo.append((lane0, koff * GRID_W - lane0, r0 - qr + NA_ROWS - 1))
        kinds.append(tuple(geo))
    return tuple(kinds)


def _attn_kernel(q_ref, k_ref, v_ref, bias_ref, *rest, windows, nb, n_cast):
    cast_src, o_ref, cast_dst, p_ref = rest[:n_cast], rest[n_cast], rest[n_cast + 1:2 * n_cast + 1], rest[-1]
    b = pl.program_id(1)

    def one_head(hh, geo):
        c = slice(hh * NA_HEAD_DIM, (hh + 1) * NA_HEAD_DIM)
        per = ATT_Q_ROWS // ATT_ROW_SPLIT
        for part in range(ATT_ROW_SPLIT):
            sub = geo[part * per:(part + 1) * per]
            cover = [(lane0 + phase - phase % LANES, phase - phase % LANES,
                      NA_ROWS * GRID_W if phase % LANES == 0 else ATT_WIN) for lane0, phase, _ in sub]
            lo = min(start for start, _, _ in cover)
            hi = max(start + width for start, _, width in cover)
            rows = slice(part * per * GRID_W, (part + 1) * per * GRID_W)
            s = lax.dot_general(q_ref[hh, rows, :], k_ref[hh, lo:hi, :], (((1,), (1,)), ((), ())),
                                preferred_element_type=F32)
            inv = []
            for i, (start, skip, width) in enumerate(cover):
                ql = part * per + i
                r = slice(ql * GRID_W, (ql + 1) * GRID_W)
                sw = s[i * GRID_W:(i + 1) * GRID_W, start - lo:start - lo + width] + bias_ref[hh, ql, :, skip:skip + width]
                m = jnp.max(sw, axis=-1, keepdims=True)
                p = jnp.exp(sw - m)
                inv.append(1.0 / jnp.sum(p, axis=-1, keepdims=True))
                if start > lo:
                    p_ref[hh, r, lo:start] = jnp.zeros((GRID_W, start - lo), BF16)
                p_ref[hh, r, start:start + width] = p.astype(BF16)
                if start + width < hi:
                    p_ref[hh, r, start + width:hi] = jnp.zeros((GRID_W, hi - start - width), BF16)
            o = jnp.dot(p_ref[hh, rows, lo:hi], v_ref[hh, lo:hi, :], preferred_element_type=F32)
            o_ref[rows, c] = (o * jnp.concatenate(inv, axis=0)).astype(o_ref.dtype)

    def block_kind(geo):
        for src, dst in zip(cast_src, cast_dst):
            dst[...] = src[...].astype(dst.dtype)
        for hh in range(ATT_HEADS_PER_STEP):
            one_head(hh, geo)

    @pl.when(b == 0)
    def _():
        block_kind(windows[0])

    @pl.when((b > 0) & (b < nb - 1))
    def _():
        block_kind(windows[1])

    @pl.when(b == nb - 1)
    def _():
        block_kind(windows[2])


def _attention(qkv, bias, seq, weights):
    nb = seq // ATT_Q
    n_kblk = seq // ATT_KBLK
    per_blk = ATT_K // ATT_KBLK

    def kstart(b):
        return jnp.clip(2 * b - 1, 0, n_kblk - per_blk)

    def kind(b):
        return jnp.where(b == 0, 0, jnp.where(b == nb - 1, 2, 1))

    hps = ATT_HEADS_PER_STEP
    width = hps * NA_HEAD_DIM
    n_groups = NA_HEADS // hps

    def kv_spec(group0):
        return pl.BlockSpec((pl.Element(hps), pl.Element(ATT_K), pl.Element(NA_HEAD_DIM)),
                            lambda h, b: ((group0 + h) * hps, kstart(b) * ATT_KBLK, 0))

    in_specs = [pl.BlockSpec((hps, ATT_Q, NA_HEAD_DIM), lambda h, b: (h, b, 0))]
    in_specs += [kv_spec(n_groups), kv_spec(2 * n_groups)]
    in_specs += [pl.BlockSpec((None, hps, ATT_Q_ROWS, GRID_W, ATT_WIN), lambda h, b: (kind(b), h, 0, 0, 0))]

    n_steps = n_groups * nb

    def slab_spec(w):
        rows_w, cols_w = w.shape
        for rb in range(n_steps, 0, -1):
            cb = n_steps // rb
            if (rb * cb == n_steps and rows_w % rb == 0 and cols_w % cb == 0
                    and (rows_w // rb) % BF16_SUBLANES == 0 and (cols_w // cb) % LANES == 0):
                return pl.BlockSpec((rows_w // rb, cols_w // cb),
                                    lambda h, b, cb=cb: ((h * nb + b) // cb, (h * nb + b) % cb))
        raise ValueError(f"no slab tiling for {w.shape} over {n_steps} steps")

    slab_specs = [slab_spec(w) for w in weights]
    outs = pl.pallas_call(
        functools.partial(_attn_kernel, windows=_attn_windows(seq // GRID_W), nb=nb, n_cast=len(weights)),
        grid=(n_groups, nb),
        in_specs=in_specs + slab_specs,
        out_specs=[pl.BlockSpec((ATT_Q, width), lambda h, b: (b, h))] + slab_specs,
        out_shape=[jax.ShapeDtypeStruct((seq, NA_WIDTH), BF16)]
                  + [jax.ShapeDtypeStruct(w.shape, BF16) for w in weights],
        scratch_shapes=[pltpu.VMEM((hps, ATT_Q, ATT_K), BF16)],
        compiler_params=_params(("parallel", "arbitrary")),
        name="na_attention",
    )(qkv, qkv, qkv, bias, *weights)
    return outs[0], outs[1:]


def _attention_bias(rpb, rows):
    cols = jnp.arange(GRID_W)
    col_start = jnp.clip(cols - NA_COLS // 2, 0, GRID_W - NA_COLS)
    col_ok = (cols[None, :] >= col_start[:, None]) & (cols[None, :] < col_start[:, None] + NA_COLS)
    col_idx = jnp.clip(cols[None, :] - cols[:, None], -(NA_COLS - 1), NA_COLS - 1) + (NA_COLS - 1)
    pick = (jnp.arange(2 * NA_COLS - 1)[:, None, None] == col_idx[None]).astype(F32)
    by_col = jnp.einsum("hrj,jqk->hqrk", rpb.astype(F32), pick, precision=lax.Precision.HIGHEST)
    by_col = jnp.where(col_ok[None, :, None, :], by_col, MASK_VALUE)
    kinds, made = [], {}
    for geo in _attn_windows(rows):
        strips = []
        for _, phase, rfirst in geo:
            if (phase, rfirst) not in made:
                strip = by_col[:, :, rfirst:rfirst + NA_ROWS, :].reshape(NA_HEADS, GRID_W, NA_ROWS * GRID_W)
                made[phase, rfirst] = jnp.pad(strip, ((0, 0), (0, 0), (phase, ATT_WIN - NA_ROWS * GRID_W - phase)),
                                              constant_values=MASK_VALUE)
            strips.append(made[phase, rfirst])
        kinds.append(jnp.stack(strips, axis=1))
    return jnp.stack(kinds)


def _s5_tables(a_re, a_im, log_dt, b_re, b_im, c_re, c_im, d):
    t_len = S5_CHUNK
    groups = a_re.shape[1]
    steps = jnp.arange(t_len, dtype=F32)
    flip = t_len - 1 - steps

    def cmul(xr, xi, yr, yi):
        return xr * yr - xi * yi, xr * yi + xi * yr

    def powers(zr, zi, n):
        mag = jnp.exp(zr * n)
        return mag * jnp.cos(zi * n), mag * jnp.sin(zi * n)

    ar, ai = a_re.astype(F32), a_im.astype(F32)
    dt = jnp.exp(log_dt.astype(F32))[..., None]
    zr, zi = ar * dt, ai * dt
    lr, li = powers(zr, zi, 1.0)
    lr = lr - 1.0
    den = ar * ar + ai * ai
    fr, fi = (lr * ar + li * ai) / den, (li * ar - lr * ai) / den
    swap = lambda v: jnp.swapaxes(v.astype(F32), 2, 3)
    bbr, bbi = cmul(fr[:, :, None, :], fi[:, :, None, :], swap(b_re), swap(b_im))
    crt, cit = swap(c_re), swap(c_im)
    n_w = jnp.stack([flip, steps])[:, None, :, None]
    wp_r, wp_i = powers(zr[:, :, None, :], zi[:, :, None, :], n_w)
    wr, wi = cmul(wp_r[:, :, :, None, :], wp_i[:, :, :, None, :], bbr[:, :, None], bbi[:, :, None])
    wr = wr.reshape(2, groups, S5_CHUNK_WIDTH, S5_STATE)
    wi = wi.reshape(2, groups, S5_CHUNK_WIDTH, S5_STATE)
    n_e = jnp.stack([steps + 1.0, flip + 1.0])[:, None, None, :]
    ep_r, ep_i = powers(zr[..., None], zi[..., None], n_e)
    er, ei = cmul(crt[:, :, :, None, :], cit[:, :, :, None, :], ep_r[..., None], ep_i[..., None])
    er = er.reshape(2, groups, S5_STATE, S5_CHUNK_WIDTH)
    ei = ei.reshape(2, groups, S5_STATE, S5_CHUNK_WIDTH)
    lp_r, lp_i = powers(zr[..., None], zi[..., None], steps)
    xr, xi = cmul(crt[:, :, :, None, :], cit[:, :, :, None, :], lp_r[..., None], lp_i[..., None])
    lhs = jnp.concatenate([bbr, -bbi], axis=-1)
    rhs = jnp.concatenate([xr, xi], axis=2).reshape(2, groups, 2 * S5_STATE, S5_CHUNK_WIDTH)
    lag_kernels = jnp.einsum("dgck,dgkn->dgcn", lhs, rhs, precision=lax.Precision.HIGHEST)
    step_r, step_i = powers(zr, zi, float(t_len))

    w_state = jnp.concatenate([wr[0], wr[1], wi[0], wi[1]], axis=-1)
    m_state = jnp.concatenate([er[0], er[1], -ei[0], -ei[1]], axis=1)
    kf, kb = lag_kernels[0], lag_kernels[1]
    kb_rev = kb[:, :, S5_GROUP:].reshape(groups, S5_GROUP, t_len - 1, S5_GROUP)[:, :, ::-1]
    lag0 = kf[:, :, :S5_GROUP] + kb[:, :, :S5_GROUP] + d.astype(F32)[:, None, :] * jnp.eye(S5_GROUP, dtype=F32)[None]
    lags = jnp.concatenate([kb_rev.reshape(groups, S5_GROUP, -1), lag0, kf[:, :, S5_GROUP:]], axis=-1)
    return (w_state.astype(BF16), lags, m_state.astype(BF16),
            jnp.concatenate([step_r[0], step_r[1]], axis=-1), jnp.concatenate([step_i[0], step_i[1]], axis=-1))


def _lane_group_ids():
    return lax.broadcasted_iota(jnp.int32, (RELAYOUT_ROWS, LANES), 1) // S5_GROUP


def _transpose_lane_blocks(pieces, grp):
    nblk = S5_LANE_GROUPS
    rolled = []
    for r in range(nblk):
        acc = None
        for b in range(nblk):
            acc = pieces[b] if acc is None else jnp.where(grp == (b - r) % nblk, pieces[b], acc)
        rolled.append(pltpu.roll(acc, S5_GROUP * r, 1) if r else acc)
    out = []
    for k in range(nblk):
        acc = None
        for b in range(nblk):
            z = rolled[(b - k) % nblk]
            acc = z if acc is None else jnp.where(grp == b, z, acc)
        out.append(acc)
    return out


def _s5_state_kernel(u_ref, w_ref, ar_ref, ai_ref, ug_ref, h_ref, s_ref):
    lg = S5_LANE_GROUPS
    nc = ug_ref.shape[1]
    grp = _lane_group_ids()

    def regroup(rb, carry):
        r0 = pl.multiple_of(rb * RELAYOUT_ROWS, RELAYOUT_ROWS)
        for half in range(S5_CHUNK_WIDTH // LANES):
            steps = [u_ref[pl.ds(r0 * S5_CHUNK + half * S5_LANE_GROUPS + tp, RELAYOUT_ROWS, stride=S5_CHUNK), :]
                     for tp in range(S5_LANE_GROUPS)]
            for g, v in enumerate(_transpose_lane_blocks(steps, grp)):
                ug_ref[g, pl.ds(r0, RELAYOUT_ROWS), half * LANES:(half + 1) * LANES] = v.astype(BF16)
        return carry

    lax.fori_loop(0, nc // RELAYOUT_ROWS, regroup, 0, unroll=2)

    half = S5_STATE_WIDTH // 2
    for g in range(lg):
        sg = jnp.dot(ug_ref[g], w_ref[g], preferred_element_type=F32)
        s_ref[0, pl.ds(g, nc, stride=lg), :] = sg[:, :half]
        s_ref[1, pl.ds(g, nc, stride=lg), :] = sg[:, half:]

    lane = lax.broadcasted_iota(jnp.int32, (lg, half), 1)
    fwd = lane < S5_STATE
    ar = ar_ref[...]
    ai = ai_ref[...]
    rows = lambda i: pl.ds(pl.multiple_of(i * lg, lg), lg)

    def advance(i, hr, hi):
        sr = jnp.where(fwd, s_ref[0, rows(i), :], s_ref[0, rows(nc - 1 - i), :])
        si = jnp.where(fwd, s_ref[1, rows(i), :], s_ref[1, rows(nc - 1 - i), :])
        return ar * hr - ai * hi + sr, ar * hi + ai * hr + si

    def first_touch(i, carry):
        hr, hi = carry
        for c in (i, nc - 1 - i):
            h_ref[0, rows(c), :] = hr
            h_ref[1, rows(c), :] = hi
        return advance(i, hr, hi)

    def second_touch(i, carry):
        hr, hi = carry
        j = nc - 1 - i
        h_ref[0, rows(i), :] = jnp.where(fwd, hr, h_ref[0, rows(i), :])
        h_ref[1, rows(i), :] = jnp.where(fwd, hi, h_ref[1, rows(i), :])
        h_ref[0, rows(j), :] = jnp.where(fwd, h_ref[0, rows(j), :], hr)
        h_ref[1, rows(j), :] = jnp.where(fwd, h_ref[1, rows(j), :], hi)
        return advance(i, hr, hi)

    zero = jnp.zeros((lg, half), F32)
    carry = lax.fori_loop(0, nc // 2, first_touch, (zero, zero))
    lax.fori_loop(nc // 2, nc, second_touch, carry)


def _s5_states(u, w_state, a_re, a_im):
    seq, width = u.shape
    g = width // S5_GROUP
    nc = seq // S5_CHUNK
    lg = S5_LANE_GROUPS
    half = S5_STATE_WIDTH // 2
    return pl.pallas_call(
        _s5_state_kernel,
        grid=(width // LANES,),
        in_specs=[pl.BlockSpec((seq, LANES), lambda j: (0, j)),
                  pl.BlockSpec((lg, S5_CHUNK_WIDTH, S5_STATE_WIDTH), lambda j: (j, 0, 0)),
                  pl.BlockSpec((lg, half), lambda j: (j, 0)),
                  pl.BlockSpec((lg, half), lambda j: (j, 0))],
        out_specs=[pl.BlockSpec((lg, nc, S5_CHUNK_WIDTH), lambda j: (j, 0, 0)),
                   pl.BlockSpec((None, 2, nc * lg, half), lambda j: (j, 0, 0, 0))],
        out_shape=[jax.ShapeDtypeStruct((g, nc, S5_CHUNK_WIDTH), BF16),
                   jax.ShapeDtypeStruct((g // lg, 2, nc * lg, half), F32)],
        scratch_shapes=[pltpu.VMEM((2, nc * lg, half), F32)],
        compiler_params=_params(("parallel",)),
        name="s5_states",
    )(u, w_state, a_re, a_im)


def _s5_out_kernel(ug_ref, h_ref, lag_ref, m_ref, o_ref, y_ref, toe_ref):
    cw = S5_CHUNK_WIDTH
    nch = ug_ref.shape[1]
    for g in range(S5_LANE_GROUPS):
        lag = lag_ref[g]
        for s in range(S5_CHUNK):
            first = S5_GROUP * (S5_CHUNK - 1 - s)
            toe_ref[g, s * S5_GROUP:(s + 1) * S5_GROUP, :] = lag[:, first:first + cw].astype(BF16)
    for g in range(S5_LANE_GROUPS):
        hg = jnp.concatenate([h_ref[part, pl.ds(g, nch, stride=S5_LANE_GROUPS), :] for part in range(2)],
                             axis=1).astype(BF16)
        y = jnp.dot(ug_ref[g], toe_ref[g], preferred_element_type=F32)
        y = y + jnp.dot(hg, m_ref[g], preferred_element_type=F32)
        y_ref[g] = jax.nn.gelu(y, approximate=True)
    grp = _lane_group_ids()

    def regroup(rb, carry):
        r0 = pl.multiple_of(rb * RELAYOUT_ROWS, RELAYOUT_ROWS)
        for half in range(cw // LANES):
            groups = [y_ref[g, pl.ds(r0, RELAYOUT_ROWS), half * LANES:(half + 1) * LANES]
                      for g in range(S5_LANE_GROUPS)]
            for tp, v in enumerate(_transpose_lane_blocks(groups, grp)):
                t = half * S5_LANE_GROUPS + tp
                o_ref[pl.ds(r0 * S5_CHUNK + t, RELAYOUT_ROWS, stride=S5_CHUNK), :] = v
        return carry

    lax.fori_loop(0, nch // RELAYOUT_ROWS, regroup, 0, unroll=2)


def _s5_outputs(ug, h3, lags, m_state, nsplit):
    g, nc, cw = ug.shape
    nch = nc // nsplit
    lg = S5_LANE_GROUPS
    return pl.pallas_call(
        _s5_out_kernel,
        grid=(g // lg, nsplit),
        in_specs=[pl.BlockSpec((lg, nch, cw), lambda j, h: (j, h, 0)),
                  pl.BlockSpec((None, 2, nch * lg, S5_STATE_WIDTH // 2), lambda j, h: (j, 0, h, 0)),
                  pl.BlockSpec((lg,) + lags.shape[1:], lambda j, h: (j, 0, 0)),
                  pl.BlockSpec((lg, S5_STATE_WIDTH, cw), lambda j, h: (j, 0, 0))],
        out_specs=pl.BlockSpec((nch * S5_CHUNK, LANES), lambda j, h: (h, j)),
        out_shape=jax.ShapeDtypeStruct((nc * S5_CHUNK, g * S5_GROUP), F32),
        scratch_shapes=[pltpu.VMEM((lg, nch, cw), F32), pltpu.VMEM((lg, cw, cw), BF16)],
        compiler_params=_params(("parallel", "parallel")),
        name="s5_outputs",
    )(ug, h3, lags, m_state)


def _merge_kernel(a1_ref, w1_ref, ys_ref, wg_ref, bg_ref, w2_ref, g1_ref, g2_ref, o_ref, ysb_ref, z_ref, *, tg):
    @pl.when(pl.program_id(1) == 0)
    def _():
        ysb_ref[...] = ys_ref[...].astype(BF16)
        for c in range(z_ref.shape[1] // tg):
            cols = slice(c * tg, (c + 1) * tg)
            t = jnp.dot(ysb_ref[...], wg_ref[:, cols], preferred_element_type=F32) + bg_ref[:, cols]
            z_ref[:, cols] = (ys_ref[:, cols] * jax.nn.sigmoid(t)).astype(BF16)

    y1 = jnp.dot(a1_ref[...], w1_ref[...], preferred_element_type=F32)
    y2 = jnp.dot(z_ref[...], w2_ref[...], preferred_element_type=F32)
    g1 = jax.nn.sigmoid(g1_ref[...].astype(F32))
    g2 = jax.nn.sigmoid(g2_ref[...].astype(F32))
    o_ref[...] = (g1 * y1 + g2 * y2).astype(o_ref.dtype)


def _glu_merge(att, w_na, ys, w_glu, b_glu, w_s5, gates, tm, tn):
    m, k1 = att.shape
    k2 = ys.shape[1]
    n = w_na.shape[1]
    g1 = 0
    g2 = n // tn
    return pl.pallas_call(
        functools.partial(_merge_kernel, tg=tn),
        grid=(m // tm, n // tn),
        in_specs=[pl.BlockSpec((tm, k1), lambda i, j: (i, 0)),
                  pl.BlockSpec((k1, tn), lambda i, j: (0, j)),
                  pl.BlockSpec((tm, k2), lambda i, j: (i, 0)),
                  pl.BlockSpec((k2, k2), lambda i, j: (0, 0)),
                  pl.BlockSpec((1, k2), lambda i, j: (0, 0)),
                  pl.BlockSpec((k2, tn), lambda i, j: (0, j)),
                  pl.BlockSpec((tm, tn), lambda i, j: (i, g1 + j)),
                  pl.BlockSpec((tm, tn), lambda i, j: (i, g2 + j))],
        out_specs=pl.BlockSpec((tm, tn), lambda i, j: (i, j)),
        out_shape=jax.ShapeDtypeStruct((m, n), BF16),
        scratch_shapes=[pltpu.VMEM((tm, k2), BF16), pltpu.VMEM((tm, k2), BF16)],
        compiler_params=_params(("parallel", "arbitrary")),
        name="glu_gated_merge",
    )(att, w_na, ys, w_glu, b_glu, w_s5, gates, gates)


def _matmul_residual_ln(a_ref, w_ref, res_ref, g_ref, b_ref, o_ref, alpha, ob_ref=None):
    sub = a_ref.shape[0] // LN_ROW_SPLIT
    for r in range(LN_ROW_SPLIT):
        rows = slice(r * sub, (r + 1) * sub)
        y = jnp.dot(a_ref[rows, :], w_ref[...], preferred_element_type=F32)
        out = _layer_norm_rows(alpha * res_ref[rows, :] + y, g_ref[...], b_ref[...])
        o_ref[rows, :] = out
        if ob_ref is not None:
            ob_ref[rows, :] = out.astype(ob_ref.dtype)


def _outproj_ln_kernel(a_ref, w_ref, x_ref, g_ref, b_ref, h_ref, hb_ref, *, alpha):
    _matmul_residual_ln(a_ref, w_ref, x_ref, g_ref, b_ref, h_ref, alpha, hb_ref)


def _outproj_ln(a, w, x, g, b, alpha, tm):
    m, k = a.shape
    n = w.shape[1]
    return pl.pallas_call(
        functools.partial(_outproj_ln_kernel, alpha=alpha),
        grid=(m // tm,),
        in_specs=[pl.BlockSpec((tm, k), lambda i: (i, 0)),
                  pl.BlockSpec((k, n), lambda i: (0, 0)),
                  pl.BlockSpec((tm, n), lambda i: (i, 0)),
                  pl.BlockSpec((1, n), lambda i: (0, 0)),
                  pl.BlockSpec((1, n), lambda i: (0, 0))],
        out_specs=[pl.BlockSpec((tm, n), lambda i: (i, 0)),
                   pl.BlockSpec((tm, n), lambda i: (i, 0))],
        out_shape=[jax.ShapeDtypeStruct((m, n), F32), jax.ShapeDtypeStruct((m, n), BF16)],
        compiler_params=_params(("parallel",)),
        name="outproj_ln1",
    )(a, w, x, g, b)


def _ffn_act_kernel(hb_ref, wg_ref, wu_ref, o_ref):
    hb = hb_ref[...]
    gate = jnp.dot(hb, wg_ref[...], preferred_element_type=F32)
    up = jnp.dot(hb, wu_ref[...], preferred_element_type=F32)
    o_ref[...] = (jax.nn.silu(gate) * up).astype(o_ref.dtype)


def _ffn_act(hb, wg, wu, tm, tf):
    m, d = hb.shape
    f = wg.shape[1]
    return pl.pallas_call(
        _ffn_act_kernel,
        grid=(m // tm, f // tf),
        in_specs=[pl.BlockSpec((tm, d), lambda i, j: (i, 0)),
                  pl.BlockSpec((d, tf), lambda i, j: (0, j)),
                  pl.BlockSpec((d, tf), lambda i, j: (0, j))],
        out_specs=pl.BlockSpec((tm, tf), lambda i, j: (i, j)),
        out_shape=jax.ShapeDtypeStruct((m, f), BF16),
        compiler_params=_params(("parallel", "arbitrary")),
        name="swiglu_act",
    )(hb, wg, wu)


def _ffn_down_ln_kernel(a_ref, w_ref, h_ref, g_ref, b_ref, o_ref, *, alpha):
    _matmul_residual_ln(a_ref, w_ref, h_ref, g_ref, b_ref, o_ref, alpha)


def _ffn_down_ln(act, wd, h, g, b, alpha, tm):
    m, f = act.shape
    d = wd.shape[1]
    return pl.pallas_call(
        functools.partial(_ffn_down_ln_kernel, alpha=alpha),
        grid=(m // tm,),
        in_specs=[pl.BlockSpec((tm, f), lambda i: (i, 0)),
                  pl.BlockSpec((f, d), lambda i: (0, 0), pipeline_mode=pl.Buffered(1)),
                  pl.BlockSpec((tm, d), lambda i: (i, 0)),
                  pl.BlockSpec((1, d), lambda i: (0, 0)),
                  pl.BlockSpec((1, d), lambda i: (0, 0))],
        out_specs=pl.BlockSpec((tm, d), lambda i: (i, 0)),
        out_shape=jax.ShapeDtypeStruct((m, d), F32),
        compiler_params=_params(("parallel",)),
        name="swiglu_down_ln2",
    )(act, wd, h, g, b)


def _layer(x2, w_in, b_gate, na_rpb, w_na_out, s5_a_re, s5_a_im, s5_log_dt, s5_b_re, s5_b_im,
           s5_c_re, s5_c_im, s5_d, w_glu, b_glu, w_s5_out, w_out, ln1_g, ln1_b,
           w_ffn_gate, w_ffn_up, w_ffn_down, ln2_g, ln2_b, alpha):
    seq, d_model = x2.shape
    rows = seq // GRID_W
    nc = seq // S5_CHUNK
    gate_col0 = 3 * NA_WIDTH + S5_WIDTH
    in_cols = gate_col0 + 2 * d_model
    row = lambda v: v.astype(F32).reshape(1, -1)

    col_scale = jnp.where(jnp.arange(in_cols) < NA_WIDTH, NA_HEAD_DIM ** -0.5, 1.0).astype(F32).reshape(1, -1)
    col_bias = jnp.concatenate([jnp.zeros((gate_col0,), F32), b_gate.astype(F32)]).reshape(1, -1)
    qkv, u, gates = _inproj(x2, w_in.astype(BF16), col_scale, col_bias, 3 * NA_WIDTH, gate_col0, tm=1024, tn=1024)

    later_weights = [w.astype(F32) for w in (w_na_out, w_glu, w_s5_out, w_out, w_ffn_gate, w_ffn_up, w_ffn_down)]
    att, (w_na_b, w_glu_b, w_s5_b, w_out_b, w_gate_b, w_up_b, w_down_b) = _attention(
        qkv, _attention_bias(na_rpb, rows), seq, later_weights)

    w_state, lags, m_state, a_re, a_im = _s5_tables(s5_a_re, s5_a_im, s5_log_dt, s5_b_re, s5_b_im,
                                                    s5_c_re, s5_c_im, s5_d)
    ug, h_states = _s5_states(u, w_state, a_re, a_im)
    ys = _s5_outputs(ug, h_states, lags, m_state, nsplit=2)

    merged = _glu_merge(att, w_na_b, ys, w_glu_b, row(b_glu), w_s5_b, gates, tm=1024, tn=1024)
    h, hb = _outproj_ln(merged, w_out_b, x2, row(ln1_g), row(ln1_b), alpha, tm=512)
    act = _ffn_act(hb, w_gate_b, w_up_b, tm=1024, tf=512)
    return _ffn_down_ln(act, w_down_b, h, row(ln2_g), row(ln2_b), alpha, tm=512)


def kernel(x, w_in, b_gate, na_rpb, w_na_out, s5_a_re, s5_a_im, s5_log_dt, s5_b_re, s5_b_im, s5_c_re, s5_c_im, s5_d, w_glu, b_glu, w_s5_out, w_out, ln1_g, ln1_b, w_ffn_gate, w_ffn_up, w_ffn_down, ln2_g, ln2_b):
    bsz, seq, d_model = x.shape
    depth = w_in.shape[0]
    alpha = (2.0 * depth) ** 0.25
    outs = []
    for bi in range(bsz):
        xb = x[bi]
        for l in range(depth):
            xb = _layer(xb, w_in[l], b_gate[l], na_rpb[l], w_na_out[l], s5_a_re[l], s5_a_im[l], s5_log_dt[l],
                        s5_b_re[l], s5_b_im[l], s5_c_re[l], s5_c_im[l], s5_d[l], w_glu[l], b_glu[l],
                        w_s5_out[l], w_out[l], ln1_g[l], ln1_b[l], w_ffn_gate[l], w_ffn_up[l],
                        w_ffn_down[l], ln2_g[l], ln2_b[l], alpha)
        outs.append(xb)
    return jnp.stack(outs)
```

```python
import functools

import jax
import jax.numpy as jnp
from jax import lax
from jax.experimental import pallas as pl
from jax.experimental.pallas import tpu as pltpu

F32 = jnp.float32
BF16 = jnp.bfloat16

GRID_W = 64
NA_HEADS = 8
NA_HEAD_DIM = 128
NA_WIDTH = NA_HEADS * NA_HEAD_DIM
NA_ROWS = 8
NA_COLS = 16
S5_GROUP = 16
S5_GROUPS = 64
S5_WIDTH = S5_GROUP * S5_GROUPS
S5_STATE = 64
LN_EPS = 1e-5
MASK_VALUE = -1e30

S5_CHUNK = 16
S5_CHUNK_WIDTH = S5_CHUNK * S5_GROUP
S5_STATE_WIDTH = 4 * S5_STATE

ATT_Q_ROWS = 8
ATT_K_ROWS = 16
ATT_Q = ATT_Q_ROWS * GRID_W
ATT_K = ATT_K_ROWS * GRID_W
ATT_KBLK = 256
LANES = 128
ATT_WIN = NA_ROWS * GRID_W + LANES
ATT_HEADS_PER_STEP = 4
ATT_ROW_SPLIT = 2
BF16_SUBLANES = 16
LN_ROW_SPLIT = 4
S5_LANE_GROUPS = LANES // S5_GROUP
RELAYOUT_ROWS = 16

VMEM_LIMIT = 56 * 1024 * 1024


def _params(sem):
    return pltpu.CompilerParams(dimension_semantics=sem, vmem_limit_bytes=VMEM_LIMIT)


def _layer_norm_rows(y, g, b):
    mu = jnp.mean(y, axis=-1, keepdims=True)
    d = y - mu
    var = jnp.mean(d * d, axis=-1, keepdims=True)
    return d * lax.rsqrt(var + LN_EPS) * g + b


def _inproj_kernel(x_ref, w_ref, s_ref, b_ref, qkv_ref, u_ref, g_ref, xb_ref, *, n_blocks, u_tile0, gate_tile0):
    r = pl.program_id(0)
    j = pl.program_id(1)
    slab = x_ref.shape[0]

    @pl.when(r < n_blocks)
    def _():
        xb_ref[r % 2, pl.ds(pl.multiple_of(j * slab, slab), slab), :] = x_ref[...].astype(BF16)

    cur = 1 - r % 2
    busy = r > 0

    @pl.when(busy & (j < u_tile0))
    def _():
        acc = (jnp.dot(xb_ref[cur], w_ref[...], preferred_element_type=F32) * s_ref[...]).astype(qkv_ref.dtype)
        for h in range(qkv_ref.shape[0]):
            qkv_ref[h] = acc[:, h * NA_HEAD_DIM:(h + 1) * NA_HEAD_DIM]

    @pl.when(busy & (j >= u_tile0) & (j < gate_tile0))
    def _():
        u_ref[...] = jnp.dot(xb_ref[cur], w_ref[...], preferred_element_type=F32)

    @pl.when(busy & (j >= gate_tile0))
    def _():
        acc = jnp.dot(xb_ref[cur], w_ref[...], preferred_element_type=F32)
        g_ref[...] = (acc + b_ref[...]).astype(g_ref.dtype)


def _inproj(x, w, col_scale, col_bias, u_col0, gate_col0, tm, tn):
    m, k = x.shape
    n = w.shape[1]
    n_blocks, n_tiles = m // tm, n // tn
    u_tile0, gate_tile0 = u_col0 // tn, gate_col0 // tn
    heads_per_tile = tn // NA_HEAD_DIM
    last_u = gate_tile0 - u_tile0 - 1
    last_g = n_tiles - gate_tile0 - 1
    slab = tm // n_tiles
    blk = lambda r: jnp.maximum(r - 1, 0)
    live = lambda r, t: jnp.where(r == 0, 0, t)
    return pl.pallas_call(
        functools.partial(_inproj_kernel, n_blocks=n_blocks, u_tile0=u_tile0, gate_tile0=gate_tile0),
        grid=(n_blocks + 1, n_tiles),
        in_specs=[
            pl.BlockSpec((slab, k), lambda r, j: (jnp.minimum(r * n_tiles + j, n_blocks * n_tiles - 1), 0)),
            pl.BlockSpec((k, tn), lambda r, j: (0, live(r, j))),
            pl.BlockSpec((1, tn), lambda r, j: (0, j)),
            pl.BlockSpec((1, tn), lambda r, j: (0, j)),
        ],
        out_specs=[pl.BlockSpec((heads_per_tile, tm, NA_HEAD_DIM),
                                lambda r, j: (live(r, jnp.minimum(j, u_tile0 - 1)), blk(r), 0)),
                   pl.BlockSpec((tm, tn), lambda r, j: (blk(r), live(r, jnp.clip(j - u_tile0, 0, last_u)))),
                   pl.BlockSpec((tm, tn), lambda r, j: (blk(r), live(r, jnp.clip(j - gate_tile0, 0, last_g))))],
        out_shape=[jax.ShapeDtypeStruct((u_col0 // NA_HEAD_DIM, m, NA_HEAD_DIM), BF16),
                   jax.ShapeDtypeStruct((m, gate_col0 - u_col0), F32),
                   jax.ShapeDtypeStruct((m, n - gate_col0), BF16)],
        scratch_shapes=[pltpu.VMEM((2, tm, k), BF16)],
        compiler_params=_params(("arbitrary", "arbitrary")),
        name="inproj",
    )(x, w, col_scale, col_bias)


def _attn_windows(rows):
    nb = rows // ATT_Q_ROWS
    kinds = []
    for b in (0, 1, nb - 1):
        base = min(max(b * ATT_Q_ROWS - NA_ROWS // 2, 0), rows - ATT_K_ROWS)
        geo = []
        for ql in range(ATT_Q_ROWS):
            qr = b * ATT_Q_ROWS + ql
            r0 = min(max(qr - NA_ROWS // 2, 0), rows - NA_ROWS)
            koff = r0 - base
            lane0 = min(LANES * (koff * GRID_W // LANES), ATT_K - ATT_WIN)
            geo.append((lane0, koff * GRID_W - lane0, r0 - qr + NA_ROWS - 1))
        kinds.append(tuple(geo))
    return tuple(kinds)


def _attn_kernel(q_ref, k0, k1, k2, k3, v0, v1, v2, v3, bias_ref, *rest, windows, nb, n_cast):
    cast_src, o_ref, cast_dst, p_ref = rest[:n_cast], rest[n_cast], rest[n_cast + 1:2 * n_cast + 1], rest[-1]
    b = pl.program_id(1)
    k_refs, v_refs = (k0, k1, k2, k3), (v0, v1, v2, v3)

    def one_head(hh, geo):
        c = slice(hh * NA_HEAD_DIM, (hh + 1) * NA_HEAD_DIM)
        k = jnp.concatenate([r[hh] for r in k_refs], axis=0)
        v = jnp.concatenate([r[hh] for r in v_refs], axis=0)
        per = ATT_Q_ROWS // ATT_ROW_SPLIT
        for part in range(ATT_ROW_SPLIT):
            sub = geo[part * per:(part + 1) * per]
            cover = [(lane0 + phase - phase % LANES, phase - phase % LANES,
                      NA_ROWS * GRID_W if phase % LANES == 0 else ATT_WIN) for lane0, phase, _ in sub]
            lo = min(start for start, _, _ in cover)
            hi = max(start + width for start, _, width in cover)
            rows = slice(part * per * GRID_W, (part + 1) * per * GRID_W)
            s = lax.dot_general(q_ref[hh, rows, :], k[lo:hi], (((1,), (1,)), ((), ())),
                                preferred_element_type=F32)
            inv = []
            for i, (start, skip, width) in enumerate(cover):
                ql = part * per + i
                r = slice(ql * GRID_W, (ql + 1) * GRID_W)
                sw = s[i * GRID_W:(i + 1) * GRID_W, start - lo:start - lo + width] + bias_ref[hh, ql, :, skip:skip + width]
                m = jnp.max(sw, axis=-1, keepdims=True)
                p = jnp.exp(sw - m)
                inv.append(1.0 / jnp.sum(p, axis=-1, keepdims=True))
                if start > lo:
                    p_ref[hh, r, lo:start] = jnp.zeros((GRID_W, start - lo), BF16)
                p_ref[hh, r, start:start + width] = p.astype(BF16)
                if start + width < hi:
                    p_ref[hh, r, start + width:hi] = jnp.zeros((GRID_W, hi - start - width), BF16)
            o = jnp.dot(p_ref[hh, rows, lo:hi], v[lo:hi], preferred_element_type=F32)
            o_ref[rows, c] = (o * jnp.concatenate(inv, axis=0)).astype(o_ref.dtype)

    def block_kind(geo):
        for src, dst in zip(cast_src, cast_dst):
            dst[...] = src[...].astype(dst.dtype)
        for hh in range(ATT_HEADS_PER_STEP):
            one_head(hh, geo)

    @pl.when(b == 0)
    def _():
        block_kind(windows[0])

    @pl.when((b > 0) & (b < nb - 1))
    def _():
        block_kind(windows[1])

    @pl.when(b == nb - 1)
    def _():
        block_kind(windows[2])


def _attention(qkv, bias, seq, weights):
    nb = seq // ATT_Q
    n_kblk = seq // ATT_KBLK
    per_blk = ATT_K // ATT_KBLK

    def kstart(b):
        return jnp.clip(2 * b - 1, 0, n_kblk - per_blk)

    def kind(b):
        return jnp.where(b == 0, 0, jnp.where(b == nb - 1, 2, 1))

    hps = ATT_HEADS_PER_STEP
    width = hps * NA_HEAD_DIM
    n_groups = NA_HEADS // hps

    def kv_spec(group0, t):
        return pl.BlockSpec((hps, ATT_KBLK, NA_HEAD_DIM), lambda h, b: (group0 + h, kstart(b) + t, 0))

    in_specs = [pl.BlockSpec((hps, ATT_Q, NA_HEAD_DIM), lambda h, b: (h, b, 0))]
    in_specs += [kv_spec(n_groups, t) for t in range(per_blk)]
    in_specs += [kv_spec(2 * n_groups, t) for t in range(per_blk)]
    in_specs += [pl.BlockSpec((None, hps, ATT_Q_ROWS, GRID_W, ATT_WIN), lambda h, b: (kind(b), h, 0, 0, 0))]

    n_steps = n_groups * nb

    def slab_spec(w):
        rows_w, cols_w = w.shape
        for rb in range(n_steps, 0, -1):
            cb = n_steps // rb
            if (rb * cb == n_steps and rows_w % rb == 0 and cols_w % cb == 0
                    and (rows_w // rb) % BF16_SUBLANES == 0 and (cols_w // cb) % LANES == 0):
                return pl.BlockSpec((rows_w // rb, cols_w // cb),
                                    lambda h, b, cb=cb: ((h * nb + b) // cb, (h * nb + b) % cb))
        raise ValueError(f"no slab tiling for {w.shape} over {n_steps} steps")

    slab_specs = [slab_spec(w) for w in weights]
    outs = pl.pallas_call(
        functools.partial(_attn_kernel, windows=_attn_windows(seq // GRID_W), nb=nb, n_cast=len(weights)),
        grid=(n_groups, nb),
        in_specs=in_specs + slab_specs,
        out_specs=[pl.BlockSpec((ATT_Q, width), lambda h, b: (b, h))] + slab_specs,
        out_shape=[jax.ShapeDtypeStruct((seq, NA_WIDTH), BF16)]
                  + [jax.ShapeDtypeStruct(w.shape, BF16) for w in weights],
        scratch_shapes=[pltpu.VMEM((hps, ATT_Q, ATT_K), BF16)],
        compiler_params=_params(("parallel", "arbitrary")),
        name="na_attention",
    )(*([qkv] * (1 + 2 * per_blk)), bias, *weights)
    return outs[0], outs[1:]


def _attention_bias(rpb, rows):
    cols = jnp.arange(GRID_W)
    col_start = jnp.clip(cols - NA_COLS // 2, 0, GRID_W - NA_COLS)
    col_ok = (cols[None, :] >= col_start[:, None]) & (cols[None, :] < col_start[:, None] + NA_COLS)
    col_idx = jnp.clip(cols[None, :] - cols[:, None], -(NA_COLS - 1), NA_COLS - 1) + (NA_COLS - 1)
    pick = (jnp.arange(2 * NA_COLS - 1)[:, None, None] == col_idx[None]).astype(F32)
    by_col = jnp.einsum("hrj,jqk->hqrk", rpb.astype(F32), pick, precision=lax.Precision.HIGHEST)
    by_col = jnp.where(col_ok[None, :, None, :], by_col, MASK_VALUE)
    kinds, made = [], {}
    for geo in _attn_windows(rows):
        strips = []
        for _, phase, rfirst in geo:
            if (phase, rfirst) not in made:
                strip = by_col[:, :, rfirst:rfirst + NA_ROWS, :].reshape(NA_HEADS, GRID_W, NA_ROWS * GRID_W)
                made[phase, rfirst] = jnp.pad(strip, ((0, 0), (0, 0), (phase, ATT_WIN - NA_ROWS * GRID_W - phase)),
                                              constant_values=MASK_VALUE)
            strips.append(made[phase, rfirst])
        kinds.append(jnp.stack(strips, axis=1))
    return jnp.stack(kinds)


def _s5_tables(a_re, a_im, log_dt, b_re, b_im, c_re, c_im, d):
    t_len = S5_CHUNK
    groups = a_re.shape[1]
    steps = jnp.arange(t_len, dtype=F32)
    flip = t_len - 1 - steps

    def cmul(xr, xi, yr, yi):
        return xr * yr - xi * yi, xr * yi + xi * yr

    def powers(zr, zi, n):
        mag = jnp.exp(zr * n)
        return mag * jnp.cos(zi * n), mag * jnp.sin(zi * n)

    ar, ai = a_re.astype(F32), a_im.astype(F32)
    dt = jnp.exp(log_dt.astype(F32))[..., None]
    zr, zi = ar * dt, ai * dt
    lr, li = powers(zr, zi, 1.0)
    lr = lr - 1.0
    den = ar * ar + ai * ai
    fr, fi = (lr * ar + li * ai) / den, (li * ar - lr * ai) / den
    swap = lambda v: jnp.swapaxes(v.astype(F32), 2, 3)
    bbr, bbi = cmul(fr[:, :, None, :], fi[:, :, None, :], swap(b_re), swap(b_im))
    crt, cit = swap(c_re), swap(c_im)
    n_w = jnp.stack([flip, steps])[:, None, :, None]
    wp_r, wp_i = powers(zr[:, :, None, :], zi[:, :, None, :], n_w)
    wr, wi = cmul(wp_r[:, :, :, None, :], wp_i[:, :, :, None, :], bbr[:, :, None], bbi[:, :, None])
    wr = wr.reshape(2, groups, S5_CHUNK_WIDTH, S5_STATE)
    wi = wi.reshape(2, groups, S5_CHUNK_WIDTH, S5_STATE)
    n_e = jnp.stack([steps + 1.0, flip + 1.0])[:, None, None, :]
    ep_r, ep_i = powers(zr[..., None], zi[..., None], n_e)
    er, ei = cmul(crt[:, :, :, None, :], cit[:, :, :, None, :], ep_r[..., None], ep_i[..., None])
    er = er.reshape(2, groups, S5_STATE, S5_CHUNK_WIDTH)
    ei = ei.reshape(2, groups, S5_STATE, S5_CHUNK_WIDTH)
    lp_r, lp_i = powers(zr[..., None], zi[..., None], steps)
    xr, xi = cmul(crt[:, :, :, None, :], cit[:, :, :, None, :], lp_r[..., None], lp_i[..., None])
    lhs = jnp.concatenate([bbr, -bbi], axis=-1)
    rhs = jnp.concatenate([xr, xi], axis=2).reshape(2, groups, 2 * S5_STATE, S5_CHUNK_WIDTH)
    lag_kernels = jnp.einsum("dgck,dgkn->dgcn", lhs, rhs, precision=lax.Precision.HIGHEST)
    step_r, step_i = powers(zr, zi, float(t_len))

    w_state = jnp.concatenate([wr[0], wr[1], wi[0], wi[1]], axis=-1)
    m_state = jnp.concatenate([er[0], er[1], -ei[0], -ei[1]], axis=1)
    kf, kb = lag_kernels[0], lag_kernels[1]
    kb_rev = kb[:, :, S5_GROUP:].reshape(groups, S5_GROUP, t_len - 1, S5_GROUP)[:, :, ::-1]
    lag0 = kf[:, :, :S5_GROUP] + kb[:, :, :S5_GROUP] + d.astype(F32)[:, None, :] * jnp.eye(S5_GROUP, dtype=F32)[None]
    lags = jnp.concatenate([kb_rev.reshape(groups, S5_GROUP, -1), lag0, kf[:, :, S5_GROUP:]], axis=-1)
    return (w_state.astype(BF16), lags, m_state.astype(BF16),
            jnp.concatenate([step_r[0], step_r[1]], axis=-1), jnp.concatenate([step_i[0], step_i[1]], axis=-1))


def _lane_group_ids():
    return lax.broadcasted_iota(jnp.int32, (RELAYOUT_ROWS, LANES), 1) // S5_GROUP


def _transpose_lane_blocks(pieces, grp):
    nblk = S5_LANE_GROUPS
    rolled = []
    for r in range(nblk):
        acc = None
        for b in range(nblk):
            acc = pieces[b] if acc is None else jnp.where(grp == (b - r) % nblk, pieces[b], acc)
        rolled.append(pltpu.roll(acc, S5_GROUP * r, 1) if r else acc)
    out = []
    for k in range(nblk):
        acc = None
        for b in range(nblk):
            z = rolled[(b - k) % nblk]
            acc = z if acc is None else jnp.where(grp == b, z, acc)
        out.append(acc)
    return out


def _s5_state_kernel(u_ref, w_ref, ar_ref, ai_ref, ug_ref, h_ref, s_ref):
    lg = S5_LANE_GROUPS
    nc = ug_ref.shape[1]
    grp = _lane_group_ids()

    def regroup(rb, carry):
        r0 = pl.multiple_of(rb * RELAYOUT_ROWS, RELAYOUT_ROWS)
        for half in range(S5_CHUNK_WIDTH // LANES):
            steps = [u_ref[pl.ds(r0 * S5_CHUNK + half * S5_LANE_GROUPS + tp, RELAYOUT_ROWS, stride=S5_CHUNK), :]
                     for tp in range(S5_LANE_GROUPS)]
            for g, v in enumerate(_transpose_lane_blocks(steps, grp)):
                ug_ref[g, pl.ds(r0, RELAYOUT_ROWS), half * LANES:(half + 1) * LANES] = v.astype(BF16)
        return carry

    lax.fori_loop(0, nc // RELAYOUT_ROWS, regroup, 0, unroll=2)

    half = S5_STATE_WIDTH // 2
    for g in range(lg):
        sg = jnp.dot(ug_ref[g], w_ref[g], preferred_element_type=F32)
        s_ref[0, pl.ds(g, nc, stride=lg), :] = sg[:, :half]
        s_ref[1, pl.ds(g, nc, stride=lg), :] = sg[:, half:]

    lane = lax.broadcasted_iota(jnp.int32, (lg, half), 1)
    fwd = lane < S5_STATE
    ar = ar_ref[...]
    ai = ai_ref[...]
    rows = lambda i: pl.ds(pl.multiple_of(i * lg, lg), lg)

    def advance(i, hr, hi):
        sr = jnp.where(fwd, s_ref[0, rows(i), :], s_ref[0, rows(nc - 1 - i), :])
        si = jnp.where(fwd, s_ref[1, rows(i), :], s_ref[1, rows(nc - 1 - i), :])
        return ar * hr - ai * hi + sr, ar * hi + ai * hr + si

    def first_touch(i, carry):
        hr, hi = carry
        for c in (i, nc - 1 - i):
            h_ref[0, rows(c), :] = hr
            h_ref[1, rows(c), :] = hi
        return advance(i, hr, hi)

    def second_touch(i, carry):
        hr, hi = carry
        j = nc - 1 - i
        h_ref[0, rows(i), :] = jnp.where(fwd, hr, h_ref[0, rows(i), :])
        h_ref[1, rows(i), :] = jnp.where(fwd, hi, h_ref[1, rows(i), :])
        h_ref[0, rows(j), :] = jnp.where(fwd, h_ref[0, rows(j), :], hr)
        h_ref[1, rows(j), :] = jnp.where(fwd, h_ref[1, rows(j), :], hi)
        return advance(i, hr, hi)

    zero = jnp.zeros((lg, half), F32)
    carry = lax.fori_loop(0, nc // 2, first_touch, (zero, zero))
    lax.fori_loop(nc // 2, nc, second_touch, carry)


def _s5_states(u, w_state, a_re, a_im):
    seq, width = u.shape
    g = width // S5_GROUP
    nc = seq // S5_CHUNK
    lg = S5_LANE_GROUPS
    half = S5_STATE_WIDTH // 2
    return pl.pallas_call(
        _s5_state_kernel,
        grid=(width // LANES,),
        in_specs=[pl.BlockSpec((seq, LANES), lambda j: (0, j)),
                  pl.BlockSpec((lg, S5_CHUNK_WIDTH, S5_STATE_WIDTH), lambda j: (j, 0, 0)),
                  pl.BlockSpec((lg, half), lambda j: (j, 0)),
                  pl.BlockSpec((lg, half), lambda j: (j, 0))],
        out_specs=[pl.BlockSpec((lg, nc, S5_CHUNK_WIDTH), lambda j: (j, 0, 0)),
                   pl.BlockSpec((None, 2, nc * lg, half), lambda j: (j, 0, 0, 0))],
        out_shape=[jax.ShapeDtypeStruct((g, nc, S5_CHUNK_WIDTH), BF16),
                   jax.ShapeDtypeStruct((g // lg, 2, nc * lg, half), F32)],
        scratch_shapes=[pltpu.VMEM((2, nc * lg, half), F32)],
        compiler_params=_params(("parallel",)),
        name="s5_states",
    )(u, w_state, a_re, a_im)


def _s5_out_kernel(ug_ref, h_ref, lag_ref, m_ref, o_ref, y_ref, toe_ref):
    cw = S5_CHUNK_WIDTH
    nch = ug_ref.shape[1]
    for g in range(S5_LANE_GROUPS):
        lag = lag_ref[g]
        for s in range(S5_CHUNK):
            first = S5_GROUP * (S5_CHUNK - 1 - s)
            toe_ref[g, s * S5_GROUP:(s + 1) * S5_GROUP, :] = lag[:, first:first + cw].astype(BF16)
    for g in range(S5_LANE_GROUPS):
        hg = jnp.concatenate([h_ref[part, pl.ds(g, nch, stride=S5_LANE_GROUPS), :] for part in range(2)],
                             axis=1).astype(BF16)
        y = jnp.dot(ug_ref[g], toe_ref[g], preferred_element_type=F32)
        y = y + jnp.dot(hg, m_ref[g], preferred_element_type=F32)
        y_ref[g] = jax.nn.gelu(y, approximate=True)
    grp = _lane_group_ids()

    def regroup(rb, carry):
        r0 = pl.multiple_of(rb * RELAYOUT_ROWS, RELAYOUT_ROWS)
        for half in range(cw // LANES):
            groups = [y_ref[g, pl.ds(r0, RELAYOUT_ROWS), half * LANES:(half + 1) * LANES]
                      for g in range(S5_LANE_GROUPS)]
            for tp, v in enumerate(_transpose_lane_blocks(groups, grp)):
                t = half * S5_LANE_GROUPS + tp
                o_ref[pl.ds(r0 * S5_CHUNK + t, RELAYOUT_ROWS, stride=S5_CHUNK), :] = v
        return carry

    lax.fori_loop(0, nch // RELAYOUT_ROWS, regroup, 0, unroll=2)


def _s5_outputs(ug, h3, lags, m_state, nsplit):
    g, nc, cw = ug.shape
    nch = nc // nsplit
    lg = S5_LANE_GROUPS
    return pl.pallas_call(
        _s5_out_kernel,
        grid=(g // lg, nsplit),
        in_specs=[pl.BlockSpec((lg, nch, cw), lambda j, h: (j, h, 0)),
                  pl.BlockSpec((None, 2, nch * lg, S5_STATE_WIDTH // 2), lambda j, h: (j, 0, h, 0)),
                  pl.BlockSpec((lg,) + lags.shape[1:], lambda j, h: (j, 0, 0)),
                  pl.BlockSpec((lg, S5_STATE_WIDTH, cw), lambda j, h: (j, 0, 0))],
        out_specs=pl.BlockSpec((nch * S5_CHUNK, LANES), lambda j, h: (h, j)),
        out_shape=jax.ShapeDtypeStruct((nc * S5_CHUNK, g * S5_GROUP), F32),
        scratch_shapes=[pltpu.VMEM((lg, nch, cw), F32), pltpu.VMEM((lg, cw, cw), BF16)],
        compiler_params=_params(("parallel", "parallel")),
        name="s5_outputs",
    )(ug, h3, lags, m_state)


def _merge_kernel(a1_ref, w1_ref, ys_ref, wg_ref, bg_ref, w2_ref, g1_ref, g2_ref, o_ref, ysb_ref, z_ref, *, tg):
    @pl.when(pl.program_id(1) == 0)
    def _():
        ysb_ref[...] = ys_ref[...].astype(BF16)
        for c in range(z_ref.shape[1] // tg):
            cols = slice(c * tg, (c + 1) * tg)
            t = jnp.dot(ysb_ref[...], wg_ref[:, cols], preferred_element_type=F32) + bg_ref[:, cols]
            z_ref[:, cols] = (ys_ref[:, cols] * jax.nn.sigmoid(t)).astype(BF16)

    y1 = jnp.dot(a1_ref[...], w1_ref[...], preferred_element_type=F32)
    y2 = jnp.dot(z_ref[...], w2_ref[...], preferred_element_type=F32)
    g1 = jax.nn.sigmoid(g1_ref[...].astype(F32))
    g2 = jax.nn.sigmoid(g2_ref[...].astype(F32))
    o_ref[...] = (g1 * y1 + g2 * y2).astype(o_ref.dtype)


def _glu_merge(att, w_na, ys, w_glu, b_glu, w_s5, gates, tm, tn):
    m, k1 = att.shape
    k2 = ys.shape[1]
    n = w_na.shape[1]
    g1 = 0
    g2 = n // tn
    return pl.pallas_call(
        functools.partial(_merge_kernel, tg=tn),
        grid=(m // tm, n // tn),
        in_specs=[pl.BlockSpec((tm, k1), lambda i, j: (i, 0)),
                  pl.BlockSpec((k1, tn), lambda i, j: (0, j)),
                  pl.BlockSpec((tm, k2), lambda i, j: (i, 0)),
                  pl.BlockSpec((k2, k2), lambda i, j: (0, 0)),
                  pl.BlockSpec((1, k2), lambda i, j: (0, 0)),
                  pl.BlockSpec((k2, tn), lambda i, j: (0, j)),
                  pl.BlockSpec((tm, tn), lambda i, j: (i, g1 + j)),
                  pl.BlockSpec((tm, tn), lambda i, j: (i, g2 + j))],
        out_specs=pl.BlockSpec((tm, tn), lambda i, j: (i, j)),
        out_shape=jax.ShapeDtypeStruct((m, n), BF16),
        scratch_shapes=[pltpu.VMEM((tm, k2), BF16), pltpu.VMEM((tm, k2), BF16)],
        compiler_params=_params(("parallel", "arbitrary")),
        name="glu_gated_merge",
    )(att, w_na, ys, w_glu, b_glu, w_s5, gates, gates)


def _matmul_residual_ln(a_ref, w_ref, res_ref, g_ref, b_ref, o_ref, alpha, ob_ref=None):
    sub = a_ref.shape[0] // LN_ROW_SPLIT
    for r in range(LN_ROW_SPLIT):
        rows = slice(r * sub, (r + 1) * sub)
        y = jnp.dot(a_ref[rows, :], w_ref[...], preferred_element_type=F32)
        out = _layer_norm_rows(alpha * res_ref[rows, :] + y, g_ref[...], b_ref[...])
        o_ref[rows, :] = out
        if ob_ref is not None:
            ob_ref[rows, :] = out.astype(ob_ref.dtype)


def _outproj_ln_kernel(a_ref, w_ref, x_ref, g_ref, b_ref, h_ref, hb_ref, *, alpha):
    _matmul_residual_ln(a_ref, w_ref, x_ref, g_ref, b_ref, h_ref, alpha, hb_ref)


def _outproj_ln(a, w, x, g, b, alpha, tm):
    m, k = a.shape
    n = w.shape[1]
    return pl.pallas_call(
        functools.partial(_outproj_ln_kernel, alpha=alpha),
        grid=(m // tm,),
        in_specs=[pl.BlockSpec((tm, k), lambda i: (i, 0)),
                  pl.BlockSpec((k, n), lambda i: (0, 0)),
                  pl.BlockSpec((tm, n), lambda i: (i, 0)),
                  pl.BlockSpec((1, n), lambda i: (0, 0)),
                  pl.BlockSpec((1, n), lambda i: (0, 0))],
        out_specs=[pl.BlockSpec((tm, n), lambda i: (i, 0)),
                   pl.BlockSpec((tm, n), lambda i: (i, 0))],
        out_shape=[jax.ShapeDtypeStruct((m, n), F32), jax.ShapeDtypeStruct((m, n), BF16)],
        compiler_params=_params(("parallel",)),
        name="outproj_ln1",
    )(a, w, x, g, b)


def _ffn_act_kernel(hb_ref, wg_ref, wu_ref, o_ref):
    hb = hb_ref[...]
    gate = jnp.dot(hb, wg_ref[...], preferred_element_type=F32)
    up = jnp.dot(hb, wu_ref[...], preferred_element_type=F32)
    o_ref[...] = (jax.nn.silu(gate) * up).astype(o_ref.dtype)


def _ffn_act(hb, wg, wu, tm, tf):
    m, d = hb.shape
    f = wg.shape[1]
    return pl.pallas_call(
        _ffn_act_kernel,
        grid=(m // tm, f // tf),
        in_specs=[pl.BlockSpec((tm, d), lambda i, j: (i, 0)),
                  pl.BlockSpec((d, tf), lambda i, j: (0, j)),
                  pl.BlockSpec((d, tf), lambda i, j: (0, j))],
        out_specs=pl.BlockSpec((tm, tf), lambda i, j: (i, j)),
        out_shape=jax.ShapeDtypeStruct((m, f), BF16),
        compiler_params=_params(("parallel", "arbitrary")),
        name="swiglu_act",
    )(hb, wg, wu)


def _ffn_down_ln_kernel(a_ref, w_ref, h_ref, g_ref, b_ref, o_ref, *, alpha):
    _matmul_residual_ln(a_ref, w_ref, h_ref, g_ref, b_ref, o_ref, alpha)


def _ffn_down_ln(act, wd, h, g, b, alpha, tm):
    m, f = act.shape
    d = wd.shape[1]
    return pl.pallas_call(
        functools.partial(_ffn_down_ln_kernel, alpha=alpha),
        grid=(m // tm,),
        in_specs=[pl.BlockSpec((tm, f), lambda i: (i, 0)),
                  pl.BlockSpec((f, d), lambda i: (0, 0), pipeline_mode=pl.Buffered(1)),
                  pl.BlockSpec((tm, d), lambda i: (i, 0)),
                  pl.BlockSpec((1, d), lambda i: (0, 0)),
                  pl.BlockSpec((1, d), lambda i: (0, 0))],
        out_specs=pl.BlockSpec((tm, d), lambda i: (i, 0)),
        out_shape=jax.ShapeDtypeStruct((m, d), F32),
        compiler_params=_params(("parallel",)),
        name="swiglu_down_ln2",
    )(act, wd, h, g, b)


def _layer(x2, w_in, b_gate, na_rpb, w_na_out, s5_a_re, s5_a_im, s5_log_dt, s5_b_re, s5_b_im,
           s5_c_re, s5_c_im, s5_d, w_glu, b_glu, w_s5_out, w_out, ln1_g, ln1_b,
           w_ffn_gate, w_ffn_up, w_ffn_down, ln2_g, ln2_b, alpha):
    seq, d_model = x2.shape
    rows = seq // GRID_W
    nc = seq // S5_CHUNK
    gate_col0 = 3 * NA_WIDTH + S5_WIDTH
    in_cols = gate_col0 + 2 * d_model
    row = lambda v: v.astype(F32).reshape(1, -1)

    col_scale = jnp.where(jnp.arange(in_cols) < NA_WIDTH, NA_HEAD_DIM ** -0.5, 1.0).astype(F32).reshape(1, -1)
    col_bias = jnp.concatenate([jnp.zeros((gate_col0,), F32), b_gate.astype(F32)]).reshape(1, -1)
    qkv, u, gates = _inproj(x2, w_in.astype(BF16), col_scale, col_bias, 3 * NA_WIDTH, gate_col0, tm=1024, tn=1024)

    later_weights = [w.astype(F32) for w in (w_na_out, w_glu, w_s5_out, w_out, w_ffn_gate, w_ffn_up, w_ffn_down)]
    att, (w_na_b, w_glu_b, w_s5_b, w_out_b, w_gate_b, w_up_b, w_down_b) = _attention(
        qkv, _attention_bias(na_rpb, rows), seq, later_weights)

    w_state, lags, m_state, a_re, a_im = _s5_tables(s5_a_re, s5_a_im, s5_log_dt, s5_b_re, s5_b_im,
                                                    s5_c_re, s5_c_im, s5_d)
    ug, h_states = _s5_states(u, w_state, a_re, a_im)
    ys = _s5_outputs(ug, h_states, lags, m_state, nsplit=1)

    merged = _glu_merge(att, w_na_b, ys, w_glu_b, row(b_glu), w_s5_b, gates, tm=1024, tn=1024)
    h, hb = _outproj_ln(merged, w_out_b, x2, row(ln1_g), row(ln1_b), alpha, tm=512)
    act = _ffn_act(hb, w_gate_b, w_up_b, tm=1024, tf=512)
    return _ffn_down_ln(act, w_down_b, h, row(ln2_g), row(ln2_b), alpha, tm=512)


def kernel(x, w_in, b_gate, na_rpb, w_na_out, s5_a_re, s5_a_im, s5_log_dt, s5_b_re, s5_b_im, s5_c_re, s5_c_im, s5_d, w_glu, b_glu, w_s5_out, w_out, ln1_g, ln1_b, w_ffn_gate, w_ffn_up, w_ffn_down, ln2_g, ln2_b):
    bsz, seq, d_model = x.shape
    depth = w_in.shape[0]
    alpha = (2.0 * depth) ** 0.25
    outs = []
    for bi in range(bsz):
        xb = x[bi]
        for l in range(depth):
            xb = _layer(xb, w_in[l], b_gate[l], na_rpb[l], w_na_out[l], s5_a_re[l], s5_a_im[l], s5_log_dt[l],
                        s5_b_re[l], s5_b_im[l], s5_c_re[l], s5_c_im[l], s5_d[l], w_glu[l], b_glu[l],
                        w_s5_out[l], w_out[l], ln1_g[l], ln1_b[l], w_ffn_gate[l], w_ffn_up[l],
                        w_ffn_down[l], ln2_g[l], ln2_b[l], alpha)
        outs.append(xb)
    return jnp.stack(outs)
```
